```python
import math
import jax
import jax.numpy as jnp
from jax import lax
import numpy as np

D_MODEL = 1024
BATCH = 2
SEQ = 16384
DEPTH = 2

GRID_W = 64
CTX_LEN = 256
N_MIXERS = 2
N_MLA_LAYERS = (DEPTH + 1) // 2
N_GDN_LAYERS = DEPTH // 2
DEEPNORM_ALPHA = (2.0 * DEPTH) ** 0.25
DEEPNORM_BETA = (8.0 * DEPTH) ** -0.25
N_MOD = 6
LN_EPS = 1e-5
RMS_EPS = 1e-6

MLA_HEADS = D_MODEL // 128
MLA_Q_RANK = D_MODEL // 2
MLA_KV_RANK = D_MODEL // 4
MLA_NOPE = 128
MLA_ROPE = 64
MLA_V = 128
MLA_IN_W = MLA_Q_RANK + MLA_KV_RANK + MLA_ROPE
MLA_SCALE = (MLA_NOPE + MLA_ROPE) ** -0.5
Q_BLOCK = 128
ROPE_THETA = 10000.0

GDN_K_HEADS = D_MODEL // 128
GDN_V_HEADS = 2 * GDN_K_HEADS
GDN_DK = 128
GDN_DV = 128
GDN_KW = GDN_K_HEADS * GDN_DK
GDN_VW = GDN_V_HEADS * GDN_DV
GDN_QKV_W = 2 * GDN_KW + GDN_VW
GDN_IN_W = GDN_QKV_W + GDN_VW + 4 * GDN_V_HEADS
GDN_CONV = 5
CHUNK = 64

D_FF = int(math.ceil(8 * D_MODEL / 3 / 256)) * 256

kernel_name = 'hybrid_mla_gdn_dit_prefix_trunk'


def layer_norm(x, g, b):
    xf = x.astype(jnp.float32)
    mu = jnp.mean(xf, axis=-1, keepdims=True)
    var = jnp.mean(jnp.square(xf - mu), axis=-1, keepdims=True)
    return ((xf - mu) * lax.rsqrt(var + LN_EPS) * g + b).astype(x.dtype)


def rms_norm(x, g):
    xf = x.astype(jnp.float32)
    y = xf * lax.rsqrt(jnp.mean(jnp.square(xf), axis=-1, keepdims=True) + RMS_EPS) * g
    return y.astype(x.dtype)


def l2_norm(x):
    xf = x.astype(jnp.float32)
    return (xf * lax.rsqrt(jnp.sum(jnp.square(xf), axis=-1, keepdims=True) + RMS_EPS)).astype(x.dtype)


def modulate(h, shift, scale):
    return h * (1 + scale) + shift


def axial_rope_tables(n_tokens):
    rows = n_tokens // GRID_W
    row = jnp.broadcast_to(jnp.arange(rows)[:, None], (rows, GRID_W)).reshape(-1).astype(jnp.float32)
    col = jnp.broadcast_to(jnp.arange(GRID_W)[None, :], (rows, GRID_W)).reshape(-1).astype(jnp.float32)
    n_freq = MLA_ROPE // 4
    axis_dim = MLA_ROPE // 2
    inv_freq = ROPE_THETA ** (-(2.0 * jnp.arange(n_freq, dtype=jnp.float32)) / axis_dim)
    ang = jnp.concatenate([row[:, None] * inv_freq, col[:, None] * inv_freq], axis=-1)
    return jnp.cos(ang), jnp.sin(ang)


def apply_axial_rope(u, cos, sin):
    B, T, H, _ = u.shape
    f = MLA_ROPE // 4
    u = u.reshape(B, T, H, 2, 2, f)
    u1, u2 = u[..., 0, :], u[..., 1, :]
    cs = cos.reshape(1, T, 1, 2, f).astype(u.dtype)
    sn = sin.reshape(1, T, 1, 2, f).astype(u.dtype)
    out = jnp.stack([u1 * cs - u2 * sn, u2 * cs + u1 * sn], axis=-2)
    return out.reshape(B, T, H, MLA_ROPE)


def softmax_attend(q, k, v):
    s = jnp.einsum('bqhd,bkhd->bhqk', q, k).astype(jnp.float32) * MLA_SCALE
    p = jax.nn.softmax(s, axis=-1).astype(v.dtype)
    return jnp.einsum('bhqk,bkhd->bqhd', p, v)


def blockwise_attend(q, k, v):
    B, T, H, dq = q.shape
    qb = jnp.moveaxis(q.reshape(B, T // Q_BLOCK, Q_BLOCK, H, dq), 1, 0)
    o = lax.map(lambda blk: softmax_attend(blk, k, v), qb)
    return jnp.moveaxis(o, 0, 1).reshape(B, T, H, v.shape[-1])


def mla_project(h, w_in, q_norm, kv_norm, w_q_up, w_kv_up, rope):
    B, T, _ = h.shape
    lat = h @ w_in
    cq = rms_norm(lat[..., :MLA_Q_RANK], q_norm)
    ckv = rms_norm(lat[..., MLA_Q_RANK:MLA_Q_RANK + MLA_KV_RANK], kv_norm)
    k_rope = lat[..., MLA_Q_RANK + MLA_KV_RANK:][:, :, None, :]
    q = (cq @ w_q_up).reshape(B, T, MLA_HEADS, MLA_NOPE + MLA_ROPE)
    kv = (ckv @ w_kv_up).reshape(B, T, MLA_HEADS, MLA_NOPE + MLA_V)
    q_nope, q_rope = q[..., :MLA_NOPE], q[..., MLA_NOPE:]
    k_nope, v = kv[..., :MLA_NOPE], kv[..., MLA_NOPE:]
    if rope is not None:
        cos, sin = rope
        q_rope = apply_axial_rope(q_rope, cos, sin)
        k_rope = apply_axial_rope(k_rope, cos, sin)
    k_rope = jnp.broadcast_to(k_rope, (B, T, MLA_HEADS, MLA_ROPE))
    q = jnp.concatenate([q_nope, q_rope], axis=-1)
    k = jnp.concatenate([k_nope, k_rope], axis=-1)
    return q, k, v


def mla_mixer(h_ctx, h_lat, w_in, q_norm, kv_norm, w_q_up, w_kv_up, w_out, rope):
    q_c, k_c, v_c = mla_project(h_ctx, w_in, q_norm, kv_norm, w_q_up, w_kv_up, None)
    q_l, k_l, v_l = mla_project(h_lat, w_in, q_norm, kv_norm, w_q_up, w_kv_up, rope)
    o_c = softmax_attend(q_c, k_c, v_c)
    o_l = blockwise_attend(q_l, jnp.concatenate([k_c, k_l], axis=1), jnp.concatenate([v_c, v_l], axis=1))

    def out(o):
        B, T = o.shape[:2]
        return o.reshape(B, T, MLA_HEADS * MLA_V) @ w_out

    return out(o_c), out(o_l)


def short_conv(u, w):
    out = lax.conv_general_dilated(u, w[:, None, :], window_strides=(1,),
                                   padding=[(GDN_CONV // 2, GDN_CONV // 2)],
                                   dimension_numbers=('NWC', 'WIO', 'NWC'),
                                   feature_group_count=u.shape[-1])
    return jax.nn.silu(out)


def gated_delta_chunked(q, k, v, beta, g, s0):
    B, T, H, dk = q.shape
    dv = v.shape[-1]
    n = T // CHUNK
    f32 = jnp.float32

    def chunks(a):
        return jnp.moveaxis(a.astype(f32).reshape((B, n, CHUNK) + a.shape[2:]), 3, 1)

    qc, kc, vc, bc, gc = chunks(q), chunks(k), chunks(v), chunks(beta), chunks(g)
    gcum = jnp.cumsum(gc, axis=-1)
    idx = jnp.arange(CHUNK)
    incl = idx[:, None] >= idx[None, :]
    strict = idx[:, None] > idx[None, :]
    diff = gcum[..., :, None] - gcum[..., None, :]
    decay = jnp.where(incl, jnp.exp(jnp.where(incl, diff, 0.0)), 0.0)
    kb = kc * bc[..., None]
    lmat = jnp.where(strict, jnp.einsum('bhnid,bhnjd->bhnij', kb, kc) * decay, 0.0)
    a = lmat + jnp.eye(CHUNK, dtype=f32)
    rhs = jnp.concatenate([vc * bc[..., None], kb * jnp.exp(gcum)[..., None]], axis=-1)
    sol = lax.linalg.triangular_solve(a, rhs, left_side=True, lower=True, unit_diagonal=True)
    u, w = sol[..., :dv], sol[..., dv:]
    attn = jnp.einsum('bhnid,bhnjd->bhnij', qc, kc) * decay
    q_dec = qc * jnp.exp(gcum)[..., None]
    k_dec = kc * jnp.exp(gcum[..., -1:] - gcum)[..., None]
    g_last = jnp.exp(gcum[..., -1])
    xs = tuple(jnp.moveaxis(t, 2, 0) for t in (w, u, q_dec, k_dec, attn, g_last))

    def step(S, inp):
        w_i, u_i, qd_i, kd_i, a_i, gl_i = inp
        v_new = u_i - jnp.einsum('bhck,bhkv->bhcv', w_i, S)
        o_i = jnp.einsum('bhck,bhkv->bhcv', qd_i, S) + jnp.einsum('bhij,bhjv->bhiv', a_i, v_new)
        S = S * gl_i[..., None, None] + jnp.einsum('bhck,bhcv->bhkv', kd_i, v_new)
        return S, o_i

    s_final, o = lax.scan(step, s0, xs)
    o = jnp.transpose(o, (1, 0, 3, 2, 4)).reshape(B, T, H, dv)
    return o.astype(v.dtype), s_final


def gdn_project(h, w_in, conv_w, a_log, dt_bias):
    B, T, _ = h.shape
    proj = h @ w_in
    qkv = short_conv(proj[..., :GDN_QKV_W], conv_w)
    rep = GDN_V_HEADS // GDN_K_HEADS
    q = l2_norm(qkv[..., :GDN_KW].reshape(B, T, GDN_K_HEADS, GDN_DK)) * (GDN_DK ** -0.5)
    k = l2_norm(qkv[..., GDN_KW:2 * GDN_KW].reshape(B, T, GDN_K_HEADS, GDN_DK))
    q = jnp.repeat(q, rep, axis=2)
    k = jnp.repeat(k, rep, axis=2)
    v = qkv[..., 2 * GDN_KW:].reshape(B, T, GDN_V_HEADS, GDN_DV)
    z = proj[..., GDN_QKV_W:GDN_QKV_W + GDN_VW].reshape(B, T, GDN_V_HEADS, GDN_DV)
    ba = proj[..., GDN_QKV_W + GDN_VW:].astype(jnp.float32).reshape(B, T, 2, 2, GDN_V_HEADS)
    beta = jax.nn.sigmoid(ba[..., 0, :])
    g = -jnp.exp(a_log.astype(jnp.float32)) * jax.nn.softplus(ba[..., 1, :] + dt_bias.astype(jnp.float32))
    return q, k, v, z, beta, g


def gdn_mixer(h_ctx, h_lat, w_in, conv_w, a_log, dt_bias, norm_w, w_out):
    ctx_t = gdn_project(h_ctx, w_in, conv_w, a_log, dt_bias)
    lat_t = gdn_project(h_lat, w_in, conv_w, a_log, dt_bias)
    s0 = jnp.zeros((h_lat.shape[0], GDN_V_HEADS, GDN_DK, GDN_DV), jnp.float32)

    def one_direction(d):
        flip = (lambda a: jnp.flip(a, axis=1)) if d == 1 else (lambda a: a)

        def run(t, s_init):
            q, k, v, _, beta, g = t
            o, s_fin = gated_delta_chunked(flip(q), flip(k), flip(v), flip(beta[:, :, d]), flip(g[:, :, d]), s_init)
            return flip(o), s_fin

        o_ctx, s_ctx = run(ctx_t, s0)
        o_lat, _ = run(lat_t, s_ctx)
        return o_ctx, o_lat

    oc_f, ol_f = one_direction(0)
    oc_b, ol_b = one_direction(1)

    def finish(o, z):
        B, T = o.shape[:2]
        y = rms_norm(o, norm_w) * jax.nn.silu(z)
        return y.reshape(B, T, GDN_VW) @ w_out

    return finish(oc_f + oc_b, ctx_t[3]), finish(ol_f + ol_b, lat_t[3])


def swiglu(h, w_in, w_out):
    gate, up = jnp.split(h @ w_in, 2, axis=-1)
    return (jax.nn.silu(gate) * up) @ w_out


def setup_inputs(seed: int = 0) -> dict:
    key = jax.random.key(seed)
    ks = jax.random.split(key, 24)
    f32 = jnp.float32

    def nrm(i, shape, std=1.0):
        return std * jax.random.normal(ks[i], shape, f32)

    D = D_MODEL
    nA, nB = N_MLA_LAYERS, N_GDN_LAYERS
    a_log = jnp.log(jax.random.uniform(ks[20], (nB, 2, GDN_V_HEADS), f32, 1.0, 16.0))
    dt = jnp.exp(jax.random.uniform(ks[21], (nB, 2, GDN_V_HEADS), f32, math.log(1e-3), math.log(1e-1)))
    dt_bias = dt + jnp.log(-jnp.expm1(-dt))
    return {
        'x': nrm(0, (BATCH, SEQ, D)),
        'c': nrm(1, (BATCH, D)),
        'ctx': nrm(2, (BATCH, CTX_LEN, D)),
        'c_ctx': nrm(3, (D,)),
        'w_mod': nrm(4, (DEPTH, D, N_MOD * D), D ** -0.5),
        'b_mod': nrm(5, (DEPTH, N_MOD * D), 0.01),
        'ln1_g': 1.0 + nrm(6, (DEPTH, D), 0.02),
        'ln1_b': nrm(7, (DEPTH, D), 0.01),
        'ln2_g': 1.0 + nrm(8, (DEPTH, D), 0.02),
        'ln2_b': nrm(9, (DEPTH, D), 0.01),
        'w_ffn_in': nrm(10, (DEPTH, D, 2 * D_FF), D ** -0.5),
        'w_ffn_out': nrm(11, (DEPTH, D_FF, D), DEEPNORM_BETA * D_FF ** -0.5),
        'mla_w_in': nrm(12, (nA, D, MLA_IN_W), D ** -0.5),
        'mla_q_norm': 1.0 + nrm(13, (nA, MLA_Q_RANK), 0.02),
        'mla_kv_norm': 1.0 + nrm(14, (nA, MLA_KV_RANK), 0.02),
        'mla_w_q_up': nrm(15, (nA, MLA_Q_RANK, MLA_HEADS * (MLA_NOPE + MLA_ROPE)), MLA_Q_RANK ** -0.5),
        'mla_w_kv_up': nrm(16, (nA, MLA_KV_RANK, MLA_HEADS * (MLA_NOPE + MLA_V)), MLA_KV_RANK ** -0.5),
        'mla_w_out': nrm(17, (nA, MLA_HEADS * MLA_V, D), DEEPNORM_BETA * (MLA_HEADS * MLA_V) ** -0.5),
        'gdn_w_in': nrm(18, (nB, D, GDN_IN_W), D ** -0.5),
        'gdn_conv': nrm(19, (nB, GDN_CONV, GDN_QKV_W), GDN_CONV ** -0.5),
        'gdn_a_log': a_log,
        'gdn_dt_bias': dt_bias,
        'gdn_norm': 1.0 + nrm(22, (nB, GDN_DV), 0.02),
        'gdn_w_out': nrm(23, (nB, GDN_VW, D), DEEPNORM_BETA * GDN_VW ** -0.5),
    }


def reference(x, c, ctx, c_ctx, w_mod, b_mod, ln1_g, ln1_b, ln2_g, ln2_b, w_ffn_in, w_ffn_out,
              mla_w_in, mla_q_norm, mla_kv_norm, mla_w_q_up, mla_w_kv_up, mla_w_out,
              gdn_w_in, gdn_conv, gdn_a_log, gdn_dt_bias, gdn_norm, gdn_w_out):
    rope = axial_rope_tables(x.shape[1])
    h_lat, h_ctx = x, ctx
    for i in range(DEPTH):
        last = i == DEPTH - 1
        mod_lat = jnp.split((jax.nn.silu(c) @ w_mod[i] + b_mod[i])[:, None, :], N_MOD, axis=-1)
        mod_ctx = jnp.split(jax.nn.silu(c_ctx) @ w_mod[i] + b_mod[i], N_MOD, axis=-1)
        u_ctx = modulate(h_ctx, mod_ctx[0], mod_ctx[1])
        u_lat = modulate(h_lat, mod_lat[0], mod_lat[1])
        j = i // N_MIXERS
        if i % N_MIXERS == 0:
            y_ctx, y_lat = mla_mixer(u_ctx, u_lat, mla_w_in[j], mla_q_norm[j], mla_kv_norm[j],
                                     mla_w_q_up[j], mla_w_kv_up[j], mla_w_out[j], rope)
        else:
            y_ctx, y_lat = gdn_mixer(u_ctx, u_lat, gdn_w_in[j], gdn_conv[j], gdn_a_log[j],
                                     gdn_dt_bias[j], gdn_norm[j], gdn_w_out[j])
        h_lat = layer_norm(DEEPNORM_ALPHA * h_lat + mod_lat[2] * y_lat, ln1_g[i], ln1_b[i])
        f_lat = swiglu(modulate(h_lat, mod_lat[3], mod_lat[4]), w_ffn_in[i], w_ffn_out[i])
        h_lat = layer_norm(DEEPNORM_ALPHA * h_lat + mod_lat[5] * f_lat, ln2_g[i], ln2_b[i])
        if not last:
            h_ctx = layer_norm(DEEPNORM_ALPHA * h_ctx + mod_ctx[2] * y_ctx, ln1_g[i], ln1_b[i])
            f_ctx = swiglu(modulate(h_ctx, mod_ctx[3], mod_ctx[4]), w_ffn_in[i], w_ffn_out[i])
            h_ctx = layer_norm(DEEPNORM_ALPHA * h_ctx + mod_ctx[5] * f_ctx, ln2_g[i], ln2_b[i])
    return h_lat
```

```python
import functools
import math

import jax
import jax.numpy as jnp
from jax import lax
from jax.experimental import pallas as pl
from jax.experimental.pallas import tpu as pltpu

F32 = jnp.float32
BF16 = jnp.bfloat16

N_MOD = 6
LN_EPS = 1e-5
RMS_EPS = 1e-6
GRID_W = 64
ROPE_THETA = 10000.0

MLA_NOPE = 128
MLA_ROPE = 64
MLA_V = 128
MLA_QK = MLA_NOPE + 2 * MLA_ROPE

GDN_DK = 128
GDN_DV = 128
GDN_CONV = 5
CHUNK = 64
CONV_HALO = 8

V7X_VMEM_LIMIT = 56 * 1024 * 1024
LANES = 128


def _cparams(sem):
    return pltpu.CompilerParams(dimension_semantics=sem, vmem_limit_bytes=V7X_VMEM_LIMIT)


def _const_spec(shape):
    nd = len(shape)
    return pl.BlockSpec(shape, lambda *_: (0,) * nd, pipeline_mode=pl.Buffered(1))


def _dot(a, b):
    return jnp.dot(a, b, preferred_element_type=F32)


def _dot_nt(a, b):
    return lax.dot_general(a, b, (((1,), (1,)), ((), ())), preferred_element_type=F32)


def _split_bf16(a):
    hi = a.astype(BF16)
    lo = (a - hi.astype(F32)).astype(BF16)
    return hi, lo


def _dot3(a, b):
    ah, al = _split_bf16(a)
    bh, bl = _split_bf16(b)
    return _dot(jnp.concatenate([ah, ah, al], axis=1), jnp.concatenate([bh, bl, bh], axis=0))


def _silu(x):
    return x * jax.nn.sigmoid(x)


def _layer_norm(x, g, b):
    mu = jnp.mean(x, axis=-1, keepdims=True)
    xc = x - mu
    var = jnp.mean(xc * xc, axis=-1, keepdims=True)
    return xc * lax.rsqrt(var + LN_EPS) * g + b


def _rms_norm(x, g):
    return x * lax.rsqrt(jnp.mean(x * x, axis=-1, keepdims=True) + RMS_EPS) * g


def _mod_kernel(c_ref, w_ref, b_ref, o_ref):
    a = _silu(c_ref[...]).astype(BF16)
    o_ref[0] = _dot(a, w_ref[0].astype(BF16)) + b_ref[0]


def _mod_call(cc, w_mod, b_mod):
    depth, d, n = w_mod.shape
    tn = 1536
    assert n % tn == 0
    return pl.pallas_call(
        _mod_kernel,
        grid=(depth, n // tn),
        in_specs=[
            pl.BlockSpec((8, d), lambda l, j: (0, 0)),
            pl.BlockSpec((1, d, tn), lambda l, j: (l, 0, j)),
            pl.BlockSpec((1, 1, tn), lambda l, j: (l, 0, j)),
        ],
        out_specs=pl.BlockSpec((1, 8, tn), lambda l, j: (l, 0, j)),
        out_shape=jax.ShapeDtypeStruct((depth, 8, n), F32),
        compiler_params=_cparams(("parallel", "parallel")),
        name="mod",
    )(cc, w_mod, b_mod.reshape(depth, 1, n))


def _mod_spec(d, j):
    return pl.BlockSpec((1, 1, d), lambda b, i: (b, 0, j))


def _mla_proj_kernel(h_ref, sh_ref, sc_ref, win_ref, qn_ref, kvn_ref, wq_ref, wkv_ref,
                     qm_ref, km_ref, q_ref, k_ref, v_ref, *, heads, q_rank, kv_rank):
    x = h_ref[0]
    u = x * (1.0 + sc_ref[0]) + sh_ref[0]
    lat = _dot(u.astype(BF16), win_ref[...])
    cq = _rms_norm(lat[:, :q_rank], qn_ref[...])
    ckv = _rms_norm(lat[:, q_rank:q_rank + kv_rank], kvn_ref[...])
    kr2 = lat[:, q_rank + kv_rank:]
    qall = _dot(cq.astype(BF16), wq_ref[...])
    kvall = _dot(ckv.astype(BF16), wkv_ref[...])
    qm = qm_ref[...]
    t = kr2 * km_ref[...]
    krot2 = (t + pltpu.roll(t, MLA_ROPE, axis=1)).astype(BF16)
    nk = heads * MLA_NOPE
    for hh in range(heads):
        q_ref[0, hh] = (qall[:, hh * MLA_QK:(hh + 1) * MLA_QK] * qm).astype(BF16)
        k_ref[0, hh, :, :MLA_NOPE] = kvall[:, hh * MLA_NOPE:(hh + 1) * MLA_NOPE].astype(BF16)
        k_ref[0, hh, :, MLA_NOPE:] = krot2
        v_ref[0, hh] = kvall[:, nk + hh * MLA_V:nk + (hh + 1) * MLA_V].astype(BF16)


def _mla_proj_call(h, mod, w_in, qn, kvn, wq, wkv, qm, km, *, tm, heads):
    b, t, d = h.shape
    q_rank, kv_rank = qn.shape[1], kvn.shape[1]
    assert t % tm == 0
    kern = functools.partial(_mla_proj_kernel, heads=heads, q_rank=q_rank, kv_rank=kv_rank)
    return pl.pallas_call(
        kern,
        grid=(b, t // tm),
        in_specs=[
            pl.BlockSpec((1, tm, d), lambda bb, i: (bb, i, 0)),
            _mod_spec(d, 0), _mod_spec(d, 1),
            _const_spec(w_in.shape), _const_spec(qn.shape), _const_spec(kvn.shape),
            _const_spec(wq.shape), _const_spec(wkv.shape),
            pl.BlockSpec((tm, MLA_QK), lambda bb, i: (i, 0)),
            pl.BlockSpec((tm, 2 * MLA_ROPE), lambda bb, i: (i, 0)),
        ],
        out_specs=[
            pl.BlockSpec((1, heads, tm, MLA_QK), lambda bb, i: (bb, 0, i, 0)),
            pl.BlockSpec((1, heads, tm, MLA_QK), lambda bb, i: (bb, 0, i, 0)),
            pl.BlockSpec((1, heads, tm, MLA_V), lambda bb, i: (bb, 0, i, 0)),
        ],
        out_shape=[
            jax.ShapeDtypeStruct((b, heads, t, MLA_QK), BF16),
            jax.ShapeDtypeStruct((b, heads, t, MLA_QK), BF16),
            jax.ShapeDtypeStruct((b, heads, t, MLA_V), BF16),
        ],
        compiler_params=_cparams(("parallel", "parallel")),
        name="mla_proj",
    )(h, mod, mod, w_in, qn, kvn, wq, wkv, qm, km)


def _attn_kernel(*refs, tk, n_lat):
    if n_lat:
        q_ref, kc_ref, vc_ref, kl_ref, vl_ref, o_ref, m_ref, l_ref, acc_ref = refs
    else:
        q_ref, kc_ref, vc_ref, o_ref, m_ref, l_ref, acc_ref = refs
    q = q_ref[0, 0]

    def update(kblk, vblk, first):
        s = _dot_nt(q, kblk)
        smax = jnp.max(s, axis=-1, keepdims=True)
        if first:
            m_new = smax
            p = jnp.exp2(s - m_new)
            l_ref[...] = jnp.sum(p, axis=-1, keepdims=True)
            acc_ref[...] = _dot(p.astype(BF16), vblk)
        else:
            m_old = m_ref[...]
            m_new = jnp.maximum(m_old, smax)
            a = jnp.exp2(m_old - m_new)
            p = jnp.exp2(s - m_new)
            l_ref[...] = a * l_ref[...] + jnp.sum(p, axis=-1, keepdims=True)
            acc_ref[...] = a * acc_ref[...] + _dot(p.astype(BF16), vblk)
        m_ref[...] = m_new

    update(kc_ref[0, 0], vc_ref[0, 0], True)
    if n_lat:
        def body(j, carry):
            off = pl.multiple_of(j * tk, tk)
            update(kl_ref[0, 0, pl.ds(off, tk), :], vl_ref[0, 0, pl.ds(off, tk), :], False)
            return carry
        lax.fori_loop(0, n_lat, body, 0)
    o_ref[0] = (acc_ref[...] / l_ref[...]).astype(o_ref.dtype)


def _attn_call(q, k_ctx, v_ctx, k_lat=None, v_lat=None, *, tq, tk):
    b, h, t, dqk = q.shape
    tc = k_ctx.shape[2]
    dv = v_ctx.shape[3]
    assert t % tq == 0
    n_lat = 0
    in_specs = [
        pl.BlockSpec((1, 1, tq, dqk), lambda bb, hh, i: (bb, hh, i, 0)),
        pl.BlockSpec((1, 1, tc, dqk), lambda bb, hh, i: (bb, hh, 0, 0)),
        pl.BlockSpec((1, 1, tc, dv), lambda bb, hh, i: (bb, hh, 0, 0)),
    ]
    args = [q, k_ctx, v_ctx]
    if k_lat is not None:
        tl = k_lat.shape[2]
        assert tl % tk == 0
        n_lat = tl // tk
        in_specs += [
            pl.BlockSpec((1, 1, tl, dqk), lambda bb, hh, i: (bb, hh, 0, 0)),
            pl.BlockSpec((1, 1, tl, dv), lambda bb, hh, i: (bb, hh, 0, 0)),
        ]
        args += [k_lat, v_lat]
    kern = functools.partial(_attn_kernel, tk=tk, n_lat=n_lat)
    return pl.pallas_call(
        kern,
        grid=(b, h, t // tq),
        in_specs=in_specs,
        out_specs=pl.BlockSpec((1, tq, dv), lambda bb, hh, i: (bb, i, hh)),
        out_shape=jax.ShapeDtypeStruct((b, t, h * dv), BF16),
        scratch_shapes=[
            pltpu.VMEM((tq, 1), F32),
            pltpu.VMEM((tq, 1), F32),
            pltpu.VMEM((tq, dv), F32),
        ],
        compiler_params=_cparams(("parallel", "parallel", "arbitrary")),
        name="attn_lat" if n_lat else "attn_ctx",
    )(*args)


def _post_tail(y_in, h_ref, g1_ref, sh2_ref, sc2_ref, g2_ref, wout_ref, l1g_ref, l1b_ref,
               wfi_ref, wfo_ref, l2g_ref, l2b_ref, o_ref, *, alpha, d_ff):
    y = _dot(y_in, wout_ref[...])
    h1 = _layer_norm(alpha * h_ref[0] + g1_ref[0] * y, l1g_ref[...], l1b_ref[...])
    u2 = h1 * (1.0 + sc2_ref[0]) + sh2_ref[0]
    gu = _dot(u2.astype(BF16), wfi_ref[...])
    act = _silu(gu[:, :d_ff]) * gu[:, d_ff:]
    f = _dot(act.astype(BF16), wfo_ref[...])
    o_ref[0] = _layer_norm(alpha * h1 + g2_ref[0] * f, l2g_ref[...], l2b_ref[...])


def _post_mla_kernel(a_ref, *rest, alpha, d_ff):
    _post_tail(a_ref[0], *rest, alpha=alpha, d_ff=d_ff)


def _post_gdn_kernel(of_ref, ob_ref, z_ref, nw_ref, *rest, alpha, d_ff, k_heads):
    nw = nw_ref[...]
    parts = []
    for kh in range(k_heads):
        o2 = of_ref[0, kh] + ob_ref[0, kh]
        for v in range(2):
            o = o2[:, v * GDN_DV:(v + 1) * GDN_DV]
            vh = 2 * kh + v
            z = z_ref[0, :, vh * GDN_DV:(vh + 1) * GDN_DV]
            parts.append((_rms_norm(o, nw) * _silu(z)).astype(BF16))
    _post_tail(jnp.concatenate(parts, axis=1), *rest, alpha=alpha, d_ff=d_ff)


def _post_call(mixer_in, h, mod, w_out, l1g, l1b, wfi, wfo, l2g, l2b, *, tm, alpha, gdn_norm=None):
    b, t, d = h.shape
    d_ff = wfo.shape[0]
    assert t % tm == 0
    row = lambda w: pl.BlockSpec((1, tm, w), lambda bb, i: (bb, i, 0))
    if gdn_norm is None:
        (a,) = mixer_in
        kern = functools.partial(_post_mla_kernel, alpha=alpha, d_ff=d_ff)
        head_specs = [row(a.shape[2])]
        head_args = [a]
    else:
        o_f, o_b, z = mixer_in
        kh = o_f.shape[1]
        kern = functools.partial(_post_gdn_kernel, alpha=alpha, d_ff=d_ff, k_heads=kh)
        hm = pl.BlockSpec((1, kh, tm, 2 * GDN_DV), lambda bb, i: (bb, 0, i, 0))
        head_specs = [hm, hm, row(z.shape[2]), _const_spec(gdn_norm.shape)]
        head_args = [o_f, o_b, z, gdn_norm]
    return pl.pallas_call(
        kern,
        grid=(b, t // tm),
        in_specs=head_specs + [
            row(d),
            _mod_spec(d, 2), _mod_spec(d, 3), _mod_spec(d, 4), _mod_spec(d, 5),
            _const_spec(w_out.shape), _const_spec(l1g.shape), _const_spec(l1b.shape),
            _const_spec(wfi.shape), _const_spec(wfo.shape),
            _const_spec(l2g.shape), _const_spec(l2b.shape),
        ],
        out_specs=row(d),
        out_shape=jax.ShapeDtypeStruct((b, t, d), F32),
        compiler_params=_cparams(("parallel", "parallel")),
        name="post_mla" if gdn_norm is None else "post_gdn",
    )(*head_args, h, mod, mod, mod, mod, w_out, l1g, l1b, wfi, wfo, l2g, l2b)


def _gdn_proj_kernel(h_ref, sh_ref, sc_ref, w_ref, qkv_ref, z_ref, ba_ref, *, qkv_w, z_w):
    u = h_ref[0] * (1.0 + sc_ref[0]) + sh_ref[0]
    p = _dot(u.astype(BF16), w_ref[...])
    qkv_ref[0] = p[:, :qkv_w]
    z_ref[0] = p[:, qkv_w:qkv_w + z_w]
    ba_ref[0] = p[:, qkv_w + z_w:]


def _gdn_proj_call(h, mod, w, *, tm, qkv_w, z_w):
    b, t, d = h.shape
    ba_w = w.shape[1] - qkv_w - z_w
    assert t % tm == 0
    row = lambda wd: pl.BlockSpec((1, tm, wd), lambda bb, i: (bb, i, 0))
    kern = functools.partial(_gdn_proj_kernel, qkv_w=qkv_w, z_w=z_w)
    return pl.pallas_call(
        kern,
        grid=(b, t // tm),
        in_specs=[row(d), _mod_spec(d, 0), _mod_spec(d, 1), _const_spec(w.shape)],
        out_specs=[row(qkv_w), row(z_w), row(ba_w)],
        out_shape=[
            jax.ShapeDtypeStruct((b, t, qkv_w), F32),
            jax.ShapeDtypeStruct((b, t, z_w), F32),
            jax.ShapeDtypeStruct((b, t, ba_w), F32),
        ],
        compiler_params=_cparams(("parallel", "parallel")),
        name="gdn_proj",
    )(h, mod, mod, w)


def _chunk_cumsum(g, suffix):
    n = g.shape[0]
    pos = lax.broadcasted_iota(jnp.int32, g.shape, 0) % CHUNK
    s = 1
    while s < CHUNK:
        if suffix:
            shifted = pltpu.roll(g, n - s, axis=0)
            ok = pos < CHUNK - s
        else:
            shifted = pltpu.roll(g, s, axis=0)
            ok = pos >= s
        g = g + jnp.where(ok, shifted, 0.0)
        s *= 2
    return g


def _gdn_conv_kernel(prev_ref, cur_ref, next_ref, ba_ref, cw_ref, al_ref, dtb_ref,
                     q_ref, k_ref, v_ref, gt_ref, xe_ref, *, k_heads, tm):
    i = pl.program_id(1)
    n_i = pl.num_programs(1)
    kw = k_heads * GDN_DK
    xe_ref[0:CONV_HALO] = jnp.where(i > 0, prev_ref[0], 0.0)
    xe_ref[CONV_HALO:CONV_HALO + tm] = cur_ref[0]
    xe_ref[CONV_HALO + tm:] = jnp.where(i < n_i - 1, next_ref[0], 0.0)
    half = GDN_CONV // 2
    acc = None
    for j in range(GDN_CONV):
        term = xe_ref[pl.ds(CONV_HALO - half + j, tm), :] * cw_ref[j:j + 1, :]
        acc = term if acc is None else acc + term
    y = _silu(acc)
    for kh in range(k_heads):
        qh = y[:, kh * GDN_DK:(kh + 1) * GDN_DK]
        q_ref[0, kh] = (qh * lax.rsqrt(jnp.sum(qh * qh, axis=-1, keepdims=True) + RMS_EPS)
                        * (GDN_DK ** -0.5)).astype(BF16)
        kk = y[:, kw + kh * GDN_DK:kw + (kh + 1) * GDN_DK]
        k_ref[0, kh] = (kk * lax.rsqrt(jnp.sum(kk * kk, axis=-1, keepdims=True) + RMS_EPS)).astype(BF16)
        v_ref[0, kh] = y[:, 2 * kw + kh * 2 * GDN_DV:2 * kw + (kh + 1) * 2 * GDN_DV].astype(BF16)
    ba = ba_ref[0]
    lane = lax.broadcasted_iota(jnp.int32, ba.shape, 1) % LANES
    beta = jax.nn.sigmoid(ba)
    g = -jnp.exp(al_ref[...]) * jax.nn.softplus(ba + dtb_ref[...])
    g = jnp.where((lane >= 4) & (lane < 8), g, 0.0)
    gcum = jnp.where(lane < 6, _chunk_cumsum(g, False), _chunk_cumsum(g, True))
    gates = jnp.where(lane < 4, beta, gcum)
    for kh in range(k_heads):
        gt_ref[0, kh] = gates[:, kh * LANES:(kh + 1) * LANES]


def _gdn_conv_call(qkv, ba, conv_w, a_log_ext, dtb_ext, *, tm, k_heads):
    b, t, c = qkv.shape
    assert t % tm == 0 and tm % CHUNK == 0 and tm % CONV_HALO == 0
    r = tm // CONV_HALO
    nblk8 = t // CONV_HALO
    kern = functools.partial(_gdn_conv_kernel, k_heads=k_heads, tm=tm)
    hm = lambda w: pl.BlockSpec((1, k_heads, tm, w), lambda bb, i: (bb, 0, i, 0))
    return pl.pallas_call(
        kern,
        grid=(b, t // tm),
        in_specs=[
            pl.BlockSpec((1, CONV_HALO, c), lambda bb, i: (bb, jnp.maximum(i * r - 1, 0), 0)),
            pl.BlockSpec((1, tm, c), lambda bb, i: (bb, i, 0)),
            pl.BlockSpec((1, CONV_HALO, c), lambda bb, i: (bb, jnp.minimum((i + 1) * r, nblk8 - 1), 0)),
            pl.BlockSpec((1, tm, ba.shape[2]), lambda bb, i: (bb, i, 0)),
            _const_spec(conv_w.shape), _const_spec(a_log_ext.shape), _const_spec(dtb_ext.shape),
        ],
        out_specs=[hm(GDN_DK), hm(GDN_DK), hm(2 * GDN_DV), hm(LANES)],
        out_shape=[
            jax.ShapeDtypeStruct((b, k_heads, t, GDN_DK), BF16),
            jax.ShapeDtypeStruct((b, k_heads, t, GDN_DK), BF16),
            jax.ShapeDtypeStruct((b, k_heads, t, 2 * GDN_DV), BF16),
            jax.ShapeDtypeStruct((b, k_heads, t, LANES), F32),
        ],
        scratch_shapes=[pltpu.VMEM((tm + 2 * CONV_HALO, c), F32)],
        compiler_params=_cparams(("parallel", "parallel")),
        name="gdn_conv",
    )(qkv, qkv, qkv, ba, conv_w, a_log_ext, dtb_ext)


def _gdn_chain(qc, kc, vv, gt, gtt, kk, qk, slot, backward, s_ref, s_idx):
    c = qc.shape[0]
    row = lax.broadcasted_iota(jnp.int32, (c, c), 0)
    col = lax.broadcasted_iota(jnp.int32, (c, c), 1)
    if backward:
        incl, strict = row <= col, row < col
    else:
        incl, strict = row >= col, row > col
    beta_col = jnp.broadcast_to(gt[:, slot:slot + 1], (c, LANES))
    g_col = jnp.broadcast_to(gt[:, 4 + slot:5 + slot], (c, LANES))
    g_row = gtt[4 + slot:5 + slot, :]
    diff = g_col[:, :c] - g_row
    decay = jnp.where(incl, jnp.exp(jnp.where(incl, diff, 0.0)), 0.0)
    lmat = jnp.where(strict, kk * beta_col[:, :c] * decay, 0.0)
    attn = qk * decay
    eg_col = jnp.exp(g_col)
    blk_r, blk_c = row, col
    tinv = None
    b = 1
    while b < c:
        pair = (blk_r // 2 == blk_c // 2) & (blk_r != blk_c)
        o_l = jnp.where(pair, lmat, 0.0)
        if tinv is None:
            tinv = jnp.where(row == col, 1.0, 0.0) - o_l
        else:
            tinv = tinv - _dot3(tinv, _dot3(o_l, tinv))
        blk_r, blk_c = blk_r // 2, blk_c // 2
        b *= 2
    rhs = jnp.concatenate([vv.astype(F32) * beta_col, kc.astype(F32) * (beta_col * eg_col)], axis=1)
    x = _dot3(tinv, rhs)
    u = x[:, :GDN_DV]
    w = x[:, GDN_DV:]
    s_old = s_ref[s_idx]
    wq = _dot(jnp.concatenate([w.astype(BF16), qc], axis=0), s_old.astype(BF16))
    v_new = u - wq[:c]
    g_last = g_col[0:1, :] if backward else g_col[c - 1:c, :]
    k_dec = kc.astype(F32) * jnp.exp(g_last - g_col)
    lhs = jnp.concatenate([attn.astype(BF16), k_dec.T.astype(BF16)], axis=0)
    ak = _dot(lhs, v_new.astype(BF16))
    s_ref[s_idx] = s_old * jnp.exp(g_last) + ak[c:]
    return eg_col * wq[c:] + ak[:c]


def _gdn_scan_kernel(qf_ref, kf_ref, vf_ref, gf_ref, qb_ref, kb_ref, vb_ref, gb_ref, s0_ref,
                     of_ref, ob_ref, sfin_ref, s_ref, *, hg):
    n = pl.program_id(2)

    @pl.when(n == 0)
    def _():
        s_ref[...] = s0_ref[0]

    for j in range(hg):
        for d, (q_ref, k_ref, v_ref, g_ref, o_ref) in enumerate(
                ((qf_ref, kf_ref, vf_ref, gf_ref, of_ref), (qb_ref, kb_ref, vb_ref, gb_ref, ob_ref))):
            qc = q_ref[0, j]
            kc = k_ref[0, j]
            v2 = v_ref[0, j]
            gt = g_ref[0, j]
            gtt = gt.T
            kq = _dot_nt(jnp.concatenate([kc, qc], axis=0), kc)
            c = qc.shape[0]
            kk, qk = kq[:c], kq[c:]
            outs = []
            for v in range(2):
                outs.append(_gdn_chain(qc, kc, v2[:, v * GDN_DV:(v + 1) * GDN_DV], gt, gtt, kk, qk,
                                       2 * d + v, d == 1, s_ref, j * 4 + 2 * d + v))
            o_ref[0, j] = jnp.concatenate(outs, axis=1)

    @pl.when(n == pl.num_programs(2) - 1)
    def _():
        sfin_ref[0] = s_ref[...]


def _gdn_scan_call(q, k, v, gates, s0, *, hg):
    b, kh, t, _ = q.shape
    assert kh % hg == 0 and t % CHUNK == 0
    nc = t // CHUNK
    ng = kh // hg
    fwd = lambda w: pl.BlockSpec((1, hg, CHUNK, w), lambda bb, g, n: (bb, g, n, 0))
    bwd = lambda w: pl.BlockSpec((1, hg, CHUNK, w), lambda bb, g, n: (bb, g, nc - 1 - n, 0))
    st = pl.BlockSpec((1, hg * 4, GDN_DK, GDN_DV), lambda bb, g, n: (bb, g, 0, 0))
    kern = functools.partial(_gdn_scan_kernel, hg=hg)
    return pl.pallas_call(
        kern,
        grid=(b, ng, nc),
        in_specs=[fwd(GDN_DK), fwd(GDN_DK), fwd(2 * GDN_DV), fwd(LANES),
                  bwd(GDN_DK), bwd(GDN_DK), bwd(2 * GDN_DV), bwd(LANES), st],
        out_specs=[fwd(2 * GDN_DV), bwd(2 * GDN_DV), st],
        out_shape=[
            jax.ShapeDtypeStruct((b, kh, t, 2 * GDN_DV), F32),
            jax.ShapeDtypeStruct((b, kh, t, 2 * GDN_DV), F32),
            jax.ShapeDtypeStruct(s0.shape, F32),
        ],
        scratch_shapes=[pltpu.VMEM((hg * 4, GDN_DK, GDN_DV), F32)],
        compiler_params=_cparams(("parallel", "parallel", "arbitrary")),
        name="gdn_scan",
    )(q, k, v, gates, q, k, v, gates, s0)


def _rope_tables(t, q_scale):
    pos = jnp.arange(t)
    row = (pos // GRID_W).astype(F32)
    col = (pos % GRID_W).astype(F32)
    n_freq = MLA_ROPE // 4
    inv_freq = ROPE_THETA ** (-(2.0 * jnp.arange(n_freq, dtype=F32)) / (MLA_ROPE // 2))
    ar = row[:, None] * inv_freq
    ac = col[:, None] * inv_freq
    cs = jnp.concatenate([jnp.cos(ar), jnp.cos(ar), jnp.cos(ac), jnp.cos(ac)], axis=1)
    sn = jnp.concatenate([-jnp.sin(ar), jnp.sin(ar), -jnp.sin(ac), jnp.sin(ac)], axis=1)
    km = jnp.concatenate([cs, sn], axis=1)
    qm = q_scale * jnp.concatenate([jnp.ones((t, MLA_NOPE), F32), cs, sn], axis=1)
    return qm, km


def _no_rope_tables(t, q_scale):
    one = jnp.ones((t, MLA_ROPE), F32)
    zero = jnp.zeros((t, MLA_ROPE), F32)
    km = jnp.concatenate([one, zero], axis=1)
    qm = q_scale * jnp.concatenate([jnp.ones((t, MLA_NOPE), F32), one, zero], axis=1)
    return qm, km


def _swap_perm():
    f = MLA_ROPE // 4
    return jnp.array(list(range(f, 2 * f)) + list(range(0, f)) + list(range(3 * f, 4 * f)) + list(range(2 * f, 3 * f)))


def _mla_layer(h_lat, h_ctx, mod_lat, mod_ctx, ln, w_in, q_norm, kv_norm, w_q_up, w_kv_up, w_out, *, alpha, last):
    bsz, seq, d = h_lat.shape
    tctx = h_ctx.shape[1]
    heads = d // 128
    q_rank, kv_rank = q_norm.shape[0], kv_norm.shape[0]
    tm_lat, tm_ctx = min(256, seq), min(256, tctx)
    q_scale = (MLA_NOPE + MLA_ROPE) ** -0.5 * math.log2(math.e)
    perm = _swap_perm()
    rope_cols = w_in[:, q_rank + kv_rank:]
    w_in_ext = jnp.concatenate([w_in, rope_cols[:, perm]], axis=1).astype(BF16)
    wq = w_q_up.reshape(q_rank, heads, MLA_NOPE + MLA_ROPE)
    wq_ext = jnp.concatenate([wq, wq[:, :, MLA_NOPE:][:, :, perm]], axis=2)
    wq_ext = wq_ext.reshape(q_rank, heads * MLA_QK).astype(BF16)
    wkv = w_kv_up.reshape(kv_rank, heads, MLA_NOPE + MLA_V)
    wkv_ext = jnp.concatenate([wkv[:, :, :MLA_NOPE].reshape(kv_rank, -1),
                               wkv[:, :, MLA_NOPE:].reshape(kv_rank, -1)], axis=1).astype(BF16)
    qn, kvn = q_norm.reshape(1, -1), kv_norm.reshape(1, -1)
    qm_l, km_l = _rope_tables(seq, q_scale)
    qm_c, km_c = _no_rope_tables(tctx, q_scale)
    q_l, k_l, v_l = _mla_proj_call(h_lat, mod_lat, w_in_ext, qn, kvn, wq_ext, wkv_ext, qm_l, km_l,
                                   tm=tm_lat, heads=heads)
    q_c, k_c, v_c = _mla_proj_call(h_ctx, mod_ctx, w_in_ext, qn, kvn, wq_ext, wkv_ext, qm_c, km_c,
                                   tm=tm_ctx, heads=heads)
    o_l = _attn_call(q_l, k_c, v_c, k_l, v_l, tq=min(1024, seq), tk=min(512, seq))
    w_out = w_out.astype(BF16)
    h_lat_new = _post_call((o_l,), h_lat, mod_lat, w_out, *ln, tm=tm_lat, alpha=alpha)
    if not last:
        o_c = _attn_call(q_c, k_c, v_c, tq=tm_ctx, tk=tm_ctx)
        h_ctx = _post_call((o_c,), h_ctx, mod_ctx, w_out, *ln, tm=tm_ctx, alpha=alpha)
    return h_lat_new, h_ctx


def _gdn_layer(h_lat, h_ctx, mod_lat, mod_ctx, ln, w_in, conv_w, a_log, dt_bias, norm_w, w_out, *, alpha, last):
    bsz, seq, d = h_lat.shape
    tctx = h_ctx.shape[1]
    k_heads = d // 128
    v_heads = 2 * k_heads
    kw, vw = k_heads * GDN_DK, v_heads * GDN_DV
    qkv_w = 2 * kw + vw
    tm_lat, tm_ctx = min(256, seq), min(256, tctx)
    w_ba = w_in[:, qkv_w + vw:].reshape(d, 2, 2, k_heads, 2)
    w_ba = jnp.transpose(w_ba, (0, 3, 2, 1, 4)).reshape(d, k_heads, 8)
    w_ba = jnp.pad(w_ba, ((0, 0), (0, 0), (0, LANES - 8))).reshape(d, k_heads * LANES)
    w_ext = jnp.concatenate([w_in[:, :qkv_w + vw], w_ba], axis=1).astype(BF16)

    def gate_lanes(p):
        p = jnp.transpose(p.astype(F32).reshape(2, k_heads, 2), (1, 0, 2)).reshape(k_heads, 4)
        p = jnp.pad(p, ((0, 0), (4, LANES - 8)))
        return p.reshape(1, k_heads * LANES)

    al_ext, dtb_ext = gate_lanes(a_log), gate_lanes(dt_bias)
    s = jnp.zeros((bsz, k_heads * 4, GDN_DK, GDN_DV), F32)
    streams = []
    for h_in, m_in, tm in ((h_ctx, mod_ctx, tm_ctx), (h_lat, mod_lat, tm_lat)):
        qkv, z, ba = _gdn_proj_call(h_in, m_in, w_ext, tm=tm, qkv_w=qkv_w, z_w=vw)
        qh, kh, vh, gates = _gdn_conv_call(qkv, ba, conv_w, al_ext, dtb_ext, tm=tm, k_heads=k_heads)
        o_f, o_b, s = _gdn_scan_call(qh, kh, vh, gates, s, hg=2)
        streams.append((o_f, o_b, z))
    w_out = w_out.astype(BF16)
    nw = norm_w.reshape(1, -1)
    h_lat_new = _post_call(streams[1], h_lat, mod_lat, w_out, *ln, tm=tm_lat, alpha=alpha, gdn_norm=nw)
    if not last:
        h_ctx = _post_call(streams[0], h_ctx, mod_ctx, w_out, *ln, tm=tm_ctx, alpha=alpha, gdn_norm=nw)
    return h_lat_new, h_ctx


def kernel(x, c, ctx, c_ctx, w_mod, b_mod, ln1_g, ln1_b, ln2_g, ln2_b, w_ffn_in, w_ffn_out, mla_w_in, mla_q_norm, mla_kv_norm, mla_w_q_up, mla_w_kv_up, mla_w_out, gdn_w_in, gdn_conv, gdn_a_log, gdn_dt_bias, gdn_norm, gdn_w_out):
    bsz, _, d = x.shape
    depth = w_mod.shape[0]
    alpha = (2.0 * depth) ** 0.25

    cc = jnp.zeros((8, d), F32).at[:bsz].set(c).at[bsz].set(c_ctx)
    mod = _mod_call(cc, w_mod, b_mod)

    row = lambda a, i: a[i].reshape(1, -1)
    h_lat, h_ctx = x, ctx
    for i in range(depth):
        last = i == depth - 1
        mod_lat = mod[i, :bsz].reshape(bsz, 1, N_MOD * d)
        mod_ctx = jnp.broadcast_to(mod[i, bsz].reshape(1, 1, N_MOD * d), (bsz, 1, N_MOD * d))
        ln = (row(ln1_g, i), row(ln1_b, i), w_ffn_in[i].astype(BF16), w_ffn_out[i].astype(BF16),
              row(ln2_g, i), row(ln2_b, i))
        j = i // 2
        if i % 2 == 0:
            h_lat, h_ctx = _mla_layer(h_lat, h_ctx, mod_lat, mod_ctx, ln, mla_w_in[j], mla_q_norm[j],
                                      mla_kv_norm[j], mla_w_q_up[j], mla_w_kv_up[j], mla_w_out[j],
                                      alpha=alpha, last=last)
        else:
            h_lat, h_ctx = _gdn_layer(h_lat, h_ctx, mod_lat, mod_ctx, ln, gdn_w_in[j], gdn_conv[j],
                                      gdn_a_log[j], gdn_dt_bias[j], gdn_norm[j], gdn_w_out[j],
                                      alpha=alpha, last=last)
    return h_lat
```

```python
import functools
import math

import jax
import jax.numpy as jnp
from jax import lax
from jax.experimental import pallas as pl
from jax.experimental.pallas import tpu as pltpu

F32 = jnp.float32
BF16 = jnp.bfloat16

N_MOD = 6
LN_EPS = 1e-5
RMS_EPS = 1e-6
GRID_W = 64
ROPE_THETA = 10000.0

MLA_NOPE = 128
MLA_ROPE = 64
MLA_V = 128
MLA_QK = MLA_NOPE + 2 * MLA_ROPE

GDN_DK = 128
GDN_DV = 128
GDN_CONV = 5
CHUNK = 64
CONV_HALO = 8

V7X_VMEM_LIMIT = 56 * 1024 * 1024
LANES = 128


def _cparams(sem):
    return pltpu.CompilerParams(dimension_semantics=sem, vmem_limit_bytes=V7X_VMEM_LIMIT)


def _const_spec(shape):
    nd = len(shape)
    return pl.BlockSpec(shape, lambda *_: (0,) * nd, pipeline_mode=pl.Buffered(1))


def _dot(a, b):
    return jnp.dot(a, b, preferred_element_type=F32)


def _dot_nt(a, b):
    return lax.dot_general(a, b, (((1,), (1,)), ((), ())), preferred_element_type=F32)


def _split_bf16(a):
    hi = a.astype(BF16)
    lo = (a - hi.astype(F32)).astype(BF16)
    return hi, lo


def _dot3(a, b):
    ah, al = _split_bf16(a)
    bh, bl = _split_bf16(b)
    return _dot(jnp.concatenate([ah, ah, al], axis=1), jnp.concatenate([bh, bl, bh], axis=0))


def _silu(x):
    return x * jax.nn.sigmoid(x)


def _layer_norm(x, g, b):
    mu = jnp.mean(x, axis=-1, keepdims=True)
    xc = x - mu
    var = jnp.mean(xc * xc, axis=-1, keepdims=True)
    return xc * lax.rsqrt(var + LN_EPS) * g + b


def _rms_norm(x, g):
    return x * lax.rsqrt(jnp.mean(x * x, axis=-1, keepdims=True) + RMS_EPS) * g


def _mod_kernel(c_ref, w_ref, b_ref, o_ref):
    a = _silu(c_ref[...]).astype(BF16)
    o_ref[0] = _dot(a, w_ref[0].astype(BF16)) + b_ref[0]


def _mod_call(cc, w_mod, b_mod):
    depth, d, n = w_mod.shape
    tn = 1536
    assert n % tn == 0
    return pl.pallas_call(
        _mod_kernel,
        grid=(depth, n // tn),
        in_specs=[
            pl.BlockSpec((8, d), lambda l, j: (0, 0)),
            pl.BlockSpec((1, d, tn), lambda l, j: (l, 0, j)),
            pl.BlockSpec((1, 1, tn), lambda l, j: (l, 0, j)),
        ],
        out_specs=pl.BlockSpec((1, 8, tn), lambda l, j: (l, 0, j)),
        out_shape=jax.ShapeDtypeStruct((depth, 8, n), F32),
        compiler_params=_cparams(("parallel", "parallel")),
        name="mod",
    )(cc, w_mod, b_mod.reshape(depth, 1, n))


def _mod_spec(d, j):
    return pl.BlockSpec((1, 1, d), lambda b, i: (b, 0, j))


VT_ROWS = MLA_V + 16


def _mla_proj_kernel(h_ref, sh_ref, sc_ref, win_ref, qn_ref, kvn_ref, wqt_ref, wk_ref, wvt_ref,
                     qmt_ref, km_ref, qt_ref, k_ref, vt_ref, *, heads, q_rank, kv_rank):
    x = h_ref[0]
    tm = x.shape[0]
    u = x * (1.0 + sc_ref[0]) + sh_ref[0]
    lat = _dot(u.astype(BF16), win_ref[...])
    cq = _rms_norm(lat[:, :q_rank], qn_ref[...]).astype(BF16)
    ckv = _rms_norm(lat[:, q_rank:q_rank + kv_rank], kvn_ref[...]).astype(BF16)
    kr2 = lat[:, q_rank + kv_rank:]
    qt_all = _dot_nt(wqt_ref[...], cq)
    k_all = _dot(ckv, wk_ref[...])
    vt_all = _dot_nt(wvt_ref[...], ckv)
    qmt = qmt_ref[...]
    t = kr2 * km_ref[...]
    krot2 = (t + pltpu.roll(t, MLA_ROPE, axis=1)).astype(BF16)
    ones_rows = jnp.where(lax.broadcasted_iota(jnp.int32, (VT_ROWS - MLA_V, tm), 0) == 0, 1.0, 0.0).astype(BF16)
    for hh in range(heads):
        qt_ref[0, hh] = (qt_all[hh * MLA_QK:(hh + 1) * MLA_QK, :] * qmt).astype(BF16)
        k_ref[0, hh, :, :MLA_NOPE] = k_all[:, hh * MLA_NOPE:(hh + 1) * MLA_NOPE].astype(BF16)
        k_ref[0, hh, :, MLA_NOPE:] = krot2
        vt_ref[0, hh, 0, :MLA_V, :] = vt_all[hh * MLA_V:(hh + 1) * MLA_V, :].astype(BF16)
        vt_ref[0, hh, 0, MLA_V:, :] = ones_rows


def _mla_proj_call(h, mod, w_in, qn, kvn, wqt, wk, wvt, qmt, km, *, tm, heads):
    b, t, d = h.shape
    q_rank, kv_rank = qn.shape[1], kvn.shape[1]
    assert t % tm == 0
    kern = functools.partial(_mla_proj_kernel, heads=heads, q_rank=q_rank, kv_rank=kv_rank)
    return pl.pallas_call(
        kern,
        grid=(b, t // tm),
        in_specs=[
            pl.BlockSpec((1, tm, d), lambda bb, i: (bb, i, 0)),
            _mod_spec(d, 0), _mod_spec(d, 1),
            _const_spec(w_in.shape), _const_spec(qn.shape), _const_spec(kvn.shape),
            _const_spec(wqt.shape), _const_spec(wk.shape), _const_spec(wvt.shape),
            pl.BlockSpec((MLA_QK, tm), lambda bb, i: (0, i)),
            pl.BlockSpec((tm, 2 * MLA_ROPE), lambda bb, i: (i, 0)),
        ],
        out_specs=[
            pl.BlockSpec((1, heads, MLA_QK, tm), lambda bb, i: (bb, 0, 0, i)),
            pl.BlockSpec((1, heads, tm, MLA_QK), lambda bb, i: (bb, 0, i, 0)),
            pl.BlockSpec((1, heads, 1, VT_ROWS, tm), lambda bb, i: (bb, 0, i, 0, 0)),
        ],
        out_shape=[
            jax.ShapeDtypeStruct((b, heads, MLA_QK, t), BF16),
            jax.ShapeDtypeStruct((b, heads, t, MLA_QK), BF16),
            jax.ShapeDtypeStruct((b, heads, t // tm, VT_ROWS, tm), BF16),
        ],
        compiler_params=_cparams(("parallel", "parallel")),
        name="mla_proj",
    )(h, mod, mod, w_in, qn, kvn, wqt, wk, wvt, qmt, km)


QGROUP = 256


def _attn_kernel(*refs, slabs_per_step, n_lat, n_groups):
    if n_lat:
        qt_ref, kc_ref, vtc_ref, kl_ref, vtl_ref, o_ref, m_ref, acc_ref, st_ref = refs
    else:
        qt_ref, kc_ref, vtc_ref, o_ref, m_ref, acc_ref = refs

    def scores(kblk, g):
        return _dot(kblk, qt_ref[0, 0, :, g * QGROUP:(g + 1) * QGROUP])

    def softmax_pv(st, vt_slabs, g, first):
        slab = vt_slabs[0].shape[1]
        cols = slice(g * QGROUP, (g + 1) * QGROUP)
        cmax = jnp.max(st, axis=0, keepdims=True)
        if first:
            m_new = cmax
        else:
            m_old = m_ref[:, cols]
            m_new = jnp.maximum(m_old, cmax)
        pt = jnp.exp2(st - m_new).astype(BF16)
        pv = None
        for s, vt in enumerate(vt_slabs):
            term = _dot(vt, pt[s * slab:(s + 1) * slab, :])
            pv = term if pv is None else pv + term
        if first:
            acc_ref[:, cols] = pv
        else:
            acc_ref[:, cols] = acc_ref[:, cols] * jnp.exp2(m_old - m_new) + pv
        m_ref[:, cols] = m_new

    kc = kc_ref[0, 0]
    vtc = [vtc_ref[0, 0, s] for s in range(vtc_ref.shape[2])]
    sts = [scores(kc, g) for g in range(n_groups)]
    for g in range(n_groups):
        softmax_pv(sts[g], vtc, g, True)
    if n_lat:
        slab = vtl_ref.shape[4]
        tk = slabs_per_step * slab
        k0 = kl_ref[0, 0, pl.ds(0, tk), :]
        for g in range(n_groups):
            st_ref[g] = scores(k0, g)

        def body(j, carry):
            off = pl.multiple_of(jnp.minimum(j + 1, n_lat - 1) * tk, tk)
            k_next = kl_ref[0, 0, pl.ds(off, tk), :]
            vts = [vtl_ref[0, 0, j * slabs_per_step + s] for s in range(slabs_per_step)]
            for g in range(n_groups):
                nxt = scores(k_next, g)
                softmax_pv(st_ref[g], vts, g, False)
                st_ref[g] = nxt
            return carry
        lax.fori_loop(0, n_lat, body, 0)
    acc = acc_ref[...]
    ot = acc[:MLA_V] * (1.0 / acc[MLA_V:MLA_V + 1])
    o_ref[0] = ot.T.astype(o_ref.dtype)


def _attn_call(qt, k_ctx, vt_ctx, k_lat=None, vt_lat=None, *, tq, slabs_per_step=2):
    b, h, dqk, t = qt.shape
    tc = k_ctx.shape[2]
    assert t % tq == 0 and tq % QGROUP == 0
    n_lat = 0
    in_specs = [
        pl.BlockSpec((1, 1, dqk, tq), lambda bb, hh, i: (bb, hh, 0, i)),
        pl.BlockSpec((1, 1, tc, dqk), lambda bb, hh, i: (bb, hh, 0, 0)),
        pl.BlockSpec((1, 1) + vt_ctx.shape[2:], lambda bb, hh, i: (bb, hh, 0, 0, 0)),
    ]
    args = [qt, k_ctx, vt_ctx]
    if k_lat is not None:
        tl = k_lat.shape[2]
        ns = vt_lat.shape[2]
        assert ns % slabs_per_step == 0
        n_lat = ns // slabs_per_step
        in_specs += [
            pl.BlockSpec((1, 1, tl, dqk), lambda bb, hh, i: (bb, hh, 0, 0)),
            pl.BlockSpec((1, 1) + vt_lat.shape[2:], lambda bb, hh, i: (bb, hh, 0, 0, 0)),
        ]
        args += [k_lat, vt_lat]
    n_groups = tq // QGROUP
    kern = functools.partial(_attn_kernel, slabs_per_step=slabs_per_step, n_lat=n_lat, n_groups=n_groups)
    scratch = [pltpu.VMEM((1, tq), F32), pltpu.VMEM((VT_ROWS, tq), F32)]
    if n_lat:
        scratch.append(pltpu.VMEM((n_groups, slabs_per_step * vt_lat.shape[4], QGROUP), F32))
    return pl.pallas_call(
        kern,
        grid=(b, h, t // tq),
        in_specs=in_specs,
        out_specs=pl.BlockSpec((1, tq, MLA_V), lambda bb, hh, i: (bb, i, hh)),
        out_shape=jax.ShapeDtypeStruct((b, t, h * MLA_V), BF16),
        scratch_shapes=scratch,
        compiler_params=_cparams(("parallel", "parallel", "arbitrary")),
        name="attn_lat" if n_lat else "attn_ctx",
    )(*args)


def _post_tail(y_in, h_ref, g1_ref, sh2_ref, sc2_ref, g2_ref, wout_ref, l1g_ref, l1b_ref,
               wfi_ref, wfo_ref, l2g_ref, l2b_ref, o_ref, *, alpha, d_ff):
    y = _dot(y_in, wout_ref[...])
    h1 = _layer_norm(alpha * h_ref[0] + g1_ref[0] * y, l1g_ref[...], l1b_ref[...])
    u2 = h1 * (1.0 + sc2_ref[0]) + sh2_ref[0]
    gu = _dot(u2.astype(BF16), wfi_ref[...])
    act = _silu(gu[:, :d_ff]) * gu[:, d_ff:]
    f = _dot(act.astype(BF16), wfo_ref[...])
    o_ref[0] = _layer_norm(alpha * h1 + g2_ref[0] * f, l2g_ref[...], l2b_ref[...])


def _post_mla_kernel(a_ref, *rest, alpha, d_ff):
    _post_tail(a_ref[0], *rest, alpha=alpha, d_ff=d_ff)


def _post_gdn_kernel(of_ref, ob_ref, z_ref, nw_ref, *rest, alpha, d_ff, k_heads):
    nw = nw_ref[...]
    parts = []
    for kh in range(k_heads):
        o2 = of_ref[0, kh] + ob_ref[0, kh]
        for v in range(2):
            o = o2[:, v * GDN_DV:(v + 1) * GDN_DV]
            vh = 2 * kh + v
            z = z_ref[0, :, vh * GDN_DV:(vh + 1) * GDN_DV]
            parts.append((_rms_norm(o, nw) * _silu(z)).astype(BF16))
    _post_tail(jnp.concatenate(parts, axis=1), *rest, alpha=alpha, d_ff=d_ff)


def _post_call(mixer_in, h, mod, w_out, l1g, l1b, wfi, wfo, l2g, l2b, *, tm, alpha, gdn_norm=None):
    b, t, d = h.shape
    d_ff = wfo.shape[0]
    assert t % tm == 0
    row = lambda w: pl.BlockSpec((1, tm, w), lambda bb, i: (bb, i, 0))
    if gdn_norm is None:
        (a,) = mixer_in
        kern = functools.partial(_post_mla_kernel, alpha=alpha, d_ff=d_ff)
        head_specs = [row(a.shape[2])]
        head_args = [a]
    else:
        o_f, o_b, z = mixer_in
        kh = o_f.shape[1]
        kern = functools.partial(_post_gdn_kernel, alpha=alpha, d_ff=d_ff, k_heads=kh)
        hm = pl.BlockSpec((1, kh, tm, 2 * GDN_DV), lambda bb, i: (bb, 0, i, 0))
        head_specs = [hm, hm, row(z.shape[2]), _const_spec(gdn_norm.shape)]
        head_args = [o_f, o_b, z, gdn_norm]
    return pl.pallas_call(
        kern,
        grid=(b, t // tm),
        in_specs=head_specs + [
            row(d),
            _mod_spec(d, 2), _mod_spec(d, 3), _mod_spec(d, 4), _mod_spec(d, 5),
            _const_spec(w_out.shape), _const_spec(l1g.shape), _const_spec(l1b.shape),
            _const_spec(wfi.shape), _const_spec(wfo.shape),
            _const_spec(l2g.shape), _const_spec(l2b.shape),
        ],
        out_specs=row(d),
        out_shape=jax.ShapeDtypeStruct((b, t, d), F32),
        compiler_params=_cparams(("parallel", "parallel")),
        name="post_mla" if gdn_norm is None else "post_gdn",
    )(*head_args, h, mod, mod, mod, mod, w_out, l1g, l1b, wfi, wfo, l2g, l2b)


def _gdn_proj_kernel(h_ref, sh_ref, sc_ref, w_ref, qkv_ref, z_ref, ba_ref, *, qkv_w, z_w):
    u = h_ref[0] * (1.0 + sc_ref[0]) + sh_ref[0]
    p = _dot(u.astype(BF16), w_ref[...])
    qkv_ref[0] = p[:, :qkv_w]
    z_ref[0] = p[:, qkv_w:qkv_w + z_w]
    ba_ref[0] = p[:, qkv_w + z_w:]


def _gdn_proj_call(h, mod, w, *, tm, qkv_w, z_w):
    b, t, d = h.shape
    ba_w = w.shape[1] - qkv_w - z_w
    assert t % tm == 0
    row = lambda wd: pl.BlockSpec((1, tm, wd), lambda bb, i: (bb, i, 0))
    kern = functools.partial(_gdn_proj_kernel, qkv_w=qkv_w, z_w=z_w)
    return pl.pallas_call(
        kern,
        grid=(b, t // tm),
        in_specs=[row(d), _mod_spec(d, 0), _mod_spec(d, 1), _const_spec(w.shape)],
        out_specs=[row(qkv_w), row(z_w), row(ba_w)],
        out_shape=[
            jax.ShapeDtypeStruct((b, t, qkv_w), F32),
            jax.ShapeDtypeStruct((b, t, z_w), F32),
            jax.ShapeDtypeStruct((b, t, ba_w), F32),
        ],
        compiler_params=_cparams(("parallel", "parallel")),
        name="gdn_proj",
    )(h, mod, mod, w)


def _chunk_cumsum(g, suffix):
    n = g.shape[0]
    pos = lax.broadcasted_iota(jnp.int32, g.shape, 0) % CHUNK
    s = 1
    while s < CHUNK:
        if suffix:
            shifted = pltpu.roll(g, n - s, axis=0)
            ok = pos < CHUNK - s
        else:
            shifted = pltpu.roll(g, s, axis=0)
            ok = pos >= s
        g = g + jnp.where(ok, shifted, 0.0)
        s *= 2
    return g


def _gdn_conv_kernel(prev_ref, cur_ref, next_ref, ba_ref, cw_ref, al_ref, dtb_ref,
                     q_ref, k_ref, v_ref, gt_ref, xe_ref, *, k_heads, tm):
    i = pl.program_id(1)
    n_i = pl.num_programs(1)
    kw = k_heads * GDN_DK
    xe_ref[0:CONV_HALO] = jnp.where(i > 0, prev_ref[0], 0.0)
    xe_ref[CONV_HALO:CONV_HALO + tm] = cur_ref[0]
    xe_ref[CONV_HALO + tm:] = jnp.where(i < n_i - 1, next_ref[0], 0.0)
    half = GDN_CONV // 2
    acc = None
    for j in range(GDN_CONV):
        term = xe_ref[pl.ds(CONV_HALO - half + j, tm), :] * cw_ref[j:j + 1, :]
        acc = term if acc is None else acc + term
    y = _silu(acc)
    for kh in range(k_heads):
        qh = y[:, kh * GDN_DK:(kh + 1) * GDN_DK]
        q_ref[0, kh] = (qh * lax.rsqrt(jnp.sum(qh * qh, axis=-1, keepdims=True) + RMS_EPS)
                        * (GDN_DK ** -0.5)).astype(BF16)
        kk = y[:, kw + kh * GDN_DK:kw + (kh + 1) * GDN_DK]
        k_ref[0, kh] = (kk * lax.rsqrt(jnp.sum(kk * kk, axis=-1, keepdims=True) + RMS_EPS)).astype(BF16)
        v_ref[0, kh] = y[:, 2 * kw + kh * 2 * GDN_DV:2 * kw + (kh + 1) * 2 * GDN_DV].astype(BF16)
    ba = ba_ref[0]
    lane = lax.broadcasted_iota(jnp.int32, ba.shape, 1) % LANES
    beta = jax.nn.sigmoid(ba)
    g = -jnp.exp(al_ref[...]) * jax.nn.softplus(ba + dtb_ref[...])
    g = jnp.where((lane >= 4) & (lane < 8), g, 0.0)
    gcum = jnp.where(lane < 6, _chunk_cumsum(g, False), _chunk_cumsum(g, True))
    gates = jnp.where(lane < 4, beta, gcum)
    for kh in range(k_heads):
        gt_ref[0, kh] = gates[:, kh * LANES:(kh + 1) * LANES]


def _gdn_conv_call(qkv, ba, conv_w, a_log_ext, dtb_ext, *, tm, k_heads):
    b, t, c = qkv.shape
    assert t % tm == 0 and tm % CHUNK == 0 and tm % CONV_HALO == 0
    r = tm // CONV_HALO
    nblk8 = t // CONV_HALO
    kern = functools.partial(_gdn_conv_kernel, k_heads=k_heads, tm=tm)
    hm = lambda w: pl.BlockSpec((1, k_heads, tm, w), lambda bb, i: (bb, 0, i, 0))
    return pl.pallas_call(
        kern,
        grid=(b, t // tm),
        in_specs=[
            pl.BlockSpec((1, CONV_HALO, c), lambda bb, i: (bb, jnp.maximum(i * r - 1, 0), 0)),
            pl.BlockSpec((1, tm, c), lambda bb, i: (bb, i, 0)),
            pl.BlockSpec((1, CONV_HALO, c), lambda bb, i: (bb, jnp.minimum((i + 1) * r, nblk8 - 1), 0)),
            pl.BlockSpec((1, tm, ba.shape[2]), lambda bb, i: (bb, i, 0)),
            _const_spec(conv_w.shape), _const_spec(a_log_ext.shape), _const_spec(dtb_ext.shape),
        ],
        out_specs=[hm(GDN_DK), hm(GDN_DK), hm(2 * GDN_DV), hm(LANES)],
        out_shape=[
            jax.ShapeDtypeStruct((b, k_heads, t, GDN_DK), BF16),
            jax.ShapeDtypeStruct((b, k_heads, t, GDN_DK), BF16),
            jax.ShapeDtypeStruct((b, k_heads, t, 2 * GDN_DV), BF16),
            jax.ShapeDtypeStruct((b, k_heads, t, LANES), F32),
        ],
        scratch_shapes=[pltpu.VMEM((tm + 2 * CONV_HALO, c), F32)],
        compiler_params=_cparams(("parallel", "parallel")),
        name="gdn_conv",
    )(qkv, qkv, qkv, ba, conv_w, a_log_ext, dtb_ext)


def _gdn_chain(qc, kc, vv, gt, gtt, kk, qk, slot, backward, s_ref, s_idx):
    c = qc.shape[0]
    row = lax.broadcasted_iota(jnp.int32, (c, c), 0)
    col = lax.broadcasted_iota(jnp.int32, (c, c), 1)
    if backward:
        incl, strict = row <= col, row < col
    else:
        incl, strict = row >= col, row > col
    beta_col = jnp.broadcast_to(gt[:, slot:slot + 1], (c, LANES))
    g_col = jnp.broadcast_to(gt[:, 4 + slot:5 + slot], (c, LANES))
    g_row = gtt[4 + slot:5 + slot, :]
    diff = g_col[:, :c] - g_row
    decay = jnp.where(incl, jnp.exp(jnp.where(incl, diff, 0.0)), 0.0)
    lmat = jnp.where(strict, kk * beta_col[:, :c] * decay, 0.0)
    attn = qk * decay
    eg_col = jnp.exp(g_col)
    blk_r, blk_c = row, col
    tinv = None
    b = 1
    while b < c:
        pair = (blk_r // 2 == blk_c // 2) & (blk_r != blk_c)
        o_l = jnp.where(pair, lmat, 0.0)
        if tinv is None:
            tinv = jnp.where(row == col, 1.0, 0.0) - o_l
        else:
            tinv = tinv - _dot3(tinv, _dot3(o_l, tinv))
        blk_r, blk_c = blk_r // 2, blk_c // 2
        b *= 2
    rhs = jnp.concatenate([vv.astype(F32) * beta_col, kc.astype(F32) * (beta_col * eg_col)], axis=1)
    x = _dot3(tinv, rhs)
    u = x[:, :GDN_DV]
    w = x[:, GDN_DV:]
    s_old = s_ref[s_idx]
    wq = _dot(jnp.concatenate([w.astype(BF16), qc], axis=0), s_old.astype(BF16))
    v_new = u - wq[:c]
    g_last = g_col[0:1, :] if backward else g_col[c - 1:c, :]
    k_dec = kc.astype(F32) * jnp.exp(g_last - g_col)
    lhs = jnp.concatenate([attn.astype(BF16), k_dec.T.astype(BF16)], axis=0)
    ak = _dot(lhs, v_new.astype(BF16))
    s_ref[s_idx] = s_old * jnp.exp(g_last) + ak[c:]
    return eg_col * wq[c:] + ak[:c]


def _gdn_scan_kernel(qf_ref, kf_ref, vf_ref, gf_ref, qb_ref, kb_ref, vb_ref, gb_ref, s0_ref,
                     of_ref, ob_ref, sfin_ref, s_ref, *, hg):
    n = pl.program_id(2)

    @pl.when(n == 0)
    def _():
        s_ref[...] = s0_ref[0]

    for j in range(hg):
        for d, (q_ref, k_ref, v_ref, g_ref, o_ref) in enumerate(
                ((qf_ref, kf_ref, vf_ref, gf_ref, of_ref), (qb_ref, kb_ref, vb_ref, gb_ref, ob_ref))):
            qc = q_ref[0, j]
            kc = k_ref[0, j]
            v2 = v_ref[0, j]
            gt = g_ref[0, j]
            gtt = gt.T
            kq = _dot_nt(jnp.concatenate([kc, qc], axis=0), kc)
            c = qc.shape[0]
            kk, qk = kq[:c], kq[c:]
            outs = []
            for v in range(2):
                outs.append(_gdn_chain(qc, kc, v2[:, v * GDN_DV:(v + 1) * GDN_DV], gt, gtt, kk, qk,
                                       2 * d + v, d == 1, s_ref, j * 4 + 2 * d + v))
            o_ref[0, j] = jnp.concatenate(outs, axis=1)

    @pl.when(n == pl.num_programs(2) - 1)
    def _():
        sfin_ref[0] = s_ref[...]


def _gdn_scan_call(q, k, v, gates, s0, *, hg):
    b, kh, t, _ = q.shape
    assert kh % hg == 0 and t % CHUNK == 0
    nc = t // CHUNK
    ng = kh // hg
    fwd = lambda w: pl.BlockSpec((1, hg, CHUNK, w), lambda bb, g, n: (bb, g, n, 0))
    bwd = lambda w: pl.BlockSpec((1, hg, CHUNK, w), lambda bb, g, n: (bb, g, nc - 1 - n, 0))
    st = pl.BlockSpec((1, hg * 4, GDN_DK, GDN_DV), lambda bb, g, n: (bb, g, 0, 0))
    kern = functools.partial(_gdn_scan_kernel, hg=hg)
    return pl.pallas_call(
        kern,
        grid=(b, ng, nc),
        in_specs=[fwd(GDN_DK), fwd(GDN_DK), fwd(2 * GDN_DV), fwd(LANES),
                  bwd(GDN_DK), bwd(GDN_DK), bwd(2 * GDN_DV), bwd(LANES), st],
        out_specs=[fwd(2 * GDN_DV), bwd(2 * GDN_DV), st],
        out_shape=[
            jax.ShapeDtypeStruct((b, kh, t, 2 * GDN_DV), F32),
            jax.ShapeDtypeStruct((b, kh, t, 2 * GDN_DV), F32),
            jax.ShapeDtypeStruct(s0.shape, F32),
        ],
        scratch_shapes=[pltpu.VMEM((hg * 4, GDN_DK, GDN_DV), F32)],
        compiler_params=_cparams(("parallel", "parallel", "arbitrary")),
        name="gdn_scan",
    )(q, k, v, gates, q, k, v, gates, s0)


def _rope_tables(t, q_scale):
    pos = jnp.arange(t)
    row = (pos // GRID_W).astype(F32)
    col = (pos % GRID_W).astype(F32)
    n_freq = MLA_ROPE // 4
    inv_freq = ROPE_THETA ** (-(2.0 * jnp.arange(n_freq, dtype=F32)) / (MLA_ROPE // 2))
    ar = row[:, None] * inv_freq
    ac = col[:, None] * inv_freq
    cs = jnp.concatenate([jnp.cos(ar), jnp.cos(ar), jnp.cos(ac), jnp.cos(ac)], axis=1)
    sn = jnp.concatenate([-jnp.sin(ar), jnp.sin(ar), -jnp.sin(ac), jnp.sin(ac)], axis=1)
    km = jnp.concatenate([cs, sn], axis=1)
    qm = q_scale * jnp.concatenate([jnp.ones((t, MLA_NOPE), F32), cs, sn], axis=1)
    return qm, km


def _no_rope_tables(t, q_scale):
    one = jnp.ones((t, MLA_ROPE), F32)
    zero = jnp.zeros((t, MLA_ROPE), F32)
    km = jnp.concatenate([one, zero], axis=1)
    qm = q_scale * jnp.concatenate([jnp.ones((t, MLA_NOPE), F32), one, zero], axis=1)
    return qm, km


def _swap_perm():
    f = MLA_ROPE // 4
    return jnp.array(list(range(f, 2 * f)) + list(range(0, f)) + list(range(3 * f, 4 * f)) + list(range(2 * f, 3 * f)))


def _mla_layer(h_lat, h_ctx, mod_lat, mod_ctx, ln, w_in, q_norm, kv_norm, w_q_up, w_kv_up, w_out, *, alpha, last):
    bsz, seq, d = h_lat.shape
    tctx = h_ctx.shape[1]
    heads = d // 128
    q_rank, kv_rank = q_norm.shape[0], kv_norm.shape[0]
    tm_lat, tm_ctx = min(256, seq), min(256, tctx)
    q_scale = (MLA_NOPE + MLA_ROPE) ** -0.5 * math.log2(math.e)
    perm = _swap_perm()
    rope_cols = w_in[:, q_rank + kv_rank:]
    w_in_ext = jnp.concatenate([w_in, rope_cols[:, perm]], axis=1).astype(BF16)
    wq = w_q_up.reshape(q_rank, heads, MLA_NOPE + MLA_ROPE)
    wq_ext = jnp.concatenate([wq, wq[:, :, MLA_NOPE:][:, :, perm]], axis=2)
    wqt = wq_ext.reshape(q_rank, heads * MLA_QK).T.astype(BF16)
    wkv = w_kv_up.reshape(kv_rank, heads, MLA_NOPE + MLA_V)
    wk = wkv[:, :, :MLA_NOPE].reshape(kv_rank, -1).astype(BF16)
    wvt = wkv[:, :, MLA_NOPE:].reshape(kv_rank, -1).T.astype(BF16)
    qn, kvn = q_norm.reshape(1, -1), kv_norm.reshape(1, -1)
    qm_l, km_l = _rope_tables(seq, q_scale)
    qm_c, km_c = _no_rope_tables(tctx, q_scale)
    q_l, k_l, v_l = _mla_proj_call(h_lat, mod_lat, w_in_ext, qn, kvn, wqt, wk, wvt, qm_l.T, km_l,
                                   tm=tm_lat, heads=heads)
    q_c, k_c, v_c = _mla_proj_call(h_ctx, mod_ctx, w_in_ext, qn, kvn, wqt, wk, wvt, qm_c.T, km_c,
                                   tm=tm_ctx, heads=heads)
    o_l = _attn_call(q_l, k_c, v_c, k_l, v_l, tq=min(1024, seq))
    w_out = w_out.astype(BF16)
    h_lat_new = _post_call((o_l,), h_lat, mod_lat, w_out, *ln, tm=tm_lat, alpha=alpha)
    if not last:
        o_c = _attn_call(q_c, k_c, v_c, tq=tm_ctx)
        h_ctx = _post_call((o_c,), h_ctx, mod_ctx, w_out, *ln, tm=tm_ctx, alpha=alpha)
    return h_lat_new, h_ctx


def _gdn_layer(h_lat, h_ctx, mod_lat, mod_ctx, ln, w_in, conv_w, a_log, dt_bias, norm_w, w_out, *, alpha, last):
    bsz, seq, d = h_lat.shape
    tctx = h_ctx.shape[1]
    k_heads = d // 128
    v_heads = 2 * k_heads
    kw, vw = k_heads * GDN_DK, v_heads * GDN_DV
    qkv_w = 2 * kw + vw
    tm_lat, tm_ctx = min(256, seq), min(256, tctx)
    w_ba = w_in[:, qkv_w + vw:].reshape(d, 2, 2, k_heads, 2)
    w_ba = jnp.transpose(w_ba, (0, 3, 2, 1, 4)).reshape(d, k_heads, 8)
    w_ba = jnp.pad(w_ba, ((0, 0), (0, 0), (0, LANES - 8))).reshape(d, k_heads * LANES)
    w_ext = jnp.concatenate([w_in[:, :qkv_w + vw], w_ba], axis=1).astype(BF16)

    def gate_lanes(p):
        p = jnp.transpose(p.astype(F32).reshape(2, k_heads, 2), (1, 0, 2)).reshape(k_heads, 4)
        p = jnp.pad(p, ((0, 0), (4, LANES - 8)))
        return p.reshape(1, k_heads * LANES)

    al_ext, dtb_ext = gate_lanes(a_log), gate_lanes(dt_bias)
    s = jnp.zeros((bsz, k_heads * 4, GDN_DK, GDN_DV), F32)
    streams = []
    for h_in, m_in, tm in ((h_ctx, mod_ctx, tm_ctx), (h_lat, mod_lat, tm_lat)):
        qkv, z, ba = _gdn_proj_call(h_in, m_in, w_ext, tm=tm, qkv_w=qkv_w, z_w=vw)
        qh, kh, vh, gates = _gdn_conv_call(qkv, ba, conv_w, al_ext, dtb_ext, tm=tm, k_heads=k_heads)
        o_f, o_b, s = _gdn_scan_call(qh, kh, vh, gates, s, hg=2)
        streams.append((o_f, o_b, z))
    w_out = w_out.astype(BF16)
    nw = norm_w.reshape(1, -1)
    h_lat_new = _post_call(streams[1], h_lat, mod_lat, w_out, *ln, tm=tm_lat, alpha=alpha, gdn_norm=nw)
    if not last:
        h_ctx = _post_call(streams[0], h_ctx, mod_ctx, w_out, *ln, tm=tm_ctx, alpha=alpha, gdn_norm=nw)
    return h_lat_new, h_ctx


def kernel(x, c, ctx, c_ctx, w_mod, b_mod, ln1_g, ln1_b, ln2_g, ln2_b, w_ffn_in, w_ffn_out, mla_w_in, mla_q_norm, mla_kv_norm, mla_w_q_up, mla_w_kv_up, mla_w_out, gdn_w_in, gdn_conv, gdn_a_log, gdn_dt_bias, gdn_norm, gdn_w_out):
    bsz, _, d = x.shape
    depth = w_mod.shape[0]
    alpha = (2.0 * depth) ** 0.25

    cc = jnp.zeros((8, d), F32).at[:bsz].set(c).at[bsz].set(c_ctx)
    mod = _mod_call(cc, w_mod, b_mod)

    row = lambda a, i: a[i].reshape(1, -1)
    h_lat, h_ctx = x, ctx
    for i in range(depth):
        last = i == depth - 1
        mod_lat = mod[i, :bsz].reshape(bsz, 1, N_MOD * d)
        mod_ctx = jnp.broadcast_to(mod[i, bsz].reshape(1, 1, N_MOD * d), (bsz, 1, N_MOD * d))
        ln = (row(ln1_g, i), row(ln1_b, i), w_ffn_in[i].astype(BF16), w_ffn_out[i].astype(BF16),
              row(ln2_g, i), row(ln2_b, i))
        j = i // 2
        if i % 2 == 0:
            h_lat, h_ctx = _mla_layer(h_lat, h_ctx, mod_lat, mod_ctx, ln, mla_w_in[j], mla_q_norm[j],
                                      mla_kv_norm[j], mla_w_q_up[j], mla_w_kv_up[j], mla_w_out[j],
                                      alpha=alpha, last=last)
        else:
            h_lat, h_ctx = _gdn_layer(h_lat, h_ctx, mod_lat, mod_ctx, ln, gdn_w_in[j], gdn_conv[j],
                                      gdn_a_log[j], gdn_dt_bias[j], gdn_norm[j], gdn_w_out[j],
                                      alpha=alpha, last=last)
    return h_lat
```

```python
import functools
import math

import jax
import jax.numpy as jnp
from jax import lax
from jax.experimental import pallas as pl
from jax.experimental.pallas import tpu as pltpu

F32 = jnp.float32
BF16 = jnp.bfloat16

N_MOD = 6
LN_EPS = 1e-5
RMS_EPS = 1e-6
GRID_W = 64
ROPE_THETA = 10000.0

MLA_NOPE = 128
MLA_ROPE = 64
MLA_V = 128
MLA_QK = MLA_NOPE + 2 * MLA_ROPE

GDN_DK = 128
GDN_DV = 128
GDN_CONV = 5
CHUNK = 64
CONV_HALO = 8

V7X_VMEM_LIMIT = 56 * 1024 * 1024
LANES = 128


def _cparams(sem):
    return pltpu.CompilerParams(dimension_semantics=sem, vmem_limit_bytes=V7X_VMEM_LIMIT)


def _const_spec(shape):
    nd = len(shape)
    return pl.BlockSpec(shape, lambda *_: (0,) * nd, pipeline_mode=pl.Buffered(1))


def _dot(a, b):
    return jnp.dot(a, b, preferred_element_type=F32)


def _dot_nt(a, b):
    return lax.dot_general(a, b, (((1,), (1,)), ((), ())), preferred_element_type=F32)


def _split_bf16(a):
    hi = a.astype(BF16)
    lo = (a - hi.astype(F32)).astype(BF16)
    return hi, lo


def _dot3(a, b):
    ah, al = _split_bf16(a)
    bh, bl = _split_bf16(b)
    return _dot(jnp.concatenate([ah, ah, al], axis=1), jnp.concatenate([bh, bl, bh], axis=0))


def _silu(x):
    return x * jax.nn.sigmoid(x)


def _layer_norm(x, g, b):
    mu = jnp.mean(x, axis=-1, keepdims=True)
    xc = x - mu
    var = jnp.mean(xc * xc, axis=-1, keepdims=True)
    return xc * lax.rsqrt(var + LN_EPS) * g + b


def _rms_norm(x, g):
    return x * lax.rsqrt(jnp.mean(x * x, axis=-1, keepdims=True) + RMS_EPS) * g


def _mod_kernel(c_ref, w_ref, b_ref, o_ref):
    a = _silu(c_ref[...]).astype(BF16)
    o_ref[0] = _dot(a, w_ref[0].astype(BF16)) + b_ref[0]


def _mod_call(cc, w_mod, b_mod):
    depth, d, n = w_mod.shape
    tn = 1536
    assert n % tn == 0
    return pl.pallas_call(
        _mod_kernel,
        grid=(depth, n // tn),
        in_specs=[
            pl.BlockSpec((8, d), lambda l, j: (0, 0)),
            pl.BlockSpec((1, d, tn), lambda l, j: (l, 0, j)),
            pl.BlockSpec((1, 1, tn), lambda l, j: (l, 0, j)),
        ],
        out_specs=pl.BlockSpec((1, 8, tn), lambda l, j: (l, 0, j)),
        out_shape=jax.ShapeDtypeStruct((depth, 8, n), F32),
        compiler_params=_cparams(("parallel", "parallel")),
        name="mod",
    )(cc, w_mod, b_mod.reshape(depth, 1, n))


def _mod_spec(d, j):
    return pl.BlockSpec((1, 1, d), lambda b, i: (b, 0, j))


VT_ROWS = MLA_V + 16


def _mla_proj_kernel(h_ref, sh_ref, sc_ref, win_ref, qn_ref, kvn_ref, wqt_ref, wk_ref, wvt_ref,
                     qmt_ref, km_ref, qt_ref, k_ref, vt_ref, *, heads, q_rank, kv_rank):
    x = h_ref[0]
    tm = x.shape[0]
    u = x * (1.0 + sc_ref[0]) + sh_ref[0]
    lat = _dot(u.astype(BF16), win_ref[...])
    cq = _rms_norm(lat[:, :q_rank], qn_ref[...]).astype(BF16)
    ckv = _rms_norm(lat[:, q_rank:q_rank + kv_rank], kvn_ref[...]).astype(BF16)
    kr2 = lat[:, q_rank + kv_rank:]
    qt_all = _dot_nt(wqt_ref[...], cq)
    k_all = _dot(ckv, wk_ref[...])
    vt_all = _dot_nt(wvt_ref[...], ckv)
    qmt = qmt_ref[...]
    t = kr2 * km_ref[...]
    krot2 = (t + pltpu.roll(t, MLA_ROPE, axis=1)).astype(BF16)
    ones_rows = jnp.where(lax.broadcasted_iota(jnp.int32, (VT_ROWS - MLA_V, tm), 0) == 0, 1.0, 0.0).astype(BF16)
    for hh in range(heads):
        qt_ref[0, hh] = (qt_all[hh * MLA_QK:(hh + 1) * MLA_QK, :] * qmt).astype(BF16)
        k_ref[0, hh, :, :MLA_NOPE] = k_all[:, hh * MLA_NOPE:(hh + 1) * MLA_NOPE].astype(BF16)
        k_ref[0, hh, :, MLA_NOPE:] = krot2
        vt_ref[0, hh, 0, :MLA_V, :] = vt_all[hh * MLA_V:(hh + 1) * MLA_V, :].astype(BF16)
        vt_ref[0, hh, 0, MLA_V:, :] = ones_rows


def _mla_proj_call(h, mod, w_in, qn, kvn, wqt, wk, wvt, qmt, km, *, tm, heads):
    b, t, d = h.shape
    q_rank, kv_rank = qn.shape[1], kvn.shape[1]
    assert t % tm == 0
    kern = functools.partial(_mla_proj_kernel, heads=heads, q_rank=q_rank, kv_rank=kv_rank)
    return pl.pallas_call(
        kern,
        grid=(b, t // tm),
        in_specs=[
            pl.BlockSpec((1, tm, d), lambda bb, i: (bb, i, 0)),
            _mod_spec(d, 0), _mod_spec(d, 1),
            _const_spec(w_in.shape), _const_spec(qn.shape), _const_spec(kvn.shape),
            _const_spec(wqt.shape), _const_spec(wk.shape), _const_spec(wvt.shape),
            pl.BlockSpec((MLA_QK, tm), lambda bb, i: (0, i)),
            pl.BlockSpec((tm, 2 * MLA_ROPE), lambda bb, i: (i, 0)),
        ],
        out_specs=[
            pl.BlockSpec((1, heads, MLA_QK, tm), lambda bb, i: (bb, 0, 0, i)),
            pl.BlockSpec((1, heads, tm, MLA_QK), lambda bb, i: (bb, 0, i, 0)),
            pl.BlockSpec((1, heads, 1, VT_ROWS, tm), lambda bb, i: (bb, 0, i, 0, 0)),
        ],
        out_shape=[
            jax.ShapeDtypeStruct((b, heads, MLA_QK, t), BF16),
            jax.ShapeDtypeStruct((b, heads, t, MLA_QK), BF16),
            jax.ShapeDtypeStruct((b, heads, t // tm, VT_ROWS, tm), BF16),
        ],
        compiler_params=_cparams(("parallel", "parallel")),
        name="mla_proj",
    )(h, mod, mod, w_in, qn, kvn, wqt, wk, wvt, qmt, km)


QGROUP = 256


def _attn_kernel(*refs, slabs_per_step, n_lat, n_groups):
    if n_lat:
        qt_ref, kc_ref, vtc_ref, kl_ref, vtl_ref, o_ref, m_ref, acc_ref, st_ref = refs
    else:
        qt_ref, kc_ref, vtc_ref, o_ref, m_ref, acc_ref = refs

    def scores(kblk, g):
        return _dot(kblk, qt_ref[0, 0, :, g * QGROUP:(g + 1) * QGROUP])

    def softmax_pv(st, vt_slabs, g, first):
        slab = vt_slabs[0].shape[1]
        cols = slice(g * QGROUP, (g + 1) * QGROUP)
        cmax = jnp.max(st, axis=0, keepdims=True)
        if first:
            m_new = cmax
        else:
            m_old = m_ref[:, cols]
            m_new = jnp.maximum(m_old, cmax)
        pt = jnp.exp2(st - m_new).astype(BF16)
        pv = None
        for s, vt in enumerate(vt_slabs):
            term = _dot(vt, pt[s * slab:(s + 1) * slab, :])
            pv = term if pv is None else pv + term
        if first:
            acc_ref[:, cols] = pv
        else:
            acc_ref[:, cols] = acc_ref[:, cols] * jnp.exp2(m_old - m_new) + pv
        m_ref[:, cols] = m_new

    kc = kc_ref[0, 0]
    vtc = [vtc_ref[0, 0, s] for s in range(vtc_ref.shape[2])]
    sts = [scores(kc, g) for g in range(n_groups)]
    for g in range(n_groups):
        softmax_pv(sts[g], vtc, g, True)
    if n_lat:
        slab = vtl_ref.shape[4]
        tk = slabs_per_step * slab
        k0 = kl_ref[0, 0, pl.ds(0, tk), :]
        for g in range(n_groups):
            st_ref[g] = scores(k0, g)

        def body(j, carry):
            off = pl.multiple_of(jnp.minimum(j + 1, n_lat - 1) * tk, tk)
            k_next = kl_ref[0, 0, pl.ds(off, tk), :]
            vts = [vtl_ref[0, 0, j * slabs_per_step + s] for s in range(slabs_per_step)]
            for g in range(n_groups):
                nxt = scores(k_next, g)
                softmax_pv(st_ref[g], vts, g, False)
                st_ref[g] = nxt
            return carry
        lax.fori_loop(0, n_lat, body, 0)
    acc = acc_ref[...]
    ot = acc[:MLA_V] * (1.0 / acc[MLA_V:MLA_V + 1])
    o_ref[0] = ot.T.astype(o_ref.dtype)


def _attn_call(qt, k_ctx, vt_ctx, k_lat=None, vt_lat=None, *, tq, slabs_per_step=2):
    b, h, dqk, t = qt.shape
    tc = k_ctx.shape[2]
    assert t % tq == 0 and tq % QGROUP == 0
    n_lat = 0
    in_specs = [
        pl.BlockSpec((1, 1, dqk, tq), lambda bb, hh, i: (bb, hh, 0, i)),
        pl.BlockSpec((1, 1, tc, dqk), lambda bb, hh, i: (bb, hh, 0, 0)),
        pl.BlockSpec((1, 1) + vt_ctx.shape[2:], lambda bb, hh, i: (bb, hh, 0, 0, 0)),
    ]
    args = [qt, k_ctx, vt_ctx]
    if k_lat is not None:
        tl = k_lat.shape[2]
        ns = vt_lat.shape[2]
        assert ns % slabs_per_step == 0
        n_lat = ns // slabs_per_step
        in_specs += [
            pl.BlockSpec((1, 1, tl, dqk), lambda bb, hh, i: (bb, hh, 0, 0)),
            pl.BlockSpec((1, 1) + vt_lat.shape[2:], lambda bb, hh, i: (bb, hh, 0, 0, 0)),
        ]
        args += [k_lat, vt_lat]
    n_groups = tq // QGROUP
    kern = functools.partial(_attn_kernel, slabs_per_step=slabs_per_step, n_lat=n_lat, n_groups=n_groups)
    scratch = [pltpu.VMEM((1, tq), F32), pltpu.VMEM((VT_ROWS, tq), F32)]
    if n_lat:
        scratch.append(pltpu.VMEM((n_groups, slabs_per_step * vt_lat.shape[4], QGROUP), F32))
    return pl.pallas_call(
        kern,
        grid=(b, h, t // tq),
        in_specs=in_specs,
        out_specs=pl.BlockSpec((1, tq, MLA_V), lambda bb, hh, i: (bb, i, hh)),
        out_shape=jax.ShapeDtypeStruct((b, t, h * MLA_V), BF16),
        scratch_shapes=scratch,
        compiler_params=_cparams(("parallel", "parallel", "arbitrary")),
        name="attn_lat" if n_lat else "attn_ctx",
    )(*args)


def _post_tail(y_in, h_ref, g1_ref, sh2_ref, sc2_ref, g2_ref, wout_ref, l1g_ref, l1b_ref,
               wfi_ref, wfo_ref, l2g_ref, l2b_ref, o_ref, *, alpha, d_ff):
    y = _dot(y_in, wout_ref[...])
    h1 = _layer_norm(alpha * h_ref[0] + g1_ref[0] * y, l1g_ref[...], l1b_ref[...])
    u2 = h1 * (1.0 + sc2_ref[0]) + sh2_ref[0]
    gu = _dot(u2.astype(BF16), wfi_ref[...])
    act = _silu(gu[:, :d_ff]) * gu[:, d_ff:]
    f = _dot(act.astype(BF16), wfo_ref[...])
    o_ref[0] = _layer_norm(alpha * h1 + g2_ref[0] * f, l2g_ref[...], l2b_ref[...])


def _post_mla_kernel(a_ref, *rest, alpha, d_ff):
    _post_tail(a_ref[0], *rest, alpha=alpha, d_ff=d_ff)


def _post_gdn_kernel(of_ref, ob_ref, z_ref, nw_ref, *rest, alpha, d_ff, k_heads):
    nw = nw_ref[...]
    parts = []
    for kh in range(k_heads):
        o2 = of_ref[0, kh] + ob_ref[0, kh]
        for v in range(2):
            o = o2[:, v * GDN_DV:(v + 1) * GDN_DV]
            vh = 2 * kh + v
            z = z_ref[0, :, vh * GDN_DV:(vh + 1) * GDN_DV]
            parts.append((_rms_norm(o, nw) * _silu(z)).astype(BF16))
    _post_tail(jnp.concatenate(parts, axis=1), *rest, alpha=alpha, d_ff=d_ff)


def _post_call(mixer_in, h, mod, w_out, l1g, l1b, wfi, wfo, l2g, l2b, *, tm, alpha, gdn_norm=None):
    b, t, d = h.shape
    d_ff = wfo.shape[0]
    assert t % tm == 0
    row = lambda w: pl.BlockSpec((1, tm, w), lambda bb, i: (bb, i, 0))
    if gdn_norm is None:
        (a,) = mixer_in
        kern = functools.partial(_post_mla_kernel, alpha=alpha, d_ff=d_ff)
        head_specs = [row(a.shape[2])]
        head_args = [a]
    else:
        o_f, o_b, z = mixer_in
        kh = o_f.shape[1]
        kern = functools.partial(_post_gdn_kernel, alpha=alpha, d_ff=d_ff, k_heads=kh)
        hm = pl.BlockSpec((1, kh, tm, 2 * GDN_DV), lambda bb, i: (bb, 0, i, 0))
        head_specs = [hm, hm, row(z.shape[2]), _const_spec(gdn_norm.shape)]
        head_args = [o_f, o_b, z, gdn_norm]
    return pl.pallas_call(
        kern,
        grid=(b, t // tm),
        in_specs=head_specs + [
            row(d),
            _mod_spec(d, 2), _mod_spec(d, 3), _mod_spec(d, 4), _mod_spec(d, 5),
            _const_spec(w_out.shape), _const_spec(l1g.shape), _const_spec(l1b.shape),
            _const_spec(wfi.shape), _const_spec(wfo.shape),
            _const_spec(l2g.shape), _const_spec(l2b.shape),
        ],
        out_specs=row(d),
        out_shape=jax.ShapeDtypeStruct((b, t, d), F32),
        compiler_params=_cparams(("parallel", "parallel")),
        name="post_mla" if gdn_norm is None else "post_gdn",
    )(*head_args, h, mod, mod, mod, mod, w_out, l1g, l1b, wfi, wfo, l2g, l2b)


def _gdn_proj_kernel(h_ref, sh_ref, sc_ref, w_ref, qkv_ref, z_ref, ba_ref, *, qkv_w, z_w):
    u = h_ref[0] * (1.0 + sc_ref[0]) + sh_ref[0]
    p = _dot(u.astype(BF16), w_ref[...])
    qkv_ref[0] = p[:, :qkv_w]
    z_ref[0] = p[:, qkv_w:qkv_w + z_w]
    ba_ref[0] = p[:, qkv_w + z_w:]


def _gdn_proj_call(h, mod, w, *, tm, qkv_w, z_w):
    b, t, d = h.shape
    ba_w = w.shape[1] - qkv_w - z_w
    assert t % tm == 0
    row = lambda wd: pl.BlockSpec((1, tm, wd), lambda bb, i: (bb, i, 0))
    kern = functools.partial(_gdn_proj_kernel, qkv_w=qkv_w, z_w=z_w)
    return pl.pallas_call(
        kern,
        grid=(b, t // tm),
        in_specs=[row(d), _mod_spec(d, 0), _mod_spec(d, 1), _const_spec(w.shape)],
        out_specs=[row(qkv_w), row(z_w), row(ba_w)],
        out_shape=[
            jax.ShapeDtypeStruct((b, t, qkv_w), F32),
            jax.ShapeDtypeStruct((b, t, z_w), F32),
            jax.ShapeDtypeStruct((b, t, ba_w), F32),
        ],
        compiler_params=_cparams(("parallel", "parallel")),
        name="gdn_proj",
    )(h, mod, mod, w)


def _chunk_cumsum(g, suffix):
    n = g.shape[0]
    pos = lax.broadcasted_iota(jnp.int32, g.shape, 0) % CHUNK
    s = 1
    while s < CHUNK:
        if suffix:
            shifted = pltpu.roll(g, n - s, axis=0)
            ok = pos < CHUNK - s
        else:
            shifted = pltpu.roll(g, s, axis=0)
            ok = pos >= s
        g = g + jnp.where(ok, shifted, 0.0)
        s *= 2
    return g


def _gdn_conv_kernel(prev_ref, cur_ref, next_ref, ba_ref, cw_ref, al_ref, dtb_ref,
                     q_ref, k_ref, v_ref, gt_ref, xe_ref, *, k_heads, tm):
    i = pl.program_id(1)
    n_i = pl.num_programs(1)
    kw = k_heads * GDN_DK
    xe_ref[0:CONV_HALO] = jnp.where(i > 0, prev_ref[0], 0.0)
    xe_ref[CONV_HALO:CONV_HALO + tm] = cur_ref[0]
    xe_ref[CONV_HALO + tm:] = jnp.where(i < n_i - 1, next_ref[0], 0.0)
    half = GDN_CONV // 2
    acc = None
    for j in range(GDN_CONV):
        term = xe_ref[pl.ds(CONV_HALO - half + j, tm), :] * cw_ref[j:j + 1, :]
        acc = term if acc is None else acc + term
    y = _silu(acc)
    for kh in range(k_heads):
        qh = y[:, kh * GDN_DK:(kh + 1) * GDN_DK]
        q_ref[0, kh] = (qh * lax.rsqrt(jnp.sum(qh * qh, axis=-1, keepdims=True) + RMS_EPS)
                        * (GDN_DK ** -0.5)).astype(BF16)
        kk = y[:, kw + kh * GDN_DK:kw + (kh + 1) * GDN_DK]
        k_ref[0, kh] = (kk * lax.rsqrt(jnp.sum(kk * kk, axis=-1, keepdims=True) + RMS_EPS)).astype(BF16)
        v_ref[0, kh] = y[:, 2 * kw + kh * 2 * GDN_DV:2 * kw + (kh + 1) * 2 * GDN_DV].astype(BF16)
    ba = ba_ref[0]
    lane = lax.broadcasted_iota(jnp.int32, ba.shape, 1) % LANES
    beta = jax.nn.sigmoid(ba)
    g = -jnp.exp(al_ref[...]) * jax.nn.softplus(ba + dtb_ref[...])
    g = jnp.where((lane >= 4) & (lane < 8), g, 0.0)
    gcum = jnp.where(lane < 6, _chunk_cumsum(g, False), _chunk_cumsum(g, True))
    gates = jnp.where(lane < 4, beta, gcum)
    for kh in range(k_heads):
        gt_ref[0, kh] = gates[:, kh * LANES:(kh + 1) * LANES]


def _gdn_conv_call(qkv, ba, conv_w, a_log_ext, dtb_ext, *, tm, k_heads):
    b, t, c = qkv.shape
    assert t % tm == 0 and tm % CHUNK == 0 and tm % CONV_HALO == 0
    r = tm // CONV_HALO
    nblk8 = t // CONV_HALO
    kern = functools.partial(_gdn_conv_kernel, k_heads=k_heads, tm=tm)
    hm = lambda w: pl.BlockSpec((1, k_heads, tm, w), lambda bb, i: (bb, 0, i, 0))
    return pl.pallas_call(
        kern,
        grid=(b, t // tm),
        in_specs=[
            pl.BlockSpec((1, CONV_HALO, c), lambda bb, i: (bb, jnp.maximum(i * r - 1, 0), 0)),
            pl.BlockSpec((1, tm, c), lambda bb, i: (bb, i, 0)),
            pl.BlockSpec((1, CONV_HALO, c), lambda bb, i: (bb, jnp.minimum((i + 1) * r, nblk8 - 1), 0)),
            pl.BlockSpec((1, tm, ba.shape[2]), lambda bb, i: (bb, i, 0)),
            _const_spec(conv_w.shape), _const_spec(a_log_ext.shape), _const_spec(dtb_ext.shape),
        ],
        out_specs=[hm(GDN_DK), hm(GDN_DK), hm(2 * GDN_DV), hm(LANES)],
        out_shape=[
            jax.ShapeDtypeStruct((b, k_heads, t, GDN_DK), BF16),
            jax.ShapeDtypeStruct((b, k_heads, t, GDN_DK), BF16),
            jax.ShapeDtypeStruct((b, k_heads, t, 2 * GDN_DV), BF16),
            jax.ShapeDtypeStruct((b, k_heads, t, LANES), F32),
        ],
        scratch_shapes=[pltpu.VMEM((tm + 2 * CONV_HALO, c), F32)],
        compiler_params=_cparams(("parallel", "parallel")),
        name="gdn_conv",
    )(qkv, qkv, qkv, ba, conv_w, a_log_ext, dtb_ext)


WAVE_LAG = 4


def _dot_split(a_hi, a_lo, b_hi, b_lo):
    return _dot(jnp.concatenate([a_hi, a_lo], axis=1), jnp.concatenate([b_hi, b_lo, b_hi, b_lo], axis=0))


def _gdn_scan_kernel(qf_ref, kf_ref, vf_ref, gf_ref, qb_ref, kb_ref, vb_ref, gb_ref, s0_ref,
                     of_ref, ob_ref, sfin_ref, s_ref, *, hg, n_waves):
    n = pl.program_id(2)

    @pl.when(n == 0)
    def _():
        s_ref[...] = s0_ref[0]

    c = CHUNK
    row = lax.broadcasted_iota(jnp.int32, (c, LANES), 0)
    col = lax.broadcasted_iota(jnp.int32, (c, LANES), 1) % c
    dirs = ((qf_ref, kf_ref, vf_ref, gf_ref), (qb_ref, kb_ref, vb_ref, gb_ref))

    eye = jnp.where(row == col, 1.0, 0.0)
    zero_rows = jnp.zeros((c, 2 * GDN_DV), BF16)
    pairs = []
    blk_r, blk_c = row, col
    b = 1
    while b < c:
        pair = (blk_r // 2 == blk_c // 2) & (blk_r != blk_c)
        pairs.append(pair if b == 1 else jnp.where(pair, 1.0, 0.0).astype(BF16))
        blk_r, blk_c = blk_r // 2, blk_c // 2
        b *= 2

    def wave(heads):
        chains = []
        for j in heads:
            for d in range(2):
                q_ref, k_ref, v_ref, g_ref = dirs[d]
                qc, kc, v2, gt = q_ref[0, j], k_ref[0, j], v_ref[0, j], g_ref[0, j]
                gtt = jnp.concatenate([gt, gt], axis=0).T
                kq = _dot_nt(jnp.concatenate([kc, qc], axis=0), jnp.concatenate([kc, kc], axis=0))
                for v in range(2):
                    chains.append(dict(j=j, d=d, v=v, qc=qc, kc=kc, vv=v2[:, v * GDN_DV:(v + 1) * GDN_DV],
                                       gt=gt, gtt=gtt, kk=kq[:c], qk=kq[c:]))
            yield
        for i, ch in enumerate(chains):
            slot = 2 * ch["d"] + ch["v"]
            if ch["d"] == 1:
                incl, strict = row <= col, row < col
            else:
                incl, strict = row >= col, row > col
            beta_col = jnp.broadcast_to(ch["gt"][:, slot:slot + 1], (c, LANES))
            g_col = jnp.broadcast_to(ch["gt"][:, 4 + slot:5 + slot], (c, LANES))
            g_row = ch["gtt"][4 + slot:5 + slot, :]
            decay = jnp.where(incl, jnp.exp(jnp.where(incl, g_col - g_row, 0.0)), 0.0)
            eg_col = jnp.exp(g_col)
            lmat = jnp.where(strict, ch["kk"] * beta_col * decay, 0.0)
            ch.update(
                lm=lmat.astype(BF16), tinv=eye - jnp.where(pairs[0], lmat, 0.0),
                attn=(ch["qk"] * decay)[:, :c].astype(BF16),
                eg_col=eg_col, g_col=g_col,
                rhs=jnp.concatenate([ch["vv"].astype(F32) * beta_col,
                                     ch["kc"].astype(F32) * (beta_col * eg_col)], axis=1).astype(BF16))
            if i % 4 == 3:
                yield
        for pair_bf in pairs[1:]:
            for ch in chains:
                ch["t_hi"], ch["t_lo"] = _split_bf16(ch["tinv"])
                ch["w"] = _dot(ch["lm"] * pair_bf, jnp.concatenate([ch["t_hi"], ch["t_lo"]], axis=0))
            yield
            for ch in chains:
                w_hi, w_lo = _split_bf16(ch["w"])
                ch["tinv"] = ch["tinv"] - _dot_split(ch["t_hi"], ch["t_lo"], w_hi, w_lo)
            yield
        for ch in chains:
            t_hi, t_lo = _split_bf16(ch["tinv"])
            r = ch["rhs"]
            ch["x"] = _dot(jnp.concatenate([t_hi, t_lo], axis=1),
                           jnp.concatenate([r, zero_rows, r, zero_rows], axis=0))
        yield
        for ch in chains:
            ch["s_idx"] = ch["j"] * 4 + 2 * ch["d"] + ch["v"]
            ch["s_old"] = s_ref[ch["s_idx"]]
            ch["wq"] = _dot(jnp.concatenate([ch["x"][:, GDN_DV:].astype(BF16), ch["qc"]], axis=0),
                            ch["s_old"].astype(BF16))
        yield
        for ch in chains:
            g_col = ch["g_col"]
            ch["g_last"] = g_col[0:1, :] if ch["d"] == 1 else g_col[c - 1:c, :]
            v_new = (ch["x"][:, :GDN_DV] - ch["wq"][:c]).astype(BF16)
            k_dec = ch["kc"].astype(F32) * jnp.exp(ch["g_last"] - g_col)
            ch["av"] = _dot(ch["attn"], v_new)
            ch["kv"] = _dot(k_dec.T.astype(BF16), v_new)
        yield
        outs = {}
        for ch in chains:
            s_ref[ch["s_idx"]] = ch["s_old"] * jnp.exp(ch["g_last"]) + ch["kv"]
            outs[ch["j"], ch["d"], ch["v"]] = ch["eg_col"] * ch["wq"][c:] + ch["av"]
        for j in heads:
            for d, o_ref in enumerate((of_ref, ob_ref)):
                o_ref[0, j] = jnp.concatenate([outs[j, d, 0], outs[j, d, 1]], axis=1)

    waves = [wave(list(range(w, hg, n_waves))) for w in range(n_waves)]
    live = []
    pending = list(waves)
    while live or pending:
        if pending:
            live.append(pending.pop(0))
        for _ in range(WAVE_LAG if pending else 1):
            for g in list(live):
                if next(g, "done") == "done":
                    live.remove(g)

    @pl.when(n == pl.num_programs(2) - 1)
    def _():
        sfin_ref[0] = s_ref[...]


def _gdn_scan_call(q, k, v, gates, s0, *, hg):
    b, kh, t, _ = q.shape
    assert kh % hg == 0 and t % CHUNK == 0
    nc = t // CHUNK
    ng = kh // hg
    fwd = lambda w: pl.BlockSpec((1, hg, CHUNK, w), lambda bb, g, n: (bb, g, n, 0))
    bwd = lambda w: pl.BlockSpec((1, hg, CHUNK, w), lambda bb, g, n: (bb, g, nc - 1 - n, 0))
    st = pl.BlockSpec((1, hg * 4, GDN_DK, GDN_DV), lambda bb, g, n: (bb, g, 0, 0))
    kern = functools.partial(_gdn_scan_kernel, hg=hg, n_waves=1)
    return pl.pallas_call(
        kern,
        grid=(b, ng, nc),
        in_specs=[fwd(GDN_DK), fwd(GDN_DK), fwd(2 * GDN_DV), fwd(LANES),
                  bwd(GDN_DK), bwd(GDN_DK), bwd(2 * GDN_DV), bwd(LANES), st],
        out_specs=[fwd(2 * GDN_DV), bwd(2 * GDN_DV), st],
        out_shape=[
            jax.ShapeDtypeStruct((b, kh, t, 2 * GDN_DV), F32),
            jax.ShapeDtypeStruct((b, kh, t, 2 * GDN_DV), F32),
            jax.ShapeDtypeStruct(s0.shape, F32),
        ],
        scratch_shapes=[pltpu.VMEM((hg * 4, GDN_DK, GDN_DV), F32)],
        compiler_params=_cparams(("parallel", "parallel", "arbitrary")),
        name="gdn_scan",
    )(q, k, v, gates, q, k, v, gates, s0)


def _rope_tables(t, q_scale):
    pos = jnp.arange(t)
    row = (pos // GRID_W).astype(F32)
    col = (pos % GRID_W).astype(F32)
    n_freq = MLA_ROPE // 4
    inv_freq = ROPE_THETA ** (-(2.0 * jnp.arange(n_freq, dtype=F32)) / (MLA_ROPE // 2))
    ar = row[:, None] * inv_freq
    ac = col[:, None] * inv_freq
    cs = jnp.concatenate([jnp.cos(ar), jnp.cos(ar), jnp.cos(ac), jnp.cos(ac)], axis=1)
    sn = jnp.concatenate([-jnp.sin(ar), jnp.sin(ar), -jnp.sin(ac), jnp.sin(ac)], axis=1)
    km = jnp.concatenate([cs, sn], axis=1)
    qm = q_scale * jnp.concatenate([jnp.ones((t, MLA_NOPE), F32), cs, sn], axis=1)
    return qm, km


def _no_rope_tables(t, q_scale):
    one = jnp.ones((t, MLA_ROPE), F32)
    zero = jnp.zeros((t, MLA_ROPE), F32)
    km = jnp.concatenate([one, zero], axis=1)
    qm = q_scale * jnp.concatenate([jnp.ones((t, MLA_NOPE), F32), one, zero], axis=1)
    return qm, km


def _swap_perm():
    f = MLA_ROPE // 4
    return jnp.array(list(range(f, 2 * f)) + list(range(0, f)) + list(range(3 * f, 4 * f)) + list(range(2 * f, 3 * f)))


def _mla_layer(h_lat, h_ctx, mod_lat, mod_ctx, ln, w_in, q_norm, kv_norm, w_q_up, w_kv_up, w_out, *, alpha, last):
    bsz, seq, d = h_lat.shape
    tctx = h_ctx.shape[1]
    heads = d // 128
    q_rank, kv_rank = q_norm.shape[0], kv_norm.shape[0]
    tm_lat, tm_ctx = min(256, seq), min(256, tctx)
    q_scale = (MLA_NOPE + MLA_ROPE) ** -0.5 * math.log2(math.e)
    perm = _swap_perm()
    rope_cols = w_in[:, q_rank + kv_rank:]
    w_in_ext = jnp.concatenate([w_in, rope_cols[:, perm]], axis=1).astype(BF16)
    wq = w_q_up.reshape(q_rank, heads, MLA_NOPE + MLA_ROPE)
    wq_ext = jnp.concatenate([wq, wq[:, :, MLA_NOPE:][:, :, perm]], axis=2)
    wqt = wq_ext.reshape(q_rank, heads * MLA_QK).T.astype(BF16)
    wkv = w_kv_up.reshape(kv_rank, heads, MLA_NOPE + MLA_V)
    wk = wkv[:, :, :MLA_NOPE].reshape(kv_rank, -1).astype(BF16)
    wvt = wkv[:, :, MLA_NOPE:].reshape(kv_rank, -1).T.astype(BF16)
    qn, kvn = q_norm.reshape(1, -1), kv_norm.reshape(1, -1)
    qm_l, km_l = _rope_tables(seq, q_scale)
    qm_c, km_c = _no_rope_tables(tctx, q_scale)
    q_l, k_l, v_l = _mla_proj_call(h_lat, mod_lat, w_in_ext, qn, kvn, wqt, wk, wvt, qm_l.T, km_l,
                                   tm=tm_lat, heads=heads)
    q_c, k_c, v_c = _mla_proj_call(h_ctx, mod_ctx, w_in_ext, qn, kvn, wqt, wk, wvt, qm_c.T, km_c,
                                   tm=tm_ctx, heads=heads)
    o_l = _attn_call(q_l, k_c, v_c, k_l, v_l, tq=min(1024, seq))
    w_out = w_out.astype(BF16)
    h_lat_new = _post_call((o_l,), h_lat, mod_lat, w_out, *ln, tm=tm_lat, alpha=alpha)
    if not last:
        o_c = _attn_call(q_c, k_c, v_c, tq=tm_ctx)
        h_ctx = _post_call((o_c,), h_ctx, mod_ctx, w_out, *ln, tm=tm_ctx, alpha=alpha)
    return h_lat_new, h_ctx


def _gdn_layer(h_lat, h_ctx, mod_lat, mod_ctx, ln, w_in, conv_w, a_log, dt_bias, norm_w, w_out, *, alpha, last):
    bsz, seq, d = h_lat.shape
    tctx = h_ctx.shape[1]
    k_heads = d // 128
    v_heads = 2 * k_heads
    kw, vw = k_heads * GDN_DK, v_heads * GDN_DV
    qkv_w = 2 * kw + vw
    tm_lat, tm_ctx = min(256, seq), min(256, tctx)
    w_ba = w_in[:, qkv_w + vw:].reshape(d, 2, 2, k_heads, 2)
    w_ba = jnp.transpose(w_ba, (0, 3, 2, 1, 4)).reshape(d, k_heads, 8)
    w_ba = jnp.pad(w_ba, ((0, 0), (0, 0), (0, LANES - 8))).reshape(d, k_heads * LANES)
    w_ext = jnp.concatenate([w_in[:, :qkv_w + vw], w_ba], axis=1).astype(BF16)

    def gate_lanes(p):
        p = jnp.transpose(p.astype(F32).reshape(2, k_heads, 2), (1, 0, 2)).reshape(k_heads, 4)
        p = jnp.pad(p, ((0, 0), (4, LANES - 8)))
        return p.reshape(1, k_heads * LANES)

    al_ext, dtb_ext = gate_lanes(a_log), gate_lanes(dt_bias)
    s = jnp.zeros((bsz, k_heads * 4, GDN_DK, GDN_DV), F32)
    streams = []
    for h_in, m_in, tm in ((h_ctx, mod_ctx, tm_ctx), (h_lat, mod_lat, tm_lat)):
        qkv, z, ba = _gdn_proj_call(h_in, m_in, w_ext, tm=tm, qkv_w=qkv_w, z_w=vw)
        qh, kh, vh, gates = _gdn_conv_call(qkv, ba, conv_w, al_ext, dtb_ext, tm=tm, k_heads=k_heads)
        o_f, o_b, s = _gdn_scan_call(qh, kh, vh, gates, s, hg=4)
        streams.append((o_f, o_b, z))
    w_out = w_out.astype(BF16)
    nw = norm_w.reshape(1, -1)
    h_lat_new = _post_call(streams[1], h_lat, mod_lat, w_out, *ln, tm=tm_lat, alpha=alpha, gdn_norm=nw)
    if not last:
        h_ctx = _post_call(streams[0], h_ctx, mod_ctx, w_out, *ln, tm=tm_ctx, alpha=alpha, gdn_norm=nw)
    return h_lat_new, h_ctx


def kernel(x, c, ctx, c_ctx, w_mod, b_mod, ln1_g, ln1_b, ln2_g, ln2_b, w_ffn_in, w_ffn_out, mla_w_in, mla_q_norm, mla_kv_norm, mla_w_q_up, mla_w_kv_up, mla_w_out, gdn_w_in, gdn_conv, gdn_a_log, gdn_dt_bias, gdn_norm, gdn_w_out):
    bsz, _, d = x.shape
    depth = w_mod.shape[0]
    alpha = (2.0 * depth) ** 0.25

    cc = jnp.zeros((8, d), F32).at[:bsz].set(c).at[bsz].set(c_ctx)
    mod = _mod_call(cc, w_mod, b_mod)

    row = lambda a, i: a[i].reshape(1, -1)
    h_lat, h_ctx = x, ctx
    for i in range(depth):
        last = i == depth - 1
        mod_lat = mod[i, :bsz].reshape(bsz, 1, N_MOD * d)
        mod_ctx = jnp.broadcast_to(mod[i, bsz].reshape(1, 1, N_MOD * d), (bsz, 1, N_MOD * d))
        ln = (row(ln1_g, i), row(ln1_b, i), w_ffn_in[i].astype(BF16), w_ffn_out[i].astype(BF16),
              row(ln2_g, i), row(ln2_b, i))
        j = i // 2
        if i % 2 == 0:
            h_lat, h_ctx = _mla_layer(h_lat, h_ctx, mod_lat, mod_ctx, ln, mla_w_in[j], mla_q_norm[j],
                                      mla_kv_norm[j], mla_w_q_up[j], mla_w_kv_up[j], mla_w_out[j],
                                      alpha=alpha, last=last)
        else:
            h_lat, h_ctx = _gdn_layer(h_lat, h_ctx, mod_lat, mod_ctx, ln, gdn_w_in[j], gdn_conv[j],
                                      gdn_a_log[j], gdn_dt_bias[j], gdn_norm[j], gdn_w_out[j],
                                      alpha=alpha, last=last)
    return h_lat
```

```python
import functools
import math

import jax
import jax.numpy as jnp
from jax import lax
from jax.experimental import pallas as pl
from jax.experimental.pallas import tpu as pltpu

F32 = jnp.float32
BF16 = jnp.bfloat16

N_MOD = 6
LN_EPS = 1e-5
RMS_EPS = 1e-6
GRID_W = 64
ROPE_THETA = 10000.0

MLA_NOPE = 128
MLA_ROPE = 64
MLA_V = 128
MLA_QK = MLA_NOPE + 2 * MLA_ROPE

GDN_DK = 128
GDN_DV = 128
GDN_CONV = 5
CHUNK = 64
CONV_HALO = 8

V7X_VMEM_LIMIT = 56 * 1024 * 1024
LANES = 128


def _cparams(sem):
    return pltpu.CompilerParams(dimension_semantics=sem, vmem_limit_bytes=V7X_VMEM_LIMIT)


def _const_spec(shape):
    nd = len(shape)
    return pl.BlockSpec(shape, lambda *_: (0,) * nd, pipeline_mode=pl.Buffered(1))


def _dot(a, b):
    return jnp.dot(a, b, preferred_element_type=F32)


def _dot_nt(a, b):
    return lax.dot_general(a, b, (((1,), (1,)), ((), ())), preferred_element_type=F32)


def _split_bf16(a):
    hi = a.astype(BF16)
    lo = (a - hi.astype(F32)).astype(BF16)
    return hi, lo


def _dot3(a, b):
    ah, al = _split_bf16(a)
    bh, bl = _split_bf16(b)
    return _dot(jnp.concatenate([ah, ah, al], axis=1), jnp.concatenate([bh, bl, bh], axis=0))


def _silu(x):
    return x * jax.nn.sigmoid(x)


def _layer_norm(x, g, b):
    mu = jnp.mean(x, axis=-1, keepdims=True)
    xc = x - mu
    var = jnp.mean(xc * xc, axis=-1, keepdims=True)
    return xc * lax.rsqrt(var + LN_EPS) * g + b


def _rms_norm(x, g):
    return x * lax.rsqrt(jnp.mean(x * x, axis=-1, keepdims=True) + RMS_EPS) * g


def _mod_kernel(c_ref, w_ref, b_ref, o_ref):
    a = _silu(c_ref[...]).astype(BF16)
    o_ref[0] = _dot(a, w_ref[0].astype(BF16)) + b_ref[0]


def _mod_call(cc, w_mod, b_mod):
    depth, d, n = w_mod.shape
    tn = 1536
    assert n % tn == 0
    return pl.pallas_call(
        _mod_kernel,
        grid=(depth, n // tn),
        in_specs=[
            pl.BlockSpec((8, d), lambda l, j: (0, 0)),
            pl.BlockSpec((1, d, tn), lambda l, j: (l, 0, j)),
            pl.BlockSpec((1, 1, tn), lambda l, j: (l, 0, j)),
        ],
        out_specs=pl.BlockSpec((1, 8, tn), lambda l, j: (l, 0, j)),
        out_shape=jax.ShapeDtypeStruct((depth, 8, n), F32),
        compiler_params=_cparams(("parallel", "parallel")),
        name="mod",
    )(cc, w_mod, b_mod.reshape(depth, 1, n))


def _mod_spec(d, j):
    return pl.BlockSpec((1, 1, d), lambda b, i: (b, 0, j))


VT_ROWS = MLA_V + 16


def _mla_proj_kernel(h_ref, sh_ref, sc_ref, win_ref, qn_ref, kvn_ref, wqt_ref, wk_ref, wvt_ref,
                     qmt_ref, km_ref, qt_ref, k_ref, vt_ref, *, heads, q_rank, kv_rank):
    x = h_ref[0]
    tm = x.shape[0]
    u = x * (1.0 + sc_ref[0]) + sh_ref[0]
    lat = _dot(u.astype(BF16), win_ref[...])
    cq = _rms_norm(lat[:, :q_rank], qn_ref[...]).astype(BF16)
    ckv = _rms_norm(lat[:, q_rank:q_rank + kv_rank], kvn_ref[...]).astype(BF16)
    kr2 = lat[:, q_rank + kv_rank:]
    qt_all = _dot_nt(wqt_ref[...], cq)
    k_all = _dot(ckv, wk_ref[...])
    vt_all = _dot_nt(wvt_ref[...], ckv)
    qmt = qmt_ref[...]
    t = kr2 * km_ref[...]
    krot2 = (t + pltpu.roll(t, MLA_ROPE, axis=1)).astype(BF16)
    ones_rows = jnp.where(lax.broadcasted_iota(jnp.int32, (VT_ROWS - MLA_V, tm), 0) == 0, 1.0, 0.0).astype(BF16)
    for hh in range(heads):
        qt_ref[0, hh] = (qt_all[hh * MLA_QK:(hh + 1) * MLA_QK, :] * qmt).astype(BF16)
        k_ref[0, hh, :, :MLA_NOPE] = k_all[:, hh * MLA_NOPE:(hh + 1) * MLA_NOPE].astype(BF16)
        k_ref[0, hh, :, MLA_NOPE:] = krot2
        vt_ref[0, hh, 0, :MLA_V, :] = vt_all[hh * MLA_V:(hh + 1) * MLA_V, :].astype(BF16)
        vt_ref[0, hh, 0, MLA_V:, :] = ones_rows


def _mla_proj_call(h, mod, w_in, qn, kvn, wqt, wk, wvt, qmt, km, *, tm, heads):
    b, t, d = h.shape
    q_rank, kv_rank = qn.shape[1], kvn.shape[1]
    assert t % tm == 0
    kern = functools.partial(_mla_proj_kernel, heads=heads, q_rank=q_rank, kv_rank=kv_rank)
    return pl.pallas_call(
        kern,
        grid=(b, t // tm),
        in_specs=[
            pl.BlockSpec((1, tm, d), lambda bb, i: (bb, i, 0)),
            _mod_spec(d, 0), _mod_spec(d, 1),
            _const_spec(w_in.shape), _const_spec(qn.shape), _const_spec(kvn.shape),
            _const_spec(wqt.shape), _const_spec(wk.shape), _const_spec(wvt.shape),
            pl.BlockSpec((MLA_QK, tm), lambda bb, i: (0, i)),
            pl.BlockSpec((tm, 2 * MLA_ROPE), lambda bb, i: (i, 0)),
        ],
        out_specs=[
            pl.BlockSpec((1, heads, MLA_QK, tm), lambda bb, i: (bb, 0, 0, i)),
            pl.BlockSpec((1, heads, tm, MLA_QK), lambda bb, i: (bb, 0, i, 0)),
            pl.BlockSpec((1, heads, 1, VT_ROWS, tm), lambda bb, i: (bb, 0, i, 0, 0)),
        ],
        out_shape=[
            jax.ShapeDtypeStruct((b, heads, MLA_QK, t), BF16),
            jax.ShapeDtypeStruct((b, heads, t, MLA_QK), BF16),
            jax.ShapeDtypeStruct((b, heads, t // tm, VT_ROWS, tm), BF16),
        ],
        compiler_params=_cparams(("parallel", "parallel")),
        name="mla_proj",
    )(h, mod, mod, w_in, qn, kvn, wqt, wk, wvt, qmt, km)


QGROUP = 256


def _attn_kernel(*refs, slabs_per_step, n_lat, n_groups):
    if n_lat:
        qt_ref, kc_ref, vtc_ref, kl_ref, vtl_ref, o_ref, m_ref, acc_ref, st_ref = refs
    else:
        qt_ref, kc_ref, vtc_ref, o_ref, m_ref, acc_ref = refs

    def scores(kblk, g):
        return _dot(kblk, qt_ref[0, 0, :, g * QGROUP:(g + 1) * QGROUP])

    def softmax_pv(st, vt_slabs, g, first):
        slab = vt_slabs[0].shape[1]
        cols = slice(g * QGROUP, (g + 1) * QGROUP)
        cmax = jnp.max(st, axis=0, keepdims=True)
        if first:
            m_new = cmax
        else:
            m_old = m_ref[:, cols]
            m_new = jnp.maximum(m_old, cmax)
        pt = jnp.exp2(st - m_new).astype(BF16)
        pv = None
        for s, vt in enumerate(vt_slabs):
            term = _dot(vt, pt[s * slab:(s + 1) * slab, :])
            pv = term if pv is None else pv + term
        if first:
            acc_ref[:, cols] = pv
        else:
            acc_ref[:, cols] = acc_ref[:, cols] * jnp.exp2(m_old - m_new) + pv
        m_ref[:, cols] = m_new

    kc = kc_ref[0, 0]
    vtc = [vtc_ref[0, 0, s] for s in range(vtc_ref.shape[2])]
    sts = [scores(kc, g) for g in range(n_groups)]
    for g in range(n_groups):
        softmax_pv(sts[g], vtc, g, True)
    if n_lat:
        slab = vtl_ref.shape[4]
        tk = slabs_per_step * slab
        k0 = kl_ref[0, 0, pl.ds(0, tk), :]
        for g in range(n_groups):
            st_ref[g] = scores(k0, g)

        def body(j, carry):
            off = pl.multiple_of(jnp.minimum(j + 1, n_lat - 1) * tk, tk)
            k_next = kl_ref[0, 0, pl.ds(off, tk), :]
            vts = [vtl_ref[0, 0, j * slabs_per_step + s] for s in range(slabs_per_step)]
            nxt = scores(k_next, 0)
            for g in range(n_groups):
                cur, nxt = nxt, (scores(k_next, g + 1) if g + 1 < n_groups else None)
                softmax_pv(st_ref[g], vts, g, False)
                st_ref[g] = cur
            return carry
        lax.fori_loop(0, n_lat, body, 0)
    acc = acc_ref[...]
    ot = acc[:MLA_V] * (1.0 / acc[MLA_V:MLA_V + 1])
    o_ref[0] = ot.T.astype(o_ref.dtype)


def _attn_call(qt, k_ctx, vt_ctx, k_lat=None, vt_lat=None, *, tq, slabs_per_step=4):
    b, h, dqk, t = qt.shape
    tc = k_ctx.shape[2]
    assert t % tq == 0 and tq % QGROUP == 0
    n_lat = 0
    in_specs = [
        pl.BlockSpec((1, 1, dqk, tq), lambda bb, hh, i: (bb, hh, 0, i)),
        pl.BlockSpec((1, 1, tc, dqk), lambda bb, hh, i: (bb, hh, 0, 0)),
        pl.BlockSpec((1, 1) + vt_ctx.shape[2:], lambda bb, hh, i: (bb, hh, 0, 0, 0)),
    ]
    args = [qt, k_ctx, vt_ctx]
    if k_lat is not None:
        tl = k_lat.shape[2]
        ns = vt_lat.shape[2]
        assert ns % slabs_per_step == 0
        n_lat = ns // slabs_per_step
        in_specs += [
            pl.BlockSpec((1, 1, tl, dqk), lambda bb, hh, i: (bb, hh, 0, 0)),
            pl.BlockSpec((1, 1) + vt_lat.shape[2:], lambda bb, hh, i: (bb, hh, 0, 0, 0)),
        ]
        args += [k_lat, vt_lat]
    n_groups = tq // QGROUP
    kern = functools.partial(_attn_kernel, slabs_per_step=slabs_per_step, n_lat=n_lat, n_groups=n_groups)
    scratch = [pltpu.VMEM((1, tq), F32), pltpu.VMEM((VT_ROWS, tq), F32)]
    if n_lat:
        scratch.append(pltpu.VMEM((n_groups, slabs_per_step * vt_lat.shape[4], QGROUP), F32))
    return pl.pallas_call(
        kern,
        grid=(b, h, t // tq),
        in_specs=in_specs,
        out_specs=pl.BlockSpec((1, tq, MLA_V), lambda bb, hh, i: (bb, i, hh)),
        out_shape=jax.ShapeDtypeStruct((b, t, h * MLA_V), BF16),
        scratch_shapes=scratch,
        compiler_params=_cparams(("parallel", "parallel", "arbitrary")),
        name="attn_lat" if n_lat else "attn_ctx",
    )(*args)


def _post_tail(y_in, h_ref, g1_ref, sh2_ref, sc2_ref, g2_ref, wout_ref, l1g_ref, l1b_ref,
               wfi_ref, wfo_ref, l2g_ref, l2b_ref, o_ref, *, alpha, d_ff):
    y = _dot(y_in, wout_ref[...])
    h1 = _layer_norm(alpha * h_ref[0] + g1_ref[0] * y, l1g_ref[...], l1b_ref[...])
    u2 = h1 * (1.0 + sc2_ref[0]) + sh2_ref[0]
    gu = _dot(u2.astype(BF16), wfi_ref[...])
    act = _silu(gu[:, :d_ff]) * gu[:, d_ff:]
    f = _dot(act.astype(BF16), wfo_ref[...])
    o_ref[0] = _layer_norm(alpha * h1 + g2_ref[0] * f, l2g_ref[...], l2b_ref[...])


def _post_mla_kernel(a_ref, *rest, alpha, d_ff):
    _post_tail(a_ref[0], *rest, alpha=alpha, d_ff=d_ff)


def _post_gdn_kernel(of_ref, ob_ref, z_ref, nw_ref, *rest, alpha, d_ff, k_heads):
    nw = nw_ref[...]
    parts = []
    for kh in range(k_heads):
        o2 = of_ref[0, kh] + ob_ref[0, kh]
        for v in range(2):
            o = o2[:, v * GDN_DV:(v + 1) * GDN_DV]
            vh = 2 * kh + v
            z = z_ref[0, :, vh * GDN_DV:(vh + 1) * GDN_DV]
            parts.append((_rms_norm(o, nw) * _silu(z)).astype(BF16))
    _post_tail(jnp.concatenate(parts, axis=1), *rest, alpha=alpha, d_ff=d_ff)


def _post_call(mixer_in, h, mod, w_out, l1g, l1b, wfi, wfo, l2g, l2b, *, tm, alpha, gdn_norm=None):
    b, t, d = h.shape
    d_ff = wfo.shape[0]
    assert t % tm == 0
    row = lambda w: pl.BlockSpec((1, tm, w), lambda bb, i: (bb, i, 0))
    if gdn_norm is None:
        (a,) = mixer_in
        kern = functools.partial(_post_mla_kernel, alpha=alpha, d_ff=d_ff)
        head_specs = [row(a.shape[2])]
        head_args = [a]
    else:
        o_f, o_b, z = mixer_in
        kh = o_f.shape[1]
        kern = functools.partial(_post_gdn_kernel, alpha=alpha, d_ff=d_ff, k_heads=kh)
        hm = pl.BlockSpec((1, kh, tm, 2 * GDN_DV), lambda bb, i: (bb, 0, i, 0))
        head_specs = [hm, hm, row(z.shape[2]), _const_spec(gdn_norm.shape)]
        head_args = [o_f, o_b, z, gdn_norm]
    return pl.pallas_call(
        kern,
        grid=(b, t // tm),
        in_specs=head_specs + [
            row(d),
            _mod_spec(d, 2), _mod_spec(d, 3), _mod_spec(d, 4), _mod_spec(d, 5),
            _const_spec(w_out.shape), _const_spec(l1g.shape), _const_spec(l1b.shape),
            _const_spec(wfi.shape), _const_spec(wfo.shape),
            _const_spec(l2g.shape), _const_spec(l2b.shape),
        ],
        out_specs=row(d),
        out_shape=jax.ShapeDtypeStruct((b, t, d), F32),
        compiler_params=_cparams(("parallel", "parallel")),
        name="post_mla" if gdn_norm is None else "post_gdn",
    )(*head_args, h, mod, mod, mod, mod, w_out, l1g, l1b, wfi, wfo, l2g, l2b)


def _gdn_proj_kernel(h_ref, sh_ref, sc_ref, w_ref, qkv_ref, z_ref, ba_ref, *, qkv_w, z_w):
    u = h_ref[0] * (1.0 + sc_ref[0]) + sh_ref[0]
    p = _dot(u.astype(BF16), w_ref[...])
    qkv_ref[0] = p[:, :qkv_w]
    z_ref[0] = p[:, qkv_w:qkv_w + z_w]
    ba_ref[0] = p[:, qkv_w + z_w:]


def _gdn_proj_call(h, mod, w, *, tm, qkv_w, z_w):
    b, t, d = h.shape
    ba_w = w.shape[1] - qkv_w - z_w
    assert t % tm == 0
    row = lambda wd: pl.BlockSpec((1, tm, wd), lambda bb, i: (bb, i, 0))
    kern = functools.partial(_gdn_proj_kernel, qkv_w=qkv_w, z_w=z_w)
    return pl.pallas_call(
        kern,
        grid=(b, t // tm),
        in_specs=[row(d), _mod_spec(d, 0), _mod_spec(d, 1), _const_spec(w.shape)],
        out_specs=[row(qkv_w), row(z_w), row(ba_w)],
        out_shape=[
            jax.ShapeDtypeStruct((b, t, qkv_w), F32),
            jax.ShapeDtypeStruct((b, t, z_w), F32),
            jax.ShapeDtypeStruct((b, t, ba_w), F32),
        ],
        compiler_params=_cparams(("parallel", "parallel")),
        name="gdn_proj",
    )(h, mod, mod, w)


def _chunk_cumsum(g, suffix):
    n = g.shape[0]
    pos = lax.broadcasted_iota(jnp.int32, g.shape, 0) % CHUNK
    s = 1
    while s < CHUNK:
        if suffix:
            shifted = pltpu.roll(g, n - s, axis=0)
            ok = pos < CHUNK - s
        else:
            shifted = pltpu.roll(g, s, axis=0)
            ok = pos >= s
        g = g + jnp.where(ok, shifted, 0.0)
        s *= 2
    return g


CONV_ROWS = 64
CONV_COLS = 512


def _gdn_conv_kernel(prev_ref, cur_ref, next_ref, ba_ref, cw_ref, al_ref, dtb_ref,
                     q_ref, k_ref, kt_ref, v_ref, gt_ref, gr_ref, xe_ref, y_ref, *, k_heads, tm):
    i = pl.program_id(1)
    n_i = pl.num_programs(1)
    kw = k_heads * GDN_DK
    c_all = xe_ref.shape[1]
    n_chunks = tm // CHUNK
    xe_ref[0:CONV_HALO] = jnp.where(i > 0, prev_ref[0], 0.0)
    xe_ref[CONV_HALO:CONV_HALO + tm] = cur_ref[0]
    xe_ref[CONV_HALO + tm:] = jnp.where(i < n_i - 1, next_ref[0], 0.0)
    half = GDN_CONV // 2
    for c0 in range(0, c_all, CONV_COLS):
        w = cw_ref[:, c0:c0 + CONV_COLS]
        for r0 in range(0, tm, CONV_ROWS):
            acc = None
            for j in range(GDN_CONV):
                start = CONV_HALO - half + j + r0
                term = xe_ref[start:start + CONV_ROWS, c0:c0 + CONV_COLS] * w[j:j + 1, :]
                acc = term if acc is None else acc + term
            y_ref[r0:r0 + CONV_ROWS, c0:c0 + CONV_COLS] = _silu(acc)
    for kh in range(k_heads):
        qh = y_ref[:, kh * GDN_DK:(kh + 1) * GDN_DK]
        q_ref[0, kh] = (qh * lax.rsqrt(jnp.sum(qh * qh, axis=-1, keepdims=True) + RMS_EPS)
                        * (GDN_DK ** -0.5)).astype(BF16)
        kk = y_ref[:, kw + kh * GDN_DK:kw + (kh + 1) * GDN_DK]
        kn = kk * lax.rsqrt(jnp.sum(kk * kk, axis=-1, keepdims=True) + RMS_EPS)
        k_ref[0, kh] = kn.astype(BF16)
        knt = kn.T.astype(BF16)
        for n in range(n_chunks):
            kt_ref[0, kh, n] = knt[:, n * CHUNK:(n + 1) * CHUNK]
        v_ref[0, kh] = y_ref[:, 2 * kw + kh * 2 * GDN_DV:2 * kw + (kh + 1) * 2 * GDN_DV].astype(BF16)
    ba = ba_ref[0]
    lane = lax.broadcasted_iota(jnp.int32, ba.shape, 1) % LANES
    beta = jax.nn.sigmoid(ba)
    g = -jnp.exp(al_ref[...]) * jax.nn.softplus(ba + dtb_ref[...])
    g = jnp.where((lane >= 4) & (lane < 8), g, 0.0)
    gcum = jnp.where(lane < 6, _chunk_cumsum(g, False), _chunk_cumsum(g, True))
    gates = jnp.where(lane < 4, beta, gcum)
    for kh in range(k_heads):
        gk = gates[:, kh * LANES:(kh + 1) * LANES]
        gt_ref[0, kh] = gk
        rows = gk.T[0:8, :]
        for n in range(n_chunks):
            piece = rows[:, n * CHUNK:(n + 1) * CHUNK]
            gr_ref[0, kh, n] = jnp.concatenate([piece, piece], axis=1)


def _gdn_conv_call(qkv, ba, conv_w, a_log_ext, dtb_ext, *, tm, k_heads):
    b, t, c = qkv.shape
    assert t % tm == 0 and tm % CHUNK == 0 and tm % CONV_HALO == 0
    assert tm % CONV_ROWS == 0 and c % CONV_COLS == 0
    r = tm // CONV_HALO
    nblk8 = t // CONV_HALO
    ncb = tm // CHUNK
    kern = functools.partial(_gdn_conv_kernel, k_heads=k_heads, tm=tm)
    hm = lambda w: pl.BlockSpec((1, k_heads, tm, w), lambda bb, i: (bb, 0, i, 0))
    per_chunk = lambda rr, w: pl.BlockSpec((1, k_heads, ncb, rr, w), lambda bb, i: (bb, 0, i, 0, 0))
    return pl.pallas_call(
        kern,
        grid=(b, t // tm),
        in_specs=[
            pl.BlockSpec((1, CONV_HALO, c), lambda bb, i: (bb, jnp.maximum(i * r - 1, 0), 0)),
            pl.BlockSpec((1, tm, c), lambda bb, i: (bb, i, 0)),
            pl.BlockSpec((1, CONV_HALO, c), lambda bb, i: (bb, jnp.minimum((i + 1) * r, nblk8 - 1), 0)),
            pl.BlockSpec((1, tm, ba.shape[2]), lambda bb, i: (bb, i, 0)),
            _const_spec(conv_w.shape), _const_spec(a_log_ext.shape), _const_spec(dtb_ext.shape),
        ],
        out_specs=[hm(GDN_DK), hm(GDN_DK), per_chunk(GDN_DK, CHUNK), hm(2 * GDN_DV), hm(LANES),
                   per_chunk(8, LANES)],
        out_shape=[
            jax.ShapeDtypeStruct((b, k_heads, t, GDN_DK), BF16),
            jax.ShapeDtypeStruct((b, k_heads, t, GDN_DK), BF16),
            jax.ShapeDtypeStruct((b, k_heads, t // CHUNK, GDN_DK, CHUNK), BF16),
            jax.ShapeDtypeStruct((b, k_heads, t, 2 * GDN_DV), BF16),
            jax.ShapeDtypeStruct((b, k_heads, t, LANES), F32),
            jax.ShapeDtypeStruct((b, k_heads, t // CHUNK, 8, LANES), F32),
        ],
        scratch_shapes=[pltpu.VMEM((tm + 2 * CONV_HALO, c), F32), pltpu.VMEM((tm, c), F32)],
        compiler_params=_cparams(("parallel", "parallel")),
        name="gdn_conv",
    )(qkv, qkv, qkv, ba, conv_w, a_log_ext, dtb_ext)


WAVE_LAG = 4


def _dot_split(a_hi, a_lo, b_hi, b_lo):
    return _dot(jnp.concatenate([a_hi, a_lo], axis=1), jnp.concatenate([b_hi, b_lo, b_hi, b_lo], axis=0))


def _gdn_scan_kernel(qf_ref, kf_ref, ktf_ref, vf_ref, gf_ref, grf_ref,
                     qb_ref, kb_ref, ktb_ref, vb_ref, gb_ref, grb_ref, s0_ref,
                     of_ref, ob_ref, sfin_ref, s_ref, *, hg, n_waves):
    n = pl.program_id(2)

    @pl.when(n == 0)
    def _():
        s_ref[...] = s0_ref[0]

    c = CHUNK
    row = lax.broadcasted_iota(jnp.int32, (c, LANES), 0)
    col = lax.broadcasted_iota(jnp.int32, (c, LANES), 1) % c
    dirs = ((qf_ref, kf_ref, ktf_ref, vf_ref, gf_ref, grf_ref), (qb_ref, kb_ref, ktb_ref, vb_ref, gb_ref, grb_ref))

    eye = jnp.where(row == col, 1.0, 0.0)
    zero_rows = jnp.zeros((c, 2 * GDN_DV), BF16)
    pairs = []
    blk_r, blk_c = row, col
    b = 1
    while b < c:
        pair = (blk_r // 2 == blk_c // 2) & (blk_r != blk_c)
        pairs.append(pair if b == 1 else jnp.where(pair, 1.0, 0.0).astype(BF16))
        blk_r, blk_c = blk_r // 2, blk_c // 2
        b *= 2

    def wave(heads):
        chains = []
        for j in heads:
            for d in range(2):
                q_ref, k_ref, kt_ref, v_ref, g_ref, gr_ref = dirs[d]
                qc, kc, v2, gt = q_ref[0, j], k_ref[0, j], v_ref[0, j], g_ref[0, j]
                gtt = gr_ref[0, j, 0]
                ktc = kt_ref[0, j, 0]
                kq = _dot_nt(jnp.concatenate([kc, qc], axis=0), jnp.concatenate([kc, kc], axis=0))
                for v in range(2):
                    chains.append(dict(j=j, d=d, v=v, qc=qc, kc=kc, vv=v2[:, v * GDN_DV:(v + 1) * GDN_DV],
                                       gt=gt, gtt=gtt, ktc=ktc, kk=kq[:c], qk=kq[c:]))
            yield
        for i, ch in enumerate(chains):
            slot = 2 * ch["d"] + ch["v"]
            if ch["d"] == 1:
                incl, strict = row <= col, row < col
            else:
                incl, strict = row >= col, row > col
            beta_col = jnp.broadcast_to(ch["gt"][:, slot:slot + 1], (c, LANES))
            g_col = jnp.broadcast_to(ch["gt"][:, 4 + slot:5 + slot], (c, LANES))
            g_row = ch["gtt"][4 + slot:5 + slot, :]
            decay = jnp.where(incl, jnp.exp(jnp.where(incl, g_col - g_row, 0.0)), 0.0)
            eg_col = jnp.exp(g_col)
            ch["g_row"] = g_row
            lmat = jnp.where(strict, ch["kk"] * beta_col * decay, 0.0)
            ch.update(
                lm=lmat.astype(BF16), tinv=eye - jnp.where(pairs[0], lmat, 0.0),
                attn=(ch["qk"] * decay)[:, :c].astype(BF16),
                eg_col=eg_col, g_col=g_col,
                rhs=jnp.concatenate([ch["vv"].astype(F32) * beta_col,
                                     ch["kc"].astype(F32) * (beta_col * eg_col)], axis=1).astype(BF16))
            if i % 4 == 3:
                yield
        for pair_bf in pairs[1:]:
            for ch in chains:
                ch["t_hi"], ch["t_lo"] = _split_bf16(ch["tinv"])
                ch["w"] = _dot(ch["lm"] * pair_bf, jnp.concatenate([ch["t_hi"], ch["t_lo"]], axis=0))
            yield
            for ch in chains:
                w_hi, w_lo = _split_bf16(ch["w"])
                ch["tinv"] = ch["tinv"] - _dot_split(ch["t_hi"], ch["t_lo"], w_hi, w_lo)
            yield
        for ch in chains:
            t_hi, t_lo = _split_bf16(ch["tinv"])
            r = ch["rhs"]
            ch["x"] = _dot(jnp.concatenate([t_hi, t_lo], axis=1),
                           jnp.concatenate([r, zero_rows, r, zero_rows], axis=0))
        yield
        for ch in chains:
            ch["s_idx"] = ch["j"] * 4 + 2 * ch["d"] + ch["v"]
            ch["s_old"] = s_ref[ch["s_idx"]]
            ch["wq"] = _dot(jnp.concatenate([ch["x"][:, GDN_DV:].astype(BF16), ch["qc"]], axis=0),
                            ch["s_old"].astype(BF16))
        yield
        for ch in chains:
            g_col = ch["g_col"]
            ch["g_last"] = g_col[0:1, :] if ch["d"] == 1 else g_col[c - 1:c, :]
            v_new = (ch["x"][:, :GDN_DV] - ch["wq"][:c]).astype(BF16)
            dec_row = jnp.exp(ch["g_last"] - ch["g_row"])[:, :c]
            k_dec_t = (ch["ktc"].astype(F32) * dec_row).astype(BF16)
            ch["av"] = _dot(ch["attn"], v_new)
            ch["kv"] = _dot(k_dec_t, v_new)
        yield
        outs = {}
        for ch in chains:
            s_ref[ch["s_idx"]] = ch["s_old"] * jnp.exp(ch["g_last"]) + ch["kv"]
            outs[ch["j"], ch["d"], ch["v"]] = ch["eg_col"] * ch["wq"][c:] + ch["av"]
        for j in heads:
            for d, o_ref in enumerate((of_ref, ob_ref)):
                o_ref[0, j] = jnp.concatenate([outs[j, d, 0], outs[j, d, 1]], axis=1)

    waves = [wave(list(range(w, hg, n_waves))) for w in range(n_waves)]
    live = []
    pending = list(waves)
    while live or pending:
        if pending:
            live.append(pending.pop(0))
        for _ in range(WAVE_LAG if pending else 1):
            for g in list(live):
                if next(g, "done") == "done":
                    live.remove(g)

    @pl.when(n == pl.num_programs(2) - 1)
    def _():
        sfin_ref[0] = s_ref[...]


def _gdn_scan_call(q, k, kt, v, gates, gate_rows, s0, *, hg):
    b, kh, t, _ = q.shape
    assert kh % hg == 0 and t % CHUNK == 0
    nc = t // CHUNK
    ng = kh // hg
    fwd = lambda w: pl.BlockSpec((1, hg, CHUNK, w), lambda bb, g, n: (bb, g, n, 0))
    bwd = lambda w: pl.BlockSpec((1, hg, CHUNK, w), lambda bb, g, n: (bb, g, nc - 1 - n, 0))
    fwd_c = lambda r, w: pl.BlockSpec((1, hg, 1, r, w), lambda bb, g, n: (bb, g, n, 0, 0))
    bwd_c = lambda r, w: pl.BlockSpec((1, hg, 1, r, w), lambda bb, g, n: (bb, g, nc - 1 - n, 0, 0))
    st = pl.BlockSpec((1, hg * 4, GDN_DK, GDN_DV), lambda bb, g, n: (bb, g, 0, 0))
    kern = functools.partial(_gdn_scan_kernel, hg=hg, n_waves=1)
    return pl.pallas_call(
        kern,
        grid=(b, ng, nc),
        in_specs=[fwd(GDN_DK), fwd(GDN_DK), fwd_c(GDN_DK, CHUNK), fwd(2 * GDN_DV), fwd(LANES), fwd_c(8, LANES),
                  bwd(GDN_DK), bwd(GDN_DK), bwd_c(GDN_DK, CHUNK), bwd(2 * GDN_DV), bwd(LANES), bwd_c(8, LANES),
                  st],
        out_specs=[fwd(2 * GDN_DV), bwd(2 * GDN_DV), st],
        out_shape=[
            jax.ShapeDtypeStruct((b, kh, t, 2 * GDN_DV), F32),
            jax.ShapeDtypeStruct((b, kh, t, 2 * GDN_DV), F32),
            jax.ShapeDtypeStruct(s0.shape, F32),
        ],
        scratch_shapes=[pltpu.VMEM((hg * 4, GDN_DK, GDN_DV), F32)],
        compiler_params=_cparams(("parallel", "parallel", "arbitrary")),
        name="gdn_scan",
    )(q, k, kt, v, gates, gate_rows, q, k, kt, v, gates, gate_rows, s0)


def _rope_tables(t, q_scale):
    pos = jnp.arange(t)
    row = (pos // GRID_W).astype(F32)
    col = (pos % GRID_W).astype(F32)
    n_freq = MLA_ROPE // 4
    inv_freq = ROPE_THETA ** (-(2.0 * jnp.arange(n_freq, dtype=F32)) / (MLA_ROPE // 2))
    ar = row[:, None] * inv_freq
    ac = col[:, None] * inv_freq
    cs = jnp.concatenate([jnp.cos(ar), jnp.cos(ar), jnp.cos(ac), jnp.cos(ac)], axis=1)
    sn = jnp.concatenate([-jnp.sin(ar), jnp.sin(ar), -jnp.sin(ac), jnp.sin(ac)], axis=1)
    km = jnp.concatenate([cs, sn], axis=1)
    qm = q_scale * jnp.concatenate([jnp.ones((t, MLA_NOPE), F32), cs, sn], axis=1)
    return qm, km


def _no_rope_tables(t, q_scale):
    one = jnp.ones((t, MLA_ROPE), F32)
    zero = jnp.zeros((t, MLA_ROPE), F32)
    km = jnp.concatenate([one, zero], axis=1)
    qm = q_scale * jnp.concatenate([jnp.ones((t, MLA_NOPE), F32), one, zero], axis=1)
    return qm, km


def _swap_perm():
    f = MLA_ROPE // 4
    return jnp.array(list(range(f, 2 * f)) + list(range(0, f)) + list(range(3 * f, 4 * f)) + list(range(2 * f, 3 * f)))


def _mla_layer(h_lat, h_ctx, mod_lat, mod_ctx, ln, w_in, q_norm, kv_norm, w_q_up, w_kv_up, w_out, *, alpha, last):
    bsz, seq, d = h_lat.shape
    tctx = h_ctx.shape[1]
    heads = d // 128
    q_rank, kv_rank = q_norm.shape[0], kv_norm.shape[0]
    tm_lat, tm_ctx = min(256, seq), min(256, tctx)
    q_scale = (MLA_NOPE + MLA_ROPE) ** -0.5 * math.log2(math.e)
    perm = _swap_perm()
    rope_cols = w_in[:, q_rank + kv_rank:]
    w_in_ext = jnp.concatenate([w_in, rope_cols[:, perm]], axis=1).astype(BF16)
    wq = w_q_up.reshape(q_rank, heads, MLA_NOPE + MLA_ROPE)
    wq_ext = jnp.concatenate([wq, wq[:, :, MLA_NOPE:][:, :, perm]], axis=2)
    wqt = wq_ext.reshape(q_rank, heads * MLA_QK).T.astype(BF16)
    wkv = w_kv_up.reshape(kv_rank, heads, MLA_NOPE + MLA_V)
    wk = wkv[:, :, :MLA_NOPE].reshape(kv_rank, -1).astype(BF16)
    wvt = wkv[:, :, MLA_NOPE:].reshape(kv_rank, -1).T.astype(BF16)
    qn, kvn = q_norm.reshape(1, -1), kv_norm.reshape(1, -1)
    qm_l, km_l = _rope_tables(seq, q_scale)
    qm_c, km_c = _no_rope_tables(tctx, q_scale)
    q_l, k_l, v_l = _mla_proj_call(h_lat, mod_lat, w_in_ext, qn, kvn, wqt, wk, wvt, qm_l.T, km_l,
                                   tm=tm_lat, heads=heads)
    q_c, k_c, v_c = _mla_proj_call(h_ctx, mod_ctx, w_in_ext, qn, kvn, wqt, wk, wvt, qm_c.T, km_c,
                                   tm=tm_ctx, heads=heads)
    o_l = _attn_call(q_l, k_c, v_c, k_l, v_l, tq=min(2048, seq))
    w_out = w_out.astype(BF16)
    h_lat_new = _post_call((o_l,), h_lat, mod_lat, w_out, *ln, tm=tm_lat, alpha=alpha)
    if not last:
        o_c = _attn_call(q_c, k_c, v_c, tq=tm_ctx)
        h_ctx = _post_call((o_c,), h_ctx, mod_ctx, w_out, *ln, tm=tm_ctx, alpha=alpha)
    return h_lat_new, h_ctx


def _gdn_layer(h_lat, h_ctx, mod_lat, mod_ctx, ln, w_in, conv_w, a_log, dt_bias, norm_w, w_out, *, alpha, last):
    bsz, seq, d = h_lat.shape
    tctx = h_ctx.shape[1]
    k_heads = d // 128
    v_heads = 2 * k_heads
    kw, vw = k_heads * GDN_DK, v_heads * GDN_DV
    qkv_w = 2 * kw + vw
    tm_lat, tm_ctx = min(256, seq), min(256, tctx)
    w_ba = w_in[:, qkv_w + vw:].reshape(d, 2, 2, k_heads, 2)
    w_ba = jnp.transpose(w_ba, (0, 3, 2, 1, 4)).reshape(d, k_heads, 8)
    w_ba = jnp.pad(w_ba, ((0, 0), (0, 0), (0, LANES - 8))).reshape(d, k_heads * LANES)
    w_ext = jnp.concatenate([w_in[:, :qkv_w + vw], w_ba], axis=1).astype(BF16)

    def gate_lanes(p):
        p = jnp.transpose(p.astype(F32).reshape(2, k_heads, 2), (1, 0, 2)).reshape(k_heads, 4)
        p = jnp.pad(p, ((0, 0), (4, LANES - 8)))
        return p.reshape(1, k_heads * LANES)

    al_ext, dtb_ext = gate_lanes(a_log), gate_lanes(dt_bias)
    s = jnp.zeros((bsz, k_heads * 4, GDN_DK, GDN_DV), F32)
    streams = []
    for h_in, m_in, tm in ((h_ctx, mod_ctx, tm_ctx), (h_lat, mod_lat, tm_lat)):
        qkv, z, ba = _gdn_proj_call(h_in, m_in, w_ext, tm=tm, qkv_w=qkv_w, z_w=vw)
        qh, kh, kt, vh, gates, gate_rows = _gdn_conv_call(qkv, ba, conv_w, al_ext, dtb_ext, tm=tm, k_heads=k_heads)
        o_f, o_b, s = _gdn_scan_call(qh, kh, kt, vh, gates, gate_rows, s, hg=4)
        streams.append((o_f, o_b, z))
    w_out = w_out.astype(BF16)
    nw = norm_w.reshape(1, -1)
    h_lat_new = _post_call(streams[1], h_lat, mod_lat, w_out, *ln, tm=tm_lat, alpha=alpha, gdn_norm=nw)
    if not last:
        h_ctx = _post_call(streams[0], h_ctx, mod_ctx, w_out, *ln, tm=tm_ctx, alpha=alpha, gdn_norm=nw)
    return h_lat_new, h_ctx


def kernel(x, c, ctx, c_ctx, w_mod, b_mod, ln1_g, ln1_b, ln2_g, ln2_b, w_ffn_in, w_ffn_out, mla_w_in, mla_q_norm, mla_kv_norm, mla_w_q_up, mla_w_kv_up, mla_w_out, gdn_w_in, gdn_conv, gdn_a_log, gdn_dt_bias, gdn_norm, gdn_w_out):
    bsz, _, d = x.shape
    depth = w_mod.shape[0]
    alpha = (2.0 * depth) ** 0.25

    cc = jnp.zeros((8, d), F32).at[:bsz].set(c).at[bsz].set(c_ctx)
    mod = _mod_call(cc, w_mod, b_mod)

    row = lambda a, i: a[i].reshape(1, -1)
    h_lat, h_ctx = x, ctx
    for i in range(depth):
        last = i == depth - 1
        mod_lat = mod[i, :bsz].reshape(bsz, 1, N_MOD * d)
        mod_ctx = jnp.broadcast_to(mod[i, bsz].reshape(1, 1, N_MOD * d), (bsz, 1, N_MOD * d))
        ln = (row(ln1_g, i), row(ln1_b, i), w_ffn_in[i].astype(BF16), w_ffn_out[i].astype(BF16),
              row(ln2_g, i), row(ln2_b, i))
        j = i // 2
        if i % 2 == 0:
            h_lat, h_ctx = _mla_layer(h_lat, h_ctx, mod_lat, mod_ctx, ln, mla_w_in[j], mla_q_norm[j],
                                      mla_kv_norm[j], mla_w_q_up[j], mla_w_kv_up[j], mla_w_out[j],
                                      alpha=alpha, last=last)
        else:
            h_lat, h_ctx = _gdn_layer(h_lat, h_ctx, mod_lat, mod_ctx, ln, gdn_w_in[j], gdn_conv[j],
                                      gdn_a_log[j], gdn_dt_bias[j], gdn_norm[j], gdn_w_out[j],
                                      alpha=alpha, last=last)
    return h_lat
```

```python
import functools
import math

import jax
import jax.numpy as jnp
from jax import lax
from jax.experimental import pallas as pl
from jax.experimental.pallas import tpu as pltpu

F32 = jnp.float32
BF16 = jnp.bfloat16

N_MOD = 6
LN_EPS = 1e-5
RMS_EPS = 1e-6
GRID_W = 64
ROPE_THETA = 10000.0

MLA_NOPE = 128
MLA_ROPE = 64
MLA_V = 128
MLA_QK = MLA_NOPE + 2 * MLA_ROPE

GDN_DK = 128
GDN_DV = 128
GDN_CONV = 5
CHUNK = 64
CONV_HALO = 8

V7X_VMEM_LIMIT = 56 * 1024 * 1024
LANES = 128


def _cparams(sem):
    return pltpu.CompilerParams(dimension_semantics=sem, vmem_limit_bytes=V7X_VMEM_LIMIT)


def _const_spec(shape):
    nd = len(shape)
    return pl.BlockSpec(shape, lambda *_: (0,) * nd, pipeline_mode=pl.Buffered(1))


def _dot(a, b):
    return jnp.dot(a, b, preferred_element_type=F32)


def _dot_nt(a, b):
    return lax.dot_general(a, b, (((1,), (1,)), ((), ())), preferred_element_type=F32)


def _split_bf16(a):
    hi = a.astype(BF16)
    lo = (a - hi.astype(F32)).astype(BF16)
    return hi, lo


def _dot3(a, b):
    ah, al = _split_bf16(a)
    bh, bl = _split_bf16(b)
    return _dot(jnp.concatenate([ah, ah, al], axis=1), jnp.concatenate([bh, bl, bh], axis=0))


def _silu(x):
    return x * jax.nn.sigmoid(x)


def _layer_norm(x, g, b):
    mu = jnp.mean(x, axis=-1, keepdims=True)
    xc = x - mu
    var = jnp.mean(xc * xc, axis=-1, keepdims=True)
    return xc * lax.rsqrt(var + LN_EPS) * g + b


def _rms_norm(x, g):
    return x * lax.rsqrt(jnp.mean(x * x, axis=-1, keepdims=True) + RMS_EPS) * g


def _mod_kernel(c_ref, w_ref, b_ref, o_ref):
    a = _silu(c_ref[...]).astype(BF16)
    o_ref[0] = _dot(a, w_ref[0].astype(BF16)) + b_ref[0]


def _mod_call(cc, w_mod, b_mod):
    depth, d, n = w_mod.shape
    tn = 1536
    assert n % tn == 0
    return pl.pallas_call(
        _mod_kernel,
        grid=(depth, n // tn),
        in_specs=[
            pl.BlockSpec((8, d), lambda l, j: (0, 0)),
            pl.BlockSpec((1, d, tn), lambda l, j: (l, 0, j)),
            pl.BlockSpec((1, 1, tn), lambda l, j: (l, 0, j)),
        ],
        out_specs=pl.BlockSpec((1, 8, tn), lambda l, j: (l, 0, j)),
        out_shape=jax.ShapeDtypeStruct((depth, 8, n), F32),
        compiler_params=_cparams(("parallel", "parallel")),
        name="mod",
    )(cc, w_mod, b_mod.reshape(depth, 1, n))


def _mod_spec(d, j):
    return pl.BlockSpec((1, 1, d), lambda b, i: (b, 0, j))


VT_ROWS = MLA_V + 16


def _mla_proj_kernel(h_ref, sh_ref, sc_ref, win_ref, qn_ref, kvn_ref, wqt_ref, wk_ref, wvt_ref,
                     qmt_ref, km_ref, qt_ref, k_ref, vt_ref, *, heads, q_rank, kv_rank):
    x = h_ref[0]
    tm = x.shape[0]
    u = x * (1.0 + sc_ref[0]) + sh_ref[0]
    lat = _dot(u.astype(BF16), win_ref[...])
    cq = _rms_norm(lat[:, :q_rank], qn_ref[...]).astype(BF16)
    ckv = _rms_norm(lat[:, q_rank:q_rank + kv_rank], kvn_ref[...]).astype(BF16)
    kr2 = lat[:, q_rank + kv_rank:]
    qt_all = _dot_nt(wqt_ref[...], cq)
    k_all = _dot(ckv, wk_ref[...])
    vt_all = _dot_nt(wvt_ref[...], ckv)
    qmt = qmt_ref[...]
    t = kr2 * km_ref[...]
    krot2 = (t + pltpu.roll(t, MLA_ROPE, axis=1)).astype(BF16)
    ones_rows = jnp.where(lax.broadcasted_iota(jnp.int32, (VT_ROWS - MLA_V, tm), 0) == 0, 1.0, 0.0).astype(BF16)
    for hh in range(heads):
        qt_ref[0, hh] = (qt_all[hh * MLA_QK:(hh + 1) * MLA_QK, :] * qmt).astype(BF16)
        k_ref[0, hh, :, :MLA_NOPE] = k_all[:, hh * MLA_NOPE:(hh + 1) * MLA_NOPE].astype(BF16)
        k_ref[0, hh, :, MLA_NOPE:] = krot2
        vt_ref[0, hh, 0, :MLA_V, :] = vt_all[hh * MLA_V:(hh + 1) * MLA_V, :].astype(BF16)
        vt_ref[0, hh, 0, MLA_V:, :] = ones_rows


def _mla_proj_call(h, mod, w_in, qn, kvn, wqt, wk, wvt, qmt, km, *, tm, heads):
    b, t, d = h.shape
    q_rank, kv_rank = qn.shape[1], kvn.shape[1]
    assert t % tm == 0
    kern = functools.partial(_mla_proj_kernel, heads=heads, q_rank=q_rank, kv_rank=kv_rank)
    return pl.pallas_call(
        kern,
        grid=(b, t // tm),
        in_specs=[
            pl.BlockSpec((1, tm, d), lambda bb, i: (bb, i, 0)),
            _mod_spec(d, 0), _mod_spec(d, 1),
            _const_spec(w_in.shape), _const_spec(qn.shape), _const_spec(kvn.shape),
            _const_spec(wqt.shape), _const_spec(wk.shape), _const_spec(wvt.shape),
            pl.BlockSpec((MLA_QK, tm), lambda bb, i: (0, i)),
            pl.BlockSpec((tm, 2 * MLA_ROPE), lambda bb, i: (i, 0)),
        ],
        out_specs=[
            pl.BlockSpec((1, heads, MLA_QK, tm), lambda bb, i: (bb, 0, 0, i)),
            pl.BlockSpec((1, heads, tm, MLA_QK), lambda bb, i: (bb, 0, i, 0)),
            pl.BlockSpec((1, heads, 1, VT_ROWS, tm), lambda bb, i: (bb, 0, i, 0, 0)),
        ],
        out_shape=[
            jax.ShapeDtypeStruct((b, heads, MLA_QK, t), BF16),
            jax.ShapeDtypeStruct((b, heads, t, MLA_QK), BF16),
            jax.ShapeDtypeStruct((b, heads, t // tm, VT_ROWS, tm), BF16),
        ],
        compiler_params=_cparams(("parallel", "parallel")),
        name="mla_proj",
    )(h, mod, mod, w_in, qn, kvn, wqt, wk, wvt, qmt, km)


QGROUP = 256


def _attn_kernel(*refs, slabs_per_step, n_lat, n_groups):
    if n_lat:
        qt_ref, kc_ref, vtc_ref, kl_ref, vtl_ref, o_ref, m_ref, acc_ref, st_ref = refs
    else:
        qt_ref, kc_ref, vtc_ref, o_ref, m_ref, acc_ref = refs

    def scores(kblk, g):
        return _dot(kblk, qt_ref[0, 0, :, g * QGROUP:(g + 1) * QGROUP])

    def softmax_pv(st, vt_slabs, g, first):
        slab = vt_slabs[0].shape[1]
        cols = slice(g * QGROUP, (g + 1) * QGROUP)
        cmax = jnp.max(st, axis=0, keepdims=True)
        if first:
            m_new = cmax
        else:
            m_old = m_ref[:, cols]
            m_new = jnp.maximum(m_old, cmax)
        pt = jnp.exp2(st - m_new).astype(BF16)
        pv = None
        for s, vt in enumerate(vt_slabs):
            term = _dot(vt, pt[s * slab:(s + 1) * slab, :])
            pv = term if pv is None else pv + term
        if first:
            acc_ref[:, cols] = pv
        else:
            acc_ref[:, cols] = acc_ref[:, cols] * jnp.exp2(m_old - m_new) + pv
        m_ref[:, cols] = m_new

    kc = kc_ref[0, 0]
    vtc = [vtc_ref[0, 0, s] for s in range(vtc_ref.shape[2])]
    sts = [scores(kc, g) for g in range(n_groups)]
    for g in range(n_groups):
        softmax_pv(sts[g], vtc, g, True)
    if n_lat:
        slab = vtl_ref.shape[4]
        tk = slabs_per_step * slab
        k0 = kl_ref[0, 0, pl.ds(0, tk), :]
        for g in range(n_groups):
            st_ref[g] = scores(k0, g)

        def body(j, carry):
            off = pl.multiple_of(jnp.minimum(j + 1, n_lat - 1) * tk, tk)
            k_next = kl_ref[0, 0, pl.ds(off, tk), :]
            vts = [vtl_ref[0, 0, j * slabs_per_step + s] for s in range(slabs_per_step)]
            nxt = scores(k_next, 0)
            for g in range(n_groups):
                cur, nxt = nxt, (scores(k_next, g + 1) if g + 1 < n_groups else None)
                softmax_pv(st_ref[g], vts, g, False)
                st_ref[g] = cur
            return carry
        lax.fori_loop(0, n_lat, body, 0)
    acc = acc_ref[...]
    ot = acc[:MLA_V] * (1.0 / acc[MLA_V:MLA_V + 1])
    o_ref[0] = ot.T.astype(o_ref.dtype)


def _attn_call(qt, k_ctx, vt_ctx, k_lat=None, vt_lat=None, *, tq, slabs_per_step=4):
    b, h, dqk, t = qt.shape
    tc = k_ctx.shape[2]
    assert t % tq == 0 and tq % QGROUP == 0
    n_lat = 0
    in_specs = [
        pl.BlockSpec((1, 1, dqk, tq), lambda bb, hh, i: (bb, hh, 0, i)),
        pl.BlockSpec((1, 1, tc, dqk), lambda bb, hh, i: (bb, hh, 0, 0)),
        pl.BlockSpec((1, 1) + vt_ctx.shape[2:], lambda bb, hh, i: (bb, hh, 0, 0, 0)),
    ]
    args = [qt, k_ctx, vt_ctx]
    if k_lat is not None:
        tl = k_lat.shape[2]
        ns = vt_lat.shape[2]
        assert ns % slabs_per_step == 0
        n_lat = ns // slabs_per_step
        in_specs += [
            pl.BlockSpec((1, 1, tl, dqk), lambda bb, hh, i: (bb, hh, 0, 0)),
            pl.BlockSpec((1, 1) + vt_lat.shape[2:], lambda bb, hh, i: (bb, hh, 0, 0, 0)),
        ]
        args += [k_lat, vt_lat]
    n_groups = tq // QGROUP
    kern = functools.partial(_attn_kernel, slabs_per_step=slabs_per_step, n_lat=n_lat, n_groups=n_groups)
    scratch = [pltpu.VMEM((1, tq), F32), pltpu.VMEM((VT_ROWS, tq), F32)]
    if n_lat:
        scratch.append(pltpu.VMEM((n_groups, slabs_per_step * vt_lat.shape[4], QGROUP), F32))
    return pl.pallas_call(
        kern,
        grid=(b, h, t // tq),
        in_specs=in_specs,
        out_specs=pl.BlockSpec((1, tq, MLA_V), lambda bb, hh, i: (bb, i, hh)),
        out_shape=jax.ShapeDtypeStruct((b, t, h * MLA_V), BF16),
        scratch_shapes=scratch,
        compiler_params=_cparams(("parallel", "parallel", "arbitrary")),
        name="attn_lat" if n_lat else "attn_ctx",
    )(*args)


def _post_tail(y_in, h_ref, g1_ref, sh2_ref, sc2_ref, g2_ref, wout_ref, l1g_ref, l1b_ref,
               wfi_ref, wfo_ref, l2g_ref, l2b_ref, o_ref, *, alpha, d_ff):
    y = _dot(y_in, wout_ref[...])
    h1 = _layer_norm(alpha * h_ref[0] + g1_ref[0] * y, l1g_ref[...], l1b_ref[...])
    u2 = h1 * (1.0 + sc2_ref[0]) + sh2_ref[0]
    gu = _dot(u2.astype(BF16), wfi_ref[...])
    act = _silu(gu[:, :d_ff]) * gu[:, d_ff:]
    f = _dot(act.astype(BF16), wfo_ref[...])
    o_ref[0] = _layer_norm(alpha * h1 + g2_ref[0] * f, l2g_ref[...], l2b_ref[...])


def _post_mla_kernel(a_ref, *rest, alpha, d_ff):
    _post_tail(a_ref[0], *rest, alpha=alpha, d_ff=d_ff)


def _post_gdn_kernel(of_ref, ob_ref, z_ref, nw_ref, *rest, alpha, d_ff, k_heads):
    nw = nw_ref[...]
    parts = []
    for kh in range(k_heads):
        o2 = of_ref[0, kh] + ob_ref[0, kh]
        for v in range(2):
            o = o2[:, v * GDN_DV:(v + 1) * GDN_DV]
            vh = 2 * kh + v
            z = z_ref[0, :, vh * GDN_DV:(vh + 1) * GDN_DV]
            parts.append((_rms_norm(o, nw) * _silu(z)).astype(BF16))
    _post_tail(jnp.concatenate(parts, axis=1), *rest, alpha=alpha, d_ff=d_ff)


def _post_call(mixer_in, h, mod, w_out, l1g, l1b, wfi, wfo, l2g, l2b, *, tm, alpha, gdn_norm=None):
    b, t, d = h.shape
    d_ff = wfo.shape[0]
    assert t % tm == 0
    row = lambda w: pl.BlockSpec((1, tm, w), lambda bb, i: (bb, i, 0))
    if gdn_norm is None:
        (a,) = mixer_in
        kern = functools.partial(_post_mla_kernel, alpha=alpha, d_ff=d_ff)
        head_specs = [row(a.shape[2])]
        head_args = [a]
    else:
        o_f, o_b, z = mixer_in
        kh = o_f.shape[1]
        kern = functools.partial(_post_gdn_kernel, alpha=alpha, d_ff=d_ff, k_heads=kh)
        hm = pl.BlockSpec((1, kh, tm, 2 * GDN_DV), lambda bb, i: (bb, 0, i, 0))
        head_specs = [hm, hm, row(z.shape[2]), _const_spec(gdn_norm.shape)]
        head_args = [o_f, o_b, z, gdn_norm]
    return pl.pallas_call(
        kern,
        grid=(b, t // tm),
        in_specs=head_specs + [
            row(d),
            _mod_spec(d, 2), _mod_spec(d, 3), _mod_spec(d, 4), _mod_spec(d, 5),
            _const_spec(w_out.shape), _const_spec(l1g.shape), _const_spec(l1b.shape),
            _const_spec(wfi.shape), _const_spec(wfo.shape),
            _const_spec(l2g.shape), _const_spec(l2b.shape),
        ],
        out_specs=row(d),
        out_shape=jax.ShapeDtypeStruct((b, t, d), F32),
        compiler_params=_cparams(("parallel", "parallel")),
        name="post_mla" if gdn_norm is None else "post_gdn",
    )(*head_args, h, mod, mod, mod, mod, w_out, l1g, l1b, wfi, wfo, l2g, l2b)


def _gdn_proj_kernel(h_ref, sh_ref, sc_ref, w_ref, qkv_ref, z_ref, ba_ref, *, qkv_w, z_w):
    u = h_ref[0] * (1.0 + sc_ref[0]) + sh_ref[0]
    p = _dot(u.astype(BF16), w_ref[...])
    qkv_ref[0] = p[:, :qkv_w]
    z_ref[0] = p[:, qkv_w:qkv_w + z_w]
    ba_ref[0] = p[:, qkv_w + z_w:]


def _gdn_proj_call(h, mod, w, *, tm, qkv_w, z_w):
    b, t, d = h.shape
    ba_w = w.shape[1] - qkv_w - z_w
    assert t % tm == 0
    row = lambda wd: pl.BlockSpec((1, tm, wd), lambda bb, i: (bb, i, 0))
    kern = functools.partial(_gdn_proj_kernel, qkv_w=qkv_w, z_w=z_w)
    return pl.pallas_call(
        kern,
        grid=(b, t // tm),
        in_specs=[row(d), _mod_spec(d, 0), _mod_spec(d, 1), _const_spec(w.shape)],
        out_specs=[row(qkv_w), row(z_w), row(ba_w)],
        out_shape=[
            jax.ShapeDtypeStruct((b, t, qkv_w), F32),
            jax.ShapeDtypeStruct((b, t, z_w), F32),
            jax.ShapeDtypeStruct((b, t, ba_w), F32),
        ],
        compiler_params=_cparams(("parallel", "parallel")),
        name="gdn_proj",
    )(h, mod, mod, w)


def _chunk_cumsum(g, suffix):
    n = g.shape[0]
    pos = lax.broadcasted_iota(jnp.int32, g.shape, 0) % CHUNK
    s = 1
    while s < CHUNK:
        if suffix:
            shifted = pltpu.roll(g, n - s, axis=0)
            ok = pos < CHUNK - s
        else:
            shifted = pltpu.roll(g, s, axis=0)
            ok = pos >= s
        g = g + jnp.where(ok, shifted, 0.0)
        s *= 2
    return g


CONV_ROWS = 64
CONV_COLS = 512


def _gdn_conv_kernel(prev_ref, cur_ref, next_ref, ba_ref, cw_ref, al_ref, dtb_ref,
                     q_ref, k_ref, kt_ref, v_ref, gt_ref, gr_ref, xe_ref, y_ref, *, k_heads, tm):
    i = pl.program_id(1)
    n_i = pl.num_programs(1)
    kw = k_heads * GDN_DK
    c_all = xe_ref.shape[1]
    n_chunks = tm // CHUNK
    xe_ref[0:CONV_HALO] = jnp.where(i > 0, prev_ref[0], 0.0)
    xe_ref[CONV_HALO:CONV_HALO + tm] = cur_ref[0]
    xe_ref[CONV_HALO + tm:] = jnp.where(i < n_i - 1, next_ref[0], 0.0)
    half = GDN_CONV // 2
    for c0 in range(0, c_all, CONV_COLS):
        w = cw_ref[:, c0:c0 + CONV_COLS]
        for r0 in range(0, tm, CONV_ROWS):
            acc = None
            for j in range(GDN_CONV):
                start = CONV_HALO - half + j + r0
                term = xe_ref[start:start + CONV_ROWS, c0:c0 + CONV_COLS] * w[j:j + 1, :]
                acc = term if acc is None else acc + term
            y_ref[r0:r0 + CONV_ROWS, c0:c0 + CONV_COLS] = _silu(acc)
    for kh in range(k_heads):
        qh = y_ref[:, kh * GDN_DK:(kh + 1) * GDN_DK]
        q_ref[0, kh] = (qh * lax.rsqrt(jnp.sum(qh * qh, axis=-1, keepdims=True) + RMS_EPS)
                        * (GDN_DK ** -0.5)).astype(BF16)
        kk = y_ref[:, kw + kh * GDN_DK:kw + (kh + 1) * GDN_DK]
        kn = kk * lax.rsqrt(jnp.sum(kk * kk, axis=-1, keepdims=True) + RMS_EPS)
        k_ref[0, kh] = kn.astype(BF16)
        knt = kn.T.astype(BF16)
        for n in range(n_chunks):
            kt_ref[0, kh, n] = knt[:, n * CHUNK:(n + 1) * CHUNK]
        v_ref[0, kh] = y_ref[:, 2 * kw + kh * 2 * GDN_DV:2 * kw + (kh + 1) * 2 * GDN_DV].astype(BF16)
    ba = ba_ref[0]
    lane = lax.broadcasted_iota(jnp.int32, ba.shape, 1) % LANES
    beta = jax.nn.sigmoid(ba)
    g = -jnp.exp(al_ref[...]) * jax.nn.softplus(ba + dtb_ref[...])
    g = jnp.where((lane >= 4) & (lane < 8), g, 0.0)
    gcum = jnp.where(lane < 6, _chunk_cumsum(g, False), _chunk_cumsum(g, True))
    gates = jnp.where(lane < 4, beta, gcum)
    for kh in range(k_heads):
        gk = gates[:, kh * LANES:(kh + 1) * LANES]
        gt_ref[0, kh] = gk
        rows = gk.T[0:8, :]
        for n in range(n_chunks):
            piece = rows[:, n * CHUNK:(n + 1) * CHUNK]
            gr_ref[0, kh, n] = jnp.concatenate([piece, piece], axis=1)


def _gdn_conv_call(qkv, ba, conv_w, a_log_ext, dtb_ext, *, tm, k_heads):
    b, t, c = qkv.shape
    assert t % tm == 0 and tm % CHUNK == 0 and tm % CONV_HALO == 0
    assert tm % CONV_ROWS == 0 and c % CONV_COLS == 0
    r = tm // CONV_HALO
    nblk8 = t // CONV_HALO
    ncb = tm // CHUNK
    kern = functools.partial(_gdn_conv_kernel, k_heads=k_heads, tm=tm)
    hm = lambda w: pl.BlockSpec((1, k_heads, tm, w), lambda bb, i: (bb, 0, i, 0))
    per_chunk = lambda rr, w: pl.BlockSpec((1, k_heads, ncb, rr, w), lambda bb, i: (bb, 0, i, 0, 0))
    return pl.pallas_call(
        kern,
        grid=(b, t // tm),
        in_specs=[
            pl.BlockSpec((1, CONV_HALO, c), lambda bb, i: (bb, jnp.maximum(i * r - 1, 0), 0)),
            pl.BlockSpec((1, tm, c), lambda bb, i: (bb, i, 0)),
            pl.BlockSpec((1, CONV_HALO, c), lambda bb, i: (bb, jnp.minimum((i + 1) * r, nblk8 - 1), 0)),
            pl.BlockSpec((1, tm, ba.shape[2]), lambda bb, i: (bb, i, 0)),
            _const_spec(conv_w.shape), _const_spec(a_log_ext.shape), _const_spec(dtb_ext.shape),
        ],
        out_specs=[hm(GDN_DK), hm(GDN_DK), per_chunk(GDN_DK, CHUNK), hm(2 * GDN_DV), hm(LANES),
                   per_chunk(8, LANES)],
        out_shape=[
            jax.ShapeDtypeStruct((b, k_heads, t, GDN_DK), BF16),
            jax.ShapeDtypeStruct((b, k_heads, t, GDN_DK), BF16),
            jax.ShapeDtypeStruct((b, k_heads, t // CHUNK, GDN_DK, CHUNK), BF16),
            jax.ShapeDtypeStruct((b, k_heads, t, 2 * GDN_DV), BF16),
            jax.ShapeDtypeStruct((b, k_heads, t, LANES), F32),
            jax.ShapeDtypeStruct((b, k_heads, t // CHUNK, 8, LANES), F32),
        ],
        scratch_shapes=[pltpu.VMEM((tm + 2 * CONV_HALO, c), F32), pltpu.VMEM((tm, c), F32)],
        compiler_params=_cparams(("parallel", "parallel")),
        name="gdn_conv",
    )(qkv, qkv, qkv, ba, conv_w, a_log_ext, dtb_ext)


def _gdn_in_kernel(hp_ref, hc_ref, hn_ref, sh_ref, sc_ref, w_ref, cw_ref, al_ref, dtb_ref,
                   q_ref, k_ref, kt_ref, v_ref, gt_ref, gr_ref, z_ref, xe_ref, y_ref, *, k_heads, tm, z_w):
    i = pl.program_id(1)
    n_i = pl.num_programs(1)
    kw = k_heads * GDN_DK
    c_all = xe_ref.shape[1]
    n_chunks = tm // CHUNK
    half = GDN_CONV // 2
    h_ext = jnp.concatenate([hp_ref[0], hc_ref[0], hn_ref[0]], axis=0)
    u = (h_ext * (1.0 + sc_ref[0]) + sh_ref[0]).astype(BF16)
    keep_prev = jnp.where(i > 0, 1.0, 0.0)
    keep_next = jnp.where(i < n_i - 1, 1.0, 0.0)

    def project(c0):
        return _dot(u, w_ref[:, c0:c0 + CONV_COLS])

    def conv_tile(p, c0):
        cols = slice(c0, c0 + CONV_COLS)
        xe_ref[0:CONV_HALO, cols] = p[0:CONV_HALO] * keep_prev
        xe_ref[CONV_HALO:CONV_HALO + tm, cols] = p[CONV_HALO:CONV_HALO + tm]
        xe_ref[CONV_HALO + tm:, cols] = p[CONV_HALO + tm:] * keep_next
        w = cw_ref[:, cols]
        for r0 in range(0, tm, CONV_ROWS):
            acc = None
            for j in range(GDN_CONV):
                start = CONV_HALO - half + j + r0
                term = xe_ref[start:start + CONV_ROWS, cols] * w[j:j + 1, :]
                acc = term if acc is None else acc + term
            y_ref[r0:r0 + CONV_ROWS, cols] = _silu(acc)

    tiles = list(range(0, c_all, CONV_COLS))
    nxt = project(tiles[0])
    for t_idx, c0 in enumerate(tiles):
        cur = nxt
        if t_idx + 1 < len(tiles):
            nxt = project(tiles[t_idx + 1])
        else:
            nxt = _dot(u[CONV_HALO:CONV_HALO + tm], w_ref[:, c_all:])
        conv_tile(cur, c0)
    z_ref[0] = nxt[:, :z_w]
    ba = nxt[:, z_w:]
    for kh in range(k_heads):
        qh = y_ref[:, kh * GDN_DK:(kh + 1) * GDN_DK]
        q_ref[0, kh] = (qh * lax.rsqrt(jnp.sum(qh * qh, axis=-1, keepdims=True) + RMS_EPS)
                        * (GDN_DK ** -0.5)).astype(BF16)
        kk = y_ref[:, kw + kh * GDN_DK:kw + (kh + 1) * GDN_DK]
        kn = kk * lax.rsqrt(jnp.sum(kk * kk, axis=-1, keepdims=True) + RMS_EPS)
        k_ref[0, kh] = kn.astype(BF16)
        knt = kn.T.astype(BF16)
        for n in range(n_chunks):
            kt_ref[0, kh, n] = knt[:, n * CHUNK:(n + 1) * CHUNK]
        v_ref[0, kh] = y_ref[:, 2 * kw + kh * 2 * GDN_DV:2 * kw + (kh + 1) * 2 * GDN_DV].astype(BF16)
    lane = lax.broadcasted_iota(jnp.int32, ba.shape, 1) % LANES
    beta = jax.nn.sigmoid(ba)
    g = -jnp.exp(al_ref[...]) * jax.nn.softplus(ba + dtb_ref[...])
    g = jnp.where((lane >= 4) & (lane < 8), g, 0.0)
    gcum = jnp.where(lane < 6, _chunk_cumsum(g, False), _chunk_cumsum(g, True))
    gates = jnp.where(lane < 4, beta, gcum)
    for kh in range(k_heads):
        gk = gates[:, kh * LANES:(kh + 1) * LANES]
        gt_ref[0, kh] = gk
        rows = gk.T[0:8, :]
        for n in range(n_chunks):
            piece = rows[:, n * CHUNK:(n + 1) * CHUNK]
            gr_ref[0, kh, n] = jnp.concatenate([piece, piece], axis=1)


def _gdn_in_call(h, mod, w, conv_w, a_log_ext, dtb_ext, *, tm, k_heads, z_w):
    b, t, d = h.shape
    c = conv_w.shape[1]
    assert t % tm == 0 and tm % CHUNK == 0 and tm % CONV_HALO == 0
    assert tm % CONV_ROWS == 0 and c % CONV_COLS == 0
    r = tm // CONV_HALO
    nblk8 = t // CONV_HALO
    ncb = tm // CHUNK
    kern = functools.partial(_gdn_in_kernel, k_heads=k_heads, tm=tm, z_w=z_w)
    hm = lambda wd: pl.BlockSpec((1, k_heads, tm, wd), lambda bb, i: (bb, 0, i, 0))
    per_chunk = lambda rr, wd: pl.BlockSpec((1, k_heads, ncb, rr, wd), lambda bb, i: (bb, 0, i, 0, 0))
    return pl.pallas_call(
        kern,
        grid=(b, t // tm),
        in_specs=[
            pl.BlockSpec((1, CONV_HALO, d), lambda bb, i: (bb, jnp.maximum(i * r - 1, 0), 0)),
            pl.BlockSpec((1, tm, d), lambda bb, i: (bb, i, 0)),
            pl.BlockSpec((1, CONV_HALO, d), lambda bb, i: (bb, jnp.minimum((i + 1) * r, nblk8 - 1), 0)),
            _mod_spec(d, 0), _mod_spec(d, 1),
            _const_spec(w.shape), _const_spec(conv_w.shape),
            _const_spec(a_log_ext.shape), _const_spec(dtb_ext.shape),
        ],
        out_specs=[hm(GDN_DK), hm(GDN_DK), per_chunk(GDN_DK, CHUNK), hm(2 * GDN_DV), hm(LANES),
                   per_chunk(8, LANES), pl.BlockSpec((1, tm, z_w), lambda bb, i: (bb, i, 0))],
        out_shape=[
            jax.ShapeDtypeStruct((b, k_heads, t, GDN_DK), BF16),
            jax.ShapeDtypeStruct((b, k_heads, t, GDN_DK), BF16),
            jax.ShapeDtypeStruct((b, k_heads, t // CHUNK, GDN_DK, CHUNK), BF16),
            jax.ShapeDtypeStruct((b, k_heads, t, 2 * GDN_DV), BF16),
            jax.ShapeDtypeStruct((b, k_heads, t, LANES), F32),
            jax.ShapeDtypeStruct((b, k_heads, t // CHUNK, 8, LANES), F32),
            jax.ShapeDtypeStruct((b, t, z_w), F32),
        ],
        scratch_shapes=[pltpu.VMEM((tm + 2 * CONV_HALO, c), F32), pltpu.VMEM((tm, c), F32)],
        compiler_params=_cparams(("parallel", "parallel")),
        name="gdn_in",
    )(h, h, h, mod, mod, w, conv_w, a_log_ext, dtb_ext)


WAVE_LAG = 8


def _dot_split(a_hi, a_lo, b_hi, b_lo):
    return _dot(jnp.concatenate([a_hi, a_lo], axis=1), jnp.concatenate([b_hi, b_lo, b_hi, b_lo], axis=0))


def _gdn_scan_kernel(qf_ref, kf_ref, ktf_ref, vf_ref, gf_ref, grf_ref,
                     qb_ref, kb_ref, ktb_ref, vb_ref, gb_ref, grb_ref, s0_ref,
                     of_ref, ob_ref, sfin_ref, s_ref, *, hg, n_waves):
    n = pl.program_id(2)

    @pl.when(n == 0)
    def _():
        s_ref[...] = s0_ref[0]

    c = CHUNK
    row = lax.broadcasted_iota(jnp.int32, (c, LANES), 0)
    col = lax.broadcasted_iota(jnp.int32, (c, LANES), 1) % c
    dirs = ((qf_ref, kf_ref, ktf_ref, vf_ref, gf_ref, grf_ref), (qb_ref, kb_ref, ktb_ref, vb_ref, gb_ref, grb_ref))

    eye = jnp.where(row == col, 1.0, 0.0)
    zero_rows = jnp.zeros((c, 2 * GDN_DV), BF16)
    pairs = []
    blk_r, blk_c = row, col
    b = 1
    while b < c:
        pair = (blk_r // 2 == blk_c // 2) & (blk_r != blk_c)
        pairs.append(pair if b == 1 else jnp.where(pair, 1.0, 0.0).astype(BF16))
        blk_r, blk_c = blk_r // 2, blk_c // 2
        b *= 2

    def wave(heads):
        chains = []
        for j in heads:
            for d in range(2):
                q_ref, k_ref, kt_ref, v_ref, g_ref, gr_ref = dirs[d]
                qc, kc, v2, gt = q_ref[0, j], k_ref[0, j], v_ref[0, j], g_ref[0, j]
                gtt = gr_ref[0, j, 0]
                ktc = kt_ref[0, j, 0]
                kq = _dot_nt(jnp.concatenate([kc, qc], axis=0), jnp.concatenate([kc, kc], axis=0))
                for v in range(2):
                    chains.append(dict(j=j, d=d, v=v, qc=qc, kc=kc, vv=v2[:, v * GDN_DV:(v + 1) * GDN_DV],
                                       gt=gt, gtt=gtt, ktc=ktc, kk=kq[:c], qk=kq[c:]))
            yield
        for i, ch in enumerate(chains):
            slot = 2 * ch["d"] + ch["v"]
            if ch["d"] == 1:
                incl, strict = row <= col, row < col
            else:
                incl, strict = row >= col, row > col
            beta_col = jnp.broadcast_to(ch["gt"][:, slot:slot + 1], (c, LANES))
            g_col = jnp.broadcast_to(ch["gt"][:, 4 + slot:5 + slot], (c, LANES))
            g_row = ch["gtt"][4 + slot:5 + slot, :]
            decay = jnp.where(incl, jnp.exp(jnp.where(incl, g_col - g_row, 0.0)), 0.0)
            eg_col = jnp.exp(g_col)
            ch["g_row"] = g_row
            lmat = jnp.where(strict, ch["kk"] * beta_col * decay, 0.0)
            ch.update(
                lm=lmat.astype(BF16), tinv=eye - jnp.where(pairs[0], lmat, 0.0),
                attn=(ch["qk"] * decay)[:, :c].astype(BF16),
                eg_col=eg_col, g_col=g_col,
                rhs=jnp.concatenate([ch["vv"].astype(F32) * beta_col,
                                     ch["kc"].astype(F32) * (beta_col * eg_col)], axis=1).astype(BF16))
            if i % 4 == 3:
                yield
        for pair_bf in pairs[1:]:
            for ch in chains:
                ch["t_hi"], ch["t_lo"] = _split_bf16(ch["tinv"])
                ch["w"] = _dot(ch["lm"] * pair_bf, jnp.concatenate([ch["t_hi"], ch["t_lo"]], axis=0))
            yield
            for ch in chains:
                w_hi, w_lo = _split_bf16(ch["w"])
                ch["tinv"] = ch["tinv"] - _dot_split(ch["t_hi"], ch["t_lo"], w_hi, w_lo)
            yield
        for ch in chains:
            t_hi, t_lo = _split_bf16(ch["tinv"])
            r = ch["rhs"]
            ch["x"] = _dot(jnp.concatenate([t_hi, t_lo], axis=1),
                           jnp.concatenate([r, zero_rows, r, zero_rows], axis=0))
        yield
        for ch in chains:
            ch["s_idx"] = ch["j"] * 4 + 2 * ch["d"] + ch["v"]
            ch["s_old"] = s_ref[ch["s_idx"]]
            ch["wq"] = _dot(jnp.concatenate([ch["x"][:, GDN_DV:].astype(BF16), ch["qc"]], axis=0),
                            ch["s_old"].astype(BF16))
        yield
        for ch in chains:
            g_col = ch["g_col"]
            ch["g_last"] = g_col[0:1, :] if ch["d"] == 1 else g_col[c - 1:c, :]
            v_new = (ch["x"][:, :GDN_DV] - ch["wq"][:c]).astype(BF16)
            dec_row = jnp.exp(ch["g_last"] - ch["g_row"])[:, :c]
            k_dec_t = (ch["ktc"].astype(F32) * dec_row).astype(BF16)
            ch["av"] = _dot(ch["attn"], v_new)
            ch["kv"] = _dot(k_dec_t, v_new)
        yield
        outs = {}
        for ch in chains:
            s_ref[ch["s_idx"]] = ch["s_old"] * jnp.exp(ch["g_last"]) + ch["kv"]
            outs[ch["j"], ch["d"], ch["v"]] = ch["eg_col"] * ch["wq"][c:] + ch["av"]
        for j in heads:
            for d, o_ref in enumerate((of_ref, ob_ref)):
                o_ref[0, j] = jnp.concatenate([outs[j, d, 0], outs[j, d, 1]], axis=1)

    waves = [wave(list(range(w, hg, n_waves))) for w in range(n_waves)]
    live = []
    pending = list(waves)
    while live or pending:
        if pending:
            live.append(pending.pop(0))
        for _ in range(WAVE_LAG if pending else 1):
            for g in list(live):
                if next(g, "done") == "done":
                    live.remove(g)

    @pl.when(n == pl.num_programs(2) - 1)
    def _():
        sfin_ref[0] = s_ref[...]


def _gdn_scan_call(q, k, kt, v, gates, gate_rows, s0, *, hg):
    b, kh, t, _ = q.shape
    assert kh % hg == 0 and t % CHUNK == 0
    nc = t // CHUNK
    ng = kh // hg
    fwd = lambda w: pl.BlockSpec((1, hg, CHUNK, w), lambda bb, g, n: (bb, g, n, 0))
    bwd = lambda w: pl.BlockSpec((1, hg, CHUNK, w), lambda bb, g, n: (bb, g, nc - 1 - n, 0))
    fwd_c = lambda r, w: pl.BlockSpec((1, hg, 1, r, w), lambda bb, g, n: (bb, g, n, 0, 0))
    bwd_c = lambda r, w: pl.BlockSpec((1, hg, 1, r, w), lambda bb, g, n: (bb, g, nc - 1 - n, 0, 0))
    st = pl.BlockSpec((1, hg * 4, GDN_DK, GDN_DV), lambda bb, g, n: (bb, g, 0, 0))
    kern = functools.partial(_gdn_scan_kernel, hg=hg, n_waves=1)
    return pl.pallas_call(
        kern,
        grid=(b, ng, nc),
        in_specs=[fwd(GDN_DK), fwd(GDN_DK), fwd_c(GDN_DK, CHUNK), fwd(2 * GDN_DV), fwd(LANES), fwd_c(8, LANES),
                  bwd(GDN_DK), bwd(GDN_DK), bwd_c(GDN_DK, CHUNK), bwd(2 * GDN_DV), bwd(LANES), bwd_c(8, LANES),
                  st],
        out_specs=[fwd(2 * GDN_DV), bwd(2 * GDN_DV), st],
        out_shape=[
            jax.ShapeDtypeStruct((b, kh, t, 2 * GDN_DV), F32),
            jax.ShapeDtypeStruct((b, kh, t, 2 * GDN_DV), F32),
            jax.ShapeDtypeStruct(s0.shape, F32),
        ],
        scratch_shapes=[pltpu.VMEM((hg * 4, GDN_DK, GDN_DV), F32)],
        compiler_params=_cparams(("parallel", "parallel", "arbitrary")),
        name="gdn_scan",
    )(q, k, kt, v, gates, gate_rows, q, k, kt, v, gates, gate_rows, s0)


def _rope_tables(t, q_scale):
    pos = jnp.arange(t)
    row = (pos // GRID_W).astype(F32)
    col = (pos % GRID_W).astype(F32)
    n_freq = MLA_ROPE // 4
    inv_freq = ROPE_THETA ** (-(2.0 * jnp.arange(n_freq, dtype=F32)) / (MLA_ROPE // 2))
    ar = row[:, None] * inv_freq
    ac = col[:, None] * inv_freq
    cs = jnp.concatenate([jnp.cos(ar), jnp.cos(ar), jnp.cos(ac), jnp.cos(ac)], axis=1)
    sn = jnp.concatenate([-jnp.sin(ar), jnp.sin(ar), -jnp.sin(ac), jnp.sin(ac)], axis=1)
    km = jnp.concatenate([cs, sn], axis=1)
    qm = q_scale * jnp.concatenate([jnp.ones((t, MLA_NOPE), F32), cs, sn], axis=1)
    return qm, km


def _no_rope_tables(t, q_scale):
    one = jnp.ones((t, MLA_ROPE), F32)
    zero = jnp.zeros((t, MLA_ROPE), F32)
    km = jnp.concatenate([one, zero], axis=1)
    qm = q_scale * jnp.concatenate([jnp.ones((t, MLA_NOPE), F32), one, zero], axis=1)
    return qm, km


def _swap_perm():
    f = MLA_ROPE // 4
    return jnp.array(list(range(f, 2 * f)) + list(range(0, f)) + list(range(3 * f, 4 * f)) + list(range(2 * f, 3 * f)))


def _mla_layer(h_lat, h_ctx, mod_lat, mod_ctx, ln, w_in, q_norm, kv_norm, w_q_up, w_kv_up, w_out, *, alpha, last):
    bsz, seq, d = h_lat.shape
    tctx = h_ctx.shape[1]
    heads = d // 128
    q_rank, kv_rank = q_norm.shape[0], kv_norm.shape[0]
    tm_lat, tm_ctx = min(256, seq), min(256, tctx)
    q_scale = (MLA_NOPE + MLA_ROPE) ** -0.5 * math.log2(math.e)
    perm = _swap_perm()
    rope_cols = w_in[:, q_rank + kv_rank:]
    w_in_ext = jnp.concatenate([w_in, rope_cols[:, perm]], axis=1).astype(BF16)
    wq = w_q_up.reshape(q_rank, heads, MLA_NOPE + MLA_ROPE)
    wq_ext = jnp.concatenate([wq, wq[:, :, MLA_NOPE:][:, :, perm]], axis=2)
    wqt = wq_ext.reshape(q_rank, heads * MLA_QK).T.astype(BF16)
    wkv = w_kv_up.reshape(kv_rank, heads, MLA_NOPE + MLA_V)
    wk = wkv[:, :, :MLA_NOPE].reshape(kv_rank, -1).astype(BF16)
    wvt = wkv[:, :, MLA_NOPE:].reshape(kv_rank, -1).T.astype(BF16)
    qn, kvn = q_norm.reshape(1, -1), kv_norm.reshape(1, -1)
    qm_l, km_l = _rope_tables(seq, q_scale)
    qm_c, km_c = _no_rope_tables(tctx, q_scale)
    q_l, k_l, v_l = _mla_proj_call(h_lat, mod_lat, w_in_ext, qn, kvn, wqt, wk, wvt, qm_l.T, km_l,
                                   tm=tm_lat, heads=heads)
    q_c, k_c, v_c = _mla_proj_call(h_ctx, mod_ctx, w_in_ext, qn, kvn, wqt, wk, wvt, qm_c.T, km_c,
                                   tm=tm_ctx, heads=heads)
    o_l = _attn_call(q_l, k_c, v_c, k_l, v_l, tq=min(2048, seq))
    w_out = w_out.astype(BF16)
    h_lat_new = _post_call((o_l,), h_lat, mod_lat, w_out, *ln, tm=tm_lat, alpha=alpha)
    if not last:
        o_c = _attn_call(q_c, k_c, v_c, tq=tm_ctx)
        h_ctx = _post_call((o_c,), h_ctx, mod_ctx, w_out, *ln, tm=tm_ctx, alpha=alpha)
    return h_lat_new, h_ctx


def _gdn_layer(h_lat, h_ctx, mod_lat, mod_ctx, ln, w_in, conv_w, a_log, dt_bias, norm_w, w_out, *, alpha, last):
    bsz, seq, d = h_lat.shape
    tctx = h_ctx.shape[1]
    k_heads = d // 128
    v_heads = 2 * k_heads
    kw, vw = k_heads * GDN_DK, v_heads * GDN_DV
    qkv_w = 2 * kw + vw
    tm_lat, tm_ctx = min(256, seq), min(256, tctx)
    w_ba = w_in[:, qkv_w + vw:].reshape(d, 2, 2, k_heads, 2)
    w_ba = jnp.transpose(w_ba, (0, 3, 2, 1, 4)).reshape(d, k_heads, 8)
    w_ba = jnp.pad(w_ba, ((0, 0), (0, 0), (0, LANES - 8))).reshape(d, k_heads * LANES)
    w_ext = jnp.concatenate([w_in[:, :qkv_w + vw], w_ba], axis=1).astype(BF16)

    def gate_lanes(p):
        p = jnp.transpose(p.astype(F32).reshape(2, k_heads, 2), (1, 0, 2)).reshape(k_heads, 4)
        p = jnp.pad(p, ((0, 0), (4, LANES - 8)))
        return p.reshape(1, k_heads * LANES)

    al_ext, dtb_ext = gate_lanes(a_log), gate_lanes(dt_bias)
    s = jnp.zeros((bsz, k_heads * 4, GDN_DK, GDN_DV), F32)
    streams = []
    for h_in, m_in, tm in ((h_ctx, mod_ctx, tm_ctx), (h_lat, mod_lat, tm_lat)):
        qh, kh, kt, vh, gates, gate_rows, z = _gdn_in_call(h_in, m_in, w_ext, conv_w, al_ext, dtb_ext,
                                                           tm=tm, k_heads=k_heads, z_w=vw)
        o_f, o_b, s = _gdn_scan_call(qh, kh, kt, vh, gates, gate_rows, s, hg=8)
        streams.append((o_f, o_b, z))
    w_out = w_out.astype(BF16)
    nw = norm_w.reshape(1, -1)
    h_lat_new = _post_call(streams[1], h_lat, mod_lat, w_out, *ln, tm=tm_lat, alpha=alpha, gdn_norm=nw)
    if not last:
        h_ctx = _post_call(streams[0], h_ctx, mod_ctx, w_out, *ln, tm=tm_ctx, alpha=alpha, gdn_norm=nw)
    return h_lat_new, h_ctx


def kernel(x, c, ctx, c_ctx, w_mod, b_mod, ln1_g, ln1_b, ln2_g, ln2_b, w_ffn_in, w_ffn_out, mla_w_in, mla_q_norm, mla_kv_norm, mla_w_q_up, mla_w_kv_up, mla_w_out, gdn_w_in, gdn_conv, gdn_a_log, gdn_dt_bias, gdn_norm, gdn_w_out):
    bsz, _, d = x.shape
    depth = w_mod.shape[0]
    alpha = (2.0 * depth) ** 0.25

    cc = jnp.zeros((8, d), F32).at[:bsz].set(c).at[bsz].set(c_ctx)
    mod = _mod_call(cc, w_mod, b_mod)

    row = lambda a, i: a[i].reshape(1, -1)
    h_lat, h_ctx = x, ctx
    for i in range(depth):
        last = i == depth - 1
        mod_lat = mod[i, :bsz].reshape(bsz, 1, N_MOD * d)
        mod_ctx = jnp.broadcast_to(mod[i, bsz].reshape(1, 1, N_MOD * d), (bsz, 1, N_MOD * d))
        ln = (row(ln1_g, i), row(ln1_b, i), w_ffn_in[i].astype(BF16), w_ffn_out[i].astype(BF16),
              row(ln2_g, i), row(ln2_b, i))
        j = i // 2
        if i % 2 == 0:
            h_lat, h_ctx = _mla_layer(h_lat, h_ctx, mod_lat, mod_ctx, ln, mla_w_in[j], mla_q_norm[j],
                                      mla_kv_norm[j], mla_w_q_up[j], mla_w_kv_up[j], mla_w_out[j],
                                      alpha=alpha, last=last)
        else:
            h_lat, h_ctx = _gdn_layer(h_lat, h_ctx, mod_lat, mod_ctx, ln, gdn_w_in[j], gdn_conv[j],
                                      gdn_a_log[j], gdn_dt_bias[j], gdn_norm[j], gdn_w_out[j],
                                      alpha=alpha, last=last)
    return h_lat
```

```python
import functools
import math

import jax
import jax.numpy as jnp
from jax import lax
from jax.experimental import pallas as pl
from jax.experimental.pallas import tpu as pltpu

F32 = jnp.float32
BF16 = jnp.bfloat16

N_MOD = 6
LN_EPS = 1e-5
RMS_EPS = 1e-6
GRID_W = 64
ROPE_THETA = 10000.0

MLA_NOPE = 128
MLA_ROPE = 64
MLA_V = 128
MLA_QK = MLA_NOPE + 2 * MLA_ROPE

GDN_DK = 128
GDN_DV = 128
GDN_CONV = 5
CHUNK = 64
CONV_HALO = 8

V7X_VMEM_LIMIT = 56 * 1024 * 1024
LANES = 128

TOKEN_TILE = 256
ATTN_Q_TILE = 4096
ATTN_KEY_SLABS = 4
SCAN_KEY_HEADS = 8


def _cparams(sem):
    return pltpu.CompilerParams(dimension_semantics=sem, vmem_limit_bytes=V7X_VMEM_LIMIT)


def _const_spec(shape):
    nd = len(shape)
    return pl.BlockSpec(shape, lambda *_: (0,) * nd, pipeline_mode=pl.Buffered(1))


def _dot(a, b):
    return jnp.dot(a, b, preferred_element_type=F32)


def _dot_nt(a, b):
    return lax.dot_general(a, b, (((1,), (1,)), ((), ())), preferred_element_type=F32)


def _split_bf16(a):
    hi = a.astype(BF16)
    lo = (a - hi.astype(F32)).astype(BF16)
    return hi, lo


def _silu(x):
    return x * jax.nn.sigmoid(x)


def _layer_norm(x, g, b):
    mu = jnp.mean(x, axis=-1, keepdims=True)
    xc = x - mu
    var = jnp.mean(xc * xc, axis=-1, keepdims=True)
    return xc * lax.rsqrt(var + LN_EPS) * g + b


def _rms_norm(x, g):
    return x * lax.rsqrt(jnp.mean(x * x, axis=-1, keepdims=True) + RMS_EPS) * g


def _mod_kernel(c_ref, w_ref, b_ref, o_ref):
    a = _silu(c_ref[...]).astype(BF16)
    o_ref[0] = _dot(a, w_ref[0].astype(BF16)) + b_ref[0]


def _mod_call(cc, w_mod, b_mod):
    depth, d, n = w_mod.shape
    tn = 1536
    assert n % tn == 0
    return pl.pallas_call(
        _mod_kernel,
        grid=(depth, n // tn),
        in_specs=[
            pl.BlockSpec((8, d), lambda l, j: (0, 0)),
            pl.BlockSpec((1, d, tn), lambda l, j: (l, 0, j)),
            pl.BlockSpec((1, 1, tn), lambda l, j: (l, 0, j)),
        ],
        out_specs=pl.BlockSpec((1, 8, tn), lambda l, j: (l, 0, j)),
        out_shape=jax.ShapeDtypeStruct((depth, 8, n), F32),
        compiler_params=_cparams(("parallel", "parallel")),
        name="mod",
    )(cc, w_mod, b_mod.reshape(depth, 1, n))


def _mod_spec(d, j):
    return pl.BlockSpec((1, 1, d), lambda b, i: (b, 0, j))


VT_ROWS = MLA_V + 16


def _mla_proj_kernel(h_ref, sh_ref, sc_ref, win_ref, qn_ref, kvn_ref, wqt_ref, wk_ref, wvt_ref,
                     qmt_ref, km_ref, qt_ref, k_ref, vt_ref, *, heads, q_rank, kv_rank):
    x = h_ref[0]
    tm = x.shape[0]
    u = x * (1.0 + sc_ref[0]) + sh_ref[0]
    lat = _dot(u.astype(BF16), win_ref[...])
    cq = _rms_norm(lat[:, :q_rank], qn_ref[...]).astype(BF16)
    ckv = _rms_norm(lat[:, q_rank:q_rank + kv_rank], kvn_ref[...]).astype(BF16)
    kr2 = lat[:, q_rank + kv_rank:]
    qt_all = _dot_nt(wqt_ref[...], cq)
    k_all = _dot(ckv, wk_ref[...])
    vt_all = _dot_nt(wvt_ref[...], ckv)
    qmt = qmt_ref[...]
    t = kr2 * km_ref[...]
    krot2 = (t + pltpu.roll(t, MLA_ROPE, axis=1)).astype(BF16)
    ones_rows = jnp.where(lax.broadcasted_iota(jnp.int32, (VT_ROWS - MLA_V, tm), 0) == 0, 1.0, 0.0).astype(BF16)
    for hh in range(heads):
        qt_ref[0, hh] = (qt_all[hh * MLA_QK:(hh + 1) * MLA_QK, :] * qmt).astype(BF16)
        k_ref[0, hh, :, :MLA_NOPE] = k_all[:, hh * MLA_NOPE:(hh + 1) * MLA_NOPE].astype(BF16)
        k_ref[0, hh, :, MLA_NOPE:] = krot2
        vt_ref[0, hh, 0, :MLA_V, :] = vt_all[hh * MLA_V:(hh + 1) * MLA_V, :].astype(BF16)
        vt_ref[0, hh, 0, MLA_V:, :] = ones_rows


def _mla_proj_call(h, mod, w_in, qn, kvn, wqt, wk, wvt, qmt, km, *, tm, heads):
    b, t, d = h.shape
    q_rank, kv_rank = qn.shape[1], kvn.shape[1]
    assert t % tm == 0
    kern = functools.partial(_mla_proj_kernel, heads=heads, q_rank=q_rank, kv_rank=kv_rank)
    return pl.pallas_call(
        kern,
        grid=(b, t // tm),
        in_specs=[
            pl.BlockSpec((1, tm, d), lambda bb, i: (bb, i, 0)),
            _mod_spec(d, 0), _mod_spec(d, 1),
            _const_spec(w_in.shape), _const_spec(qn.shape), _const_spec(kvn.shape),
            _const_spec(wqt.shape), _const_spec(wk.shape), _const_spec(wvt.shape),
            pl.BlockSpec((MLA_QK, tm), lambda bb, i: (0, i)),
            pl.BlockSpec((tm, 2 * MLA_ROPE), lambda bb, i: (i, 0)),
        ],
        out_specs=[
            pl.BlockSpec((1, heads, MLA_QK, tm), lambda bb, i: (bb, 0, 0, i)),
            pl.BlockSpec((1, heads, tm, MLA_QK), lambda bb, i: (bb, 0, i, 0)),
            pl.BlockSpec((1, heads, 1, VT_ROWS, tm), lambda bb, i: (bb, 0, i, 0, 0)),
        ],
        out_shape=[
            jax.ShapeDtypeStruct((b, heads, MLA_QK, t), BF16),
            jax.ShapeDtypeStruct((b, heads, t, MLA_QK), BF16),
            jax.ShapeDtypeStruct((b, heads, t // tm, VT_ROWS, tm), BF16),
        ],
        compiler_params=_cparams(("parallel", "parallel")),
        name="mla_proj",
    )(h, mod, mod, w_in, qn, kvn, wqt, wk, wvt, qmt, km)


QGROUP = 256


def _attn_kernel(*refs, slabs_per_step, n_lat, n_groups):
    if n_lat:
        qt_ref, kc_ref, vtc_ref, kl_ref, vtl_ref, o_ref, m_ref, acc_ref, st_ref = refs
    else:
        qt_ref, kc_ref, vtc_ref, o_ref, m_ref, acc_ref = refs

    def scores(kblk, g):
        return _dot(kblk, qt_ref[0, 0, :, g * QGROUP:(g + 1) * QGROUP])

    def softmax_pv(st, vt_slabs, g, first):
        slab = vt_slabs[0].shape[1]
        cols = slice(g * QGROUP, (g + 1) * QGROUP)
        cmax = jnp.max(st, axis=0, keepdims=True)
        if first:
            m_new = cmax
        else:
            m_old = m_ref[:, cols]
            m_new = jnp.maximum(m_old, cmax)
        pt = jnp.exp2(st - m_new).astype(BF16)
        pv = None
        for s, vt in enumerate(vt_slabs):
            term = _dot(vt, pt[s * slab:(s + 1) * slab, :])
            pv = term if pv is None else pv + term
        if first:
            acc_ref[:, cols] = pv
        else:
            acc_ref[:, cols] = acc_ref[:, cols] * jnp.exp2(m_old - m_new) + pv
        m_ref[:, cols] = m_new

    kc = kc_ref[0, 0]
    vtc = [vtc_ref[0, 0, s] for s in range(vtc_ref.shape[2])]
    sts = [scores(kc, g) for g in range(n_groups)]
    if n_lat:
        slab = vtl_ref.shape[4]
        tk = slabs_per_step * slab
        k0 = kl_ref[0, 0, pl.ds(0, tk), :]
        nxt = scores(k0, 0)
        for g in range(n_groups):
            cur, nxt = nxt, (scores(k0, g + 1) if g + 1 < n_groups else None)
            softmax_pv(sts[g], vtc, g, True)
            st_ref[g] = cur
    else:
        for g in range(n_groups):
            softmax_pv(sts[g], vtc, g, True)
    if n_lat:

        def body(j, carry):
            off = pl.multiple_of(jnp.minimum(j + 1, n_lat - 1) * tk, tk)
            k_next = kl_ref[0, 0, pl.ds(off, tk), :]
            vts = [vtl_ref[0, 0, j * slabs_per_step + s] for s in range(slabs_per_step)]
            nxt = scores(k_next, 0)
            for g in range(n_groups):
                cur, nxt = nxt, (scores(k_next, g + 1) if g + 1 < n_groups else None)
                softmax_pv(st_ref[g], vts, g, False)
                st_ref[g] = cur
            return carry
        lax.fori_loop(0, n_lat, body, 0)
    acc = acc_ref[...]
    ot = acc[:MLA_V] * (1.0 / acc[MLA_V:MLA_V + 1])
    o_ref[0] = ot.T.astype(o_ref.dtype)


def _attn_call(qt, k_ctx, vt_ctx, k_lat=None, vt_lat=None, *, tq, slabs_per_step=ATTN_KEY_SLABS):
    b, h, dqk, t = qt.shape
    tc = k_ctx.shape[2]
    assert t % tq == 0 and tq % QGROUP == 0
    n_lat = 0
    in_specs = [
        pl.BlockSpec((1, 1, dqk, tq), lambda bb, hh, i: (bb, hh, 0, i)),
        pl.BlockSpec((1, 1, tc, dqk), lambda bb, hh, i: (bb, hh, 0, 0)),
        pl.BlockSpec((1, 1) + vt_ctx.shape[2:], lambda bb, hh, i: (bb, hh, 0, 0, 0)),
    ]
    args = [qt, k_ctx, vt_ctx]
    if k_lat is not None:
        tl = k_lat.shape[2]
        ns = vt_lat.shape[2]
        assert ns % slabs_per_step == 0
        n_lat = ns // slabs_per_step
        in_specs += [
            pl.BlockSpec((1, 1, tl, dqk), lambda bb, hh, i: (bb, hh, 0, 0)),
            pl.BlockSpec((1, 1) + vt_lat.shape[2:], lambda bb, hh, i: (bb, hh, 0, 0, 0)),
        ]
        args += [k_lat, vt_lat]
    n_groups = tq // QGROUP
    kern = functools.partial(_attn_kernel, slabs_per_step=slabs_per_step, n_lat=n_lat, n_groups=n_groups)
    scratch = [pltpu.VMEM((1, tq), F32), pltpu.VMEM((VT_ROWS, tq), F32)]
    if n_lat:
        scratch.append(pltpu.VMEM((n_groups, slabs_per_step * vt_lat.shape[4], QGROUP), F32))
    return pl.pallas_call(
        kern,
        grid=(b, h, t // tq),
        in_specs=in_specs,
        out_specs=pl.BlockSpec((1, tq, MLA_V), lambda bb, hh, i: (bb, i, hh)),
        out_shape=jax.ShapeDtypeStruct((b, t, h * MLA_V), BF16),
        scratch_shapes=scratch,
        compiler_params=_cparams(("parallel", "parallel", "arbitrary")),
        name="attn_lat" if n_lat else "attn_ctx",
    )(*args)


def _post_tail(y_in, h_ref, g1_ref, sh2_ref, sc2_ref, g2_ref, wout_ref, l1g_ref, l1b_ref,
               wfi_ref, wfo_ref, l2g_ref, l2b_ref, o_ref, *, alpha, d_ff):
    y = _dot(y_in, wout_ref[...])
    h1 = _layer_norm(alpha * h_ref[0] + g1_ref[0] * y, l1g_ref[...], l1b_ref[...])
    u2 = h1 * (1.0 + sc2_ref[0]) + sh2_ref[0]
    gu = _dot(u2.astype(BF16), wfi_ref[...])
    act = _silu(gu[:, :d_ff]) * gu[:, d_ff:]
    f = _dot(act.astype(BF16), wfo_ref[...])
    o_ref[0] = _layer_norm(alpha * h1 + g2_ref[0] * f, l2g_ref[...], l2b_ref[...])


def _post_mla_kernel(a_ref, *rest, alpha, d_ff):
    _post_tail(a_ref[0], *rest, alpha=alpha, d_ff=d_ff)


def _post_gdn_kernel(of_ref, ob_ref, z_ref, nw_ref, *rest, alpha, d_ff, k_heads):
    nw = nw_ref[...]
    parts = []
    for kh in range(k_heads):
        o2 = of_ref[0, kh] + ob_ref[0, kh]
        for v in range(2):
            o = o2[:, v * GDN_DV:(v + 1) * GDN_DV]
            vh = 2 * kh + v
            z = z_ref[0, :, vh * GDN_DV:(vh + 1) * GDN_DV]
            parts.append((_rms_norm(o, nw) * _silu(z)).astype(BF16))
    _post_tail(jnp.concatenate(parts, axis=1), *rest, alpha=alpha, d_ff=d_ff)


def _post_call(mixer_in, h, mod, w_out, l1g, l1b, wfi, wfo, l2g, l2b, *, tm, alpha, gdn_norm=None):
    b, t, d = h.shape
    d_ff = wfo.shape[0]
    assert t % tm == 0
    row = lambda w: pl.BlockSpec((1, tm, w), lambda bb, i: (bb, i, 0))
    if gdn_norm is None:
        (a,) = mixer_in
        kern = functools.partial(_post_mla_kernel, alpha=alpha, d_ff=d_ff)
        head_specs = [row(a.shape[2])]
        head_args = [a]
    else:
        o_f, o_b, z = mixer_in
        kh = o_f.shape[1]
        kern = functools.partial(_post_gdn_kernel, alpha=alpha, d_ff=d_ff, k_heads=kh)
        hm = pl.BlockSpec((1, kh, tm, 2 * GDN_DV), lambda bb, i: (bb, 0, i, 0))
        head_specs = [hm, hm, row(z.shape[2]), _const_spec(gdn_norm.shape)]
        head_args = [o_f, o_b, z, gdn_norm]
    return pl.pallas_call(
        kern,
        grid=(b, t // tm),
        in_specs=head_specs + [
            row(d),
            _mod_spec(d, 2), _mod_spec(d, 3), _mod_spec(d, 4), _mod_spec(d, 5),
            _const_spec(w_out.shape), _const_spec(l1g.shape), _const_spec(l1b.shape),
            _const_spec(wfi.shape), _const_spec(wfo.shape),
            _const_spec(l2g.shape), _const_spec(l2b.shape),
        ],
        out_specs=row(d),
        out_shape=jax.ShapeDtypeStruct((b, t, d), F32),
        compiler_params=_cparams(("parallel", "parallel")),
        name="post_mla" if gdn_norm is None else "post_gdn",
    )(*head_args, h, mod, mod, mod, mod, w_out, l1g, l1b, wfi, wfo, l2g, l2b)


def _chunk_cumsum(g, suffix):
    n = g.shape[0]
    pos = lax.broadcasted_iota(jnp.int32, g.shape, 0) % CHUNK
    s = 1
    while s < CHUNK:
        if suffix:
            shifted = pltpu.roll(g, n - s, axis=0)
            ok = pos < CHUNK - s
        else:
            shifted = pltpu.roll(g, s, axis=0)
            ok = pos >= s
        g = g + jnp.where(ok, shifted, 0.0)
        s *= 2
    return g


CONV_ROWS = 64
CONV_COLS = 512


def _gdn_in_kernel(hp_ref, hc_ref, hn_ref, sh_ref, sc_ref, w_ref, cw_ref, al_ref, dtb_ref,
                   q_ref, k_ref, kt_ref, v_ref, gt_ref, gr_ref, z_ref, xe_ref, y_ref, *, k_heads, tm, z_w):
    i = pl.program_id(1)
    n_i = pl.num_programs(1)
    kw = k_heads * GDN_DK
    c_all = xe_ref.shape[1]
    n_chunks = tm // CHUNK
    half = GDN_CONV // 2
    h_ext = jnp.concatenate([hp_ref[0], hc_ref[0], hn_ref[0]], axis=0)
    u = (h_ext * (1.0 + sc_ref[0]) + sh_ref[0]).astype(BF16)
    keep_prev = jnp.where(i > 0, 1.0, 0.0)
    keep_next = jnp.where(i < n_i - 1, 1.0, 0.0)

    def project(c0):
        return _dot(u, w_ref[:, c0:c0 + CONV_COLS])

    def conv_tile(p, c0):
        cols = slice(c0, c0 + CONV_COLS)
        xe_ref[0:CONV_HALO, cols] = p[0:CONV_HALO] * keep_prev
        xe_ref[CONV_HALO:CONV_HALO + tm, cols] = p[CONV_HALO:CONV_HALO + tm]
        xe_ref[CONV_HALO + tm:, cols] = p[CONV_HALO + tm:] * keep_next
        w = cw_ref[:, cols]
        for r0 in range(0, tm, CONV_ROWS):
            acc = None
            for j in range(GDN_CONV):
                start = CONV_HALO - half + j + r0
                term = xe_ref[start:start + CONV_ROWS, cols] * w[j:j + 1, :]
                acc = term if acc is None else acc + term
            y_ref[r0:r0 + CONV_ROWS, cols] = _silu(acc)

    tiles = list(range(0, c_all, CONV_COLS))
    nxt = project(tiles[0])
    for t_idx, c0 in enumerate(tiles):
        cur = nxt
        if t_idx + 1 < len(tiles):
            nxt = project(tiles[t_idx + 1])
        else:
            nxt = _dot(u[CONV_HALO:CONV_HALO + tm], w_ref[:, c_all:])
        conv_tile(cur, c0)
    z_ref[0] = nxt[:, :z_w]
    ba = nxt[:, z_w:]
    for kh in range(k_heads):
        qh = y_ref[:, kh * GDN_DK:(kh + 1) * GDN_DK]
        q_ref[0, kh] = (qh * lax.rsqrt(jnp.sum(qh * qh, axis=-1, keepdims=True) + RMS_EPS)
                        * (GDN_DK ** -0.5)).astype(BF16)
        kk = y_ref[:, kw + kh * GDN_DK:kw + (kh + 1) * GDN_DK]
        kn = kk * lax.rsqrt(jnp.sum(kk * kk, axis=-1, keepdims=True) + RMS_EPS)
        k_ref[0, kh] = kn.astype(BF16)
        knt = kn.T.astype(BF16)
        for n in range(n_chunks):
            kt_ref[0, kh, n] = knt[:, n * CHUNK:(n + 1) * CHUNK]
        v_ref[0, kh] = y_ref[:, 2 * kw + kh * 2 * GDN_DV:2 * kw + (kh + 1) * 2 * GDN_DV].astype(BF16)
    lane = lax.broadcasted_iota(jnp.int32, ba.shape, 1) % LANES
    beta = jax.nn.sigmoid(ba)
    g = -jnp.exp(al_ref[...]) * jax.nn.softplus(ba + dtb_ref[...])
    g = jnp.where((lane >= 4) & (lane < 8), g, 0.0)
    gcum = jnp.where(lane < 6, _chunk_cumsum(g, False), _chunk_cumsum(g, True))
    gates = jnp.where(lane < 4, beta, gcum)
    for kh in range(k_heads):
        gk = gates[:, kh * LANES:(kh + 1) * LANES]
        gt_ref[0, kh] = gk
        rows = gk.T[0:8, :]
        for n in range(n_chunks):
            piece = rows[:, n * CHUNK:(n + 1) * CHUNK]
            gr_ref[0, kh, n] = jnp.concatenate([piece, piece], axis=1)


def _gdn_in_call(h, mod, w, conv_w, a_log_ext, dtb_ext, *, tm, k_heads, z_w):
    b, t, d = h.shape
    c = conv_w.shape[1]
    assert t % tm == 0 and tm % CHUNK == 0 and tm % CONV_HALO == 0
    assert tm % CONV_ROWS == 0 and c % CONV_COLS == 0
    r = tm // CONV_HALO
    nblk8 = t // CONV_HALO
    ncb = tm // CHUNK
    kern = functools.partial(_gdn_in_kernel, k_heads=k_heads, tm=tm, z_w=z_w)
    hm = lambda wd: pl.BlockSpec((1, k_heads, tm, wd), lambda bb, i: (bb, 0, i, 0))
    per_chunk = lambda rr, wd: pl.BlockSpec((1, k_heads, ncb, rr, wd), lambda bb, i: (bb, 0, i, 0, 0))
    return pl.pallas_call(
        kern,
        grid=(b, t // tm),
        in_specs=[
            pl.BlockSpec((1, CONV_HALO, d), lambda bb, i: (bb, jnp.maximum(i * r - 1, 0), 0)),
            pl.BlockSpec((1, tm, d), lambda bb, i: (bb, i, 0)),
            pl.BlockSpec((1, CONV_HALO, d), lambda bb, i: (bb, jnp.minimum((i + 1) * r, nblk8 - 1), 0)),
            _mod_spec(d, 0), _mod_spec(d, 1),
            _const_spec(w.shape), _const_spec(conv_w.shape),
            _const_spec(a_log_ext.shape), _const_spec(dtb_ext.shape),
        ],
        out_specs=[hm(GDN_DK), hm(GDN_DK), per_chunk(GDN_DK, CHUNK), hm(2 * GDN_DV), hm(LANES),
                   per_chunk(8, LANES), pl.BlockSpec((1, tm, z_w), lambda bb, i: (bb, i, 0))],
        out_shape=[
            jax.ShapeDtypeStruct((b, k_heads, t, GDN_DK), BF16),
            jax.ShapeDtypeStruct((b, k_heads, t, GDN_DK), BF16),
            jax.ShapeDtypeStruct((b, k_heads, t // CHUNK, GDN_DK, CHUNK), BF16),
            jax.ShapeDtypeStruct((b, k_heads, t, 2 * GDN_DV), BF16),
            jax.ShapeDtypeStruct((b, k_heads, t, LANES), F32),
            jax.ShapeDtypeStruct((b, k_heads, t // CHUNK, 8, LANES), F32),
            jax.ShapeDtypeStruct((b, t, z_w), F32),
        ],
        scratch_shapes=[pltpu.VMEM((tm + 2 * CONV_HALO, c), F32), pltpu.VMEM((tm, c), F32)],
        compiler_params=_cparams(("parallel", "parallel")),
        name="gdn_in",
    )(h, h, h, mod, mod, w, conv_w, a_log_ext, dtb_ext)


def _dot_split(a_hi, a_lo, b_hi, b_lo):
    return _dot(jnp.concatenate([a_hi, a_lo], axis=1), jnp.concatenate([b_hi, b_lo, b_hi, b_lo], axis=0))


def _gdn_scan_kernel(qf_ref, kf_ref, ktf_ref, vf_ref, gf_ref, grf_ref,
                     qb_ref, kb_ref, ktb_ref, vb_ref, gb_ref, grb_ref, s0_ref,
                     of_ref, ob_ref, sfin_ref, s_ref, *, hg):
    n = pl.program_id(2)

    @pl.when(n == 0)
    def _():
        s_ref[...] = s0_ref[0]

    c = CHUNK
    row = lax.broadcasted_iota(jnp.int32, (c, LANES), 0)
    col = lax.broadcasted_iota(jnp.int32, (c, LANES), 1) % c
    dirs = ((qf_ref, kf_ref, ktf_ref, vf_ref, gf_ref, grf_ref), (qb_ref, kb_ref, ktb_ref, vb_ref, gb_ref, grb_ref))

    eye = jnp.where(row == col, 1.0, 0.0)
    zero_rows = jnp.zeros((c, 2 * GDN_DV), BF16)
    pairs = []
    blk_r, blk_c = row, col
    b = 1
    while b < c:
        pair = (blk_r // 2 == blk_c // 2) & (blk_r != blk_c)
        pairs.append(pair if b == 1 else jnp.where(pair, 1.0, 0.0).astype(BF16))
        blk_r, blk_c = blk_r // 2, blk_c // 2
        b *= 2

    def stages(heads):
        chains = []
        for j in heads:
            for d in range(2):
                q_ref, k_ref, kt_ref, v_ref, g_ref, gr_ref = dirs[d]
                qc, kc, v2, gt = q_ref[0, j], k_ref[0, j], v_ref[0, j], g_ref[0, j]
                gtt = gr_ref[0, j, 0]
                ktc = kt_ref[0, j, 0]
                kq = _dot_nt(jnp.concatenate([kc, qc], axis=0), jnp.concatenate([kc, kc], axis=0))
                for v in range(2):
                    chains.append(dict(j=j, d=d, v=v, qc=qc, kc=kc, vv=v2[:, v * GDN_DV:(v + 1) * GDN_DV],
                                       gt=gt, gtt=gtt, ktc=ktc, kk=kq[:c], qk=kq[c:]))
            yield
        for i, ch in enumerate(chains):
            slot = 2 * ch["d"] + ch["v"]
            if ch["d"] == 1:
                incl, strict = row <= col, row < col
            else:
                incl, strict = row >= col, row > col
            beta_col = jnp.broadcast_to(ch["gt"][:, slot:slot + 1], (c, LANES))
            g_col = jnp.broadcast_to(ch["gt"][:, 4 + slot:5 + slot], (c, LANES))
            g_row = ch["gtt"][4 + slot:5 + slot, :]
            decay = jnp.where(incl, jnp.exp(jnp.where(incl, g_col - g_row, 0.0)), 0.0)
            eg_col = jnp.exp(g_col)
            ch["g_row"] = g_row
            lmat = jnp.where(strict, ch["kk"] * beta_col * decay, 0.0)
            ch.update(
                lm=lmat.astype(BF16), tinv=eye - jnp.where(pairs[0], lmat, 0.0),
                attn=(ch["qk"] * decay)[:, :c].astype(BF16),
                eg_col=eg_col, g_col=g_col,
                rhs=jnp.concatenate([ch["vv"].astype(F32) * beta_col,
                                     ch["kc"].astype(F32) * (beta_col * eg_col)], axis=1).astype(BF16))
            if i % 4 == 3:
                yield
        for pair_bf in pairs[1:]:
            for ch in chains:
                ch["t_hi"], ch["t_lo"] = _split_bf16(ch["tinv"])
                ch["w"] = _dot(ch["lm"] * pair_bf, jnp.concatenate([ch["t_hi"], ch["t_lo"]], axis=0))
            yield
            for ch in chains:
                w_hi, w_lo = _split_bf16(ch["w"])
                ch["tinv"] = ch["tinv"] - _dot_split(ch["t_hi"], ch["t_lo"], w_hi, w_lo)
            yield
        for ch in chains:
            t_hi, t_lo = _split_bf16(ch["tinv"])
            r = ch["rhs"]
            ch["x"] = _dot(jnp.concatenate([t_hi, t_lo], axis=1),
                           jnp.concatenate([r, zero_rows, r, zero_rows], axis=0))
        yield
        for ch in chains:
            ch["s_idx"] = ch["j"] * 4 + 2 * ch["d"] + ch["v"]
            ch["s_old"] = s_ref[ch["s_idx"]]
            ch["wq"] = _dot(jnp.concatenate([ch["x"][:, GDN_DV:].astype(BF16), ch["qc"]], axis=0),
                            ch["s_old"].astype(BF16))
        yield
        for ch in chains:
            g_col = ch["g_col"]
            ch["g_last"] = g_col[0:1, :] if ch["d"] == 1 else g_col[c - 1:c, :]
            v_new = (ch["x"][:, :GDN_DV] - ch["wq"][:c]).astype(BF16)
            dec_row = jnp.exp(ch["g_last"] - ch["g_row"])[:, :c]
            k_dec_t = (ch["ktc"].astype(F32) * dec_row).astype(BF16)
            ch["av"] = _dot(ch["attn"], v_new)
            ch["kv"] = _dot(k_dec_t, v_new)
        yield
        outs = {}
        for ch in chains:
            s_ref[ch["s_idx"]] = ch["s_old"] * jnp.exp(ch["g_last"]) + ch["kv"]
            outs[ch["j"], ch["d"], ch["v"]] = ch["eg_col"] * ch["wq"][c:] + ch["av"]
        for j in heads:
            for d, o_ref in enumerate((of_ref, ob_ref)):
                o_ref[0, j] = jnp.concatenate([outs[j, d, 0], outs[j, d, 1]], axis=1)

    for _ in stages(list(range(hg))):
        pass

    @pl.when(n == pl.num_programs(2) - 1)
    def _():
        sfin_ref[0] = s_ref[...]


def _gdn_scan_call(q, k, kt, v, gates, gate_rows, s0, *, hg):
    b, kh, t, _ = q.shape
    assert kh % hg == 0 and t % CHUNK == 0
    nc = t // CHUNK
    ng = kh // hg
    fwd = lambda w: pl.BlockSpec((1, hg, CHUNK, w), lambda bb, g, n: (bb, g, n, 0))
    bwd = lambda w: pl.BlockSpec((1, hg, CHUNK, w), lambda bb, g, n: (bb, g, nc - 1 - n, 0))
    fwd_c = lambda r, w: pl.BlockSpec((1, hg, 1, r, w), lambda bb, g, n: (bb, g, n, 0, 0))
    bwd_c = lambda r, w: pl.BlockSpec((1, hg, 1, r, w), lambda bb, g, n: (bb, g, nc - 1 - n, 0, 0))
    st = pl.BlockSpec((1, hg * 4, GDN_DK, GDN_DV), lambda bb, g, n: (bb, g, 0, 0))
    kern = functools.partial(_gdn_scan_kernel, hg=hg)
    return pl.pallas_call(
        kern,
        grid=(b, ng, nc),
        in_specs=[fwd(GDN_DK), fwd(GDN_DK), fwd_c(GDN_DK, CHUNK), fwd(2 * GDN_DV), fwd(LANES), fwd_c(8, LANES),
                  bwd(GDN_DK), bwd(GDN_DK), bwd_c(GDN_DK, CHUNK), bwd(2 * GDN_DV), bwd(LANES), bwd_c(8, LANES),
                  st],
        out_specs=[fwd(2 * GDN_DV), bwd(2 * GDN_DV), st],
        out_shape=[
            jax.ShapeDtypeStruct((b, kh, t, 2 * GDN_DV), F32),
            jax.ShapeDtypeStruct((b, kh, t, 2 * GDN_DV), F32),
            jax.ShapeDtypeStruct(s0.shape, F32),
        ],
        scratch_shapes=[pltpu.VMEM((hg * 4, GDN_DK, GDN_DV), F32)],
        compiler_params=_cparams(("parallel", "parallel", "arbitrary")),
        name="gdn_scan",
    )(q, k, kt, v, gates, gate_rows, q, k, kt, v, gates, gate_rows, s0)


def _rope_tables(t, q_scale):
    assert t % GRID_W == 0
    rows = t // GRID_W
    n_freq = MLA_ROPE // 4
    inv_freq = ROPE_THETA ** (-(2.0 * jnp.arange(n_freq, dtype=F32)) / (MLA_ROPE // 2))
    ar = jnp.arange(rows, dtype=F32)[:, None] * inv_freq
    ac = jnp.arange(GRID_W, dtype=F32)[:, None] * inv_freq
    by_row = lambda x: jnp.broadcast_to(x[:, None, :], (rows, GRID_W, n_freq)).reshape(t, n_freq)
    by_col = lambda x: jnp.broadcast_to(x[None, :, :], (rows, GRID_W, n_freq)).reshape(t, n_freq)
    cr, sr, cc, sc = by_row(jnp.cos(ar)), by_row(jnp.sin(ar)), by_col(jnp.cos(ac)), by_col(jnp.sin(ac))
    cs = jnp.concatenate([cr, cr, cc, cc], axis=1)
    sn = jnp.concatenate([-sr, sr, -sc, sc], axis=1)
    km = jnp.concatenate([cs, sn], axis=1)
    qm = q_scale * jnp.concatenate([jnp.ones((t, MLA_NOPE), F32), cs, sn], axis=1)
    return qm, km


def _no_rope_tables(t, q_scale):
    one = jnp.ones((t, MLA_ROPE), F32)
    zero = jnp.zeros((t, MLA_ROPE), F32)
    km = jnp.concatenate([one, zero], axis=1)
    qm = q_scale * jnp.concatenate([jnp.ones((t, MLA_NOPE), F32), one, zero], axis=1)
    return qm, km


def _swap_perm():
    f = MLA_ROPE // 4
    return jnp.array(list(range(f, 2 * f)) + list(range(0, f)) + list(range(3 * f, 4 * f)) + list(range(2 * f, 3 * f)))


def _mla_layer(h_lat, h_ctx, mod_lat, mod_ctx, ln, w_in, q_norm, kv_norm, w_q_up, w_kv_up, w_out, *, alpha, last):
    bsz, seq, d = h_lat.shape
    tctx = h_ctx.shape[1]
    heads = d // 128
    q_rank, kv_rank = q_norm.shape[0], kv_norm.shape[0]
    tm_lat, tm_ctx = min(TOKEN_TILE, seq), min(TOKEN_TILE, tctx)
    q_scale = (MLA_NOPE + MLA_ROPE) ** -0.5 * math.log2(math.e)
    perm = _swap_perm()
    rope_cols = w_in[:, q_rank + kv_rank:]
    w_in_ext = jnp.concatenate([w_in, rope_cols[:, perm]], axis=1).astype(BF16)
    wq = w_q_up.reshape(q_rank, heads, MLA_NOPE + MLA_ROPE)
    wq_ext = jnp.concatenate([wq, wq[:, :, MLA_NOPE:][:, :, perm]], axis=2)
    wqt = wq_ext.reshape(q_rank, heads * MLA_QK).T.astype(BF16)
    wkv = w_kv_up.reshape(kv_rank, heads, MLA_NOPE + MLA_V)
    wk = wkv[:, :, :MLA_NOPE].reshape(kv_rank, -1).astype(BF16)
    wvt = wkv[:, :, MLA_NOPE:].reshape(kv_rank, -1).T.astype(BF16)
    qn, kvn = q_norm.reshape(1, -1), kv_norm.reshape(1, -1)
    qm_l, km_l = _rope_tables(seq, q_scale)
    qm_c, km_c = _no_rope_tables(tctx, q_scale)
    q_l, k_l, v_l = _mla_proj_call(h_lat, mod_lat, w_in_ext, qn, kvn, wqt, wk, wvt, qm_l.T, km_l,
                                   tm=tm_lat, heads=heads)
    q_c, k_c, v_c = _mla_proj_call(h_ctx, mod_ctx, w_in_ext, qn, kvn, wqt, wk, wvt, qm_c.T, km_c,
                                   tm=tm_ctx, heads=heads)
    o_l = _attn_call(q_l, k_c, v_c, k_l, v_l, tq=min(ATTN_Q_TILE, seq))
    w_out = w_out.astype(BF16)
    h_lat_new = _post_call((o_l,), h_lat, mod_lat, w_out, *ln, tm=tm_lat, alpha=alpha)
    if not last:
        o_c = _attn_call(q_c, k_c, v_c, tq=tm_ctx)
        h_ctx = _post_call((o_c,), h_ctx, mod_ctx, w_out, *ln, tm=tm_ctx, alpha=alpha)
    return h_lat_new, h_ctx


def _gdn_layer(h_lat, h_ctx, mod_lat, mod_ctx, ln, w_in, conv_w, a_log, dt_bias, norm_w, w_out, *, alpha, last):
    bsz, seq, d = h_lat.shape
    tctx = h_ctx.shape[1]
    k_heads = d // 128
    v_heads = 2 * k_heads
    kw, vw = k_heads * GDN_DK, v_heads * GDN_DV
    qkv_w = 2 * kw + vw
    tm_lat, tm_ctx = min(TOKEN_TILE, seq), min(TOKEN_TILE, tctx)
    w_ba = w_in[:, qkv_w + vw:].reshape(d, 2, 2, k_heads, 2)
    w_ba = jnp.transpose(w_ba, (0, 3, 2, 1, 4)).reshape(d, k_heads, 8)
    w_ba = jnp.pad(w_ba, ((0, 0), (0, 0), (0, LANES - 8))).reshape(d, k_heads * LANES)
    w_ext = jnp.concatenate([w_in[:, :qkv_w + vw], w_ba], axis=1).astype(BF16)

    def gate_lanes(p):
        p = jnp.transpose(p.astype(F32).reshape(2, k_heads, 2), (1, 0, 2)).reshape(k_heads, 4)
        p = jnp.pad(p, ((0, 0), (4, LANES - 8)))
        return p.reshape(1, k_heads * LANES)

    al_ext, dtb_ext = gate_lanes(a_log), gate_lanes(dt_bias)
    s = jnp.zeros((bsz, k_heads * 4, GDN_DK, GDN_DV), F32)
    streams = []
    for h_in, m_in, tm in ((h_ctx, mod_ctx, tm_ctx), (h_lat, mod_lat, tm_lat)):
        qh, kh, kt, vh, gates, gate_rows, z = _gdn_in_call(h_in, m_in, w_ext, conv_w, al_ext, dtb_ext,
                                                           tm=tm, k_heads=k_heads, z_w=vw)
        o_f, o_b, s = _gdn_scan_call(qh, kh, kt, vh, gates, gate_rows, s, hg=min(SCAN_KEY_HEADS, k_heads))
        streams.append((o_f, o_b, z))
    w_out = w_out.astype(BF16)
    nw = norm_w.reshape(1, -1)
    h_lat_new = _post_call(streams[1], h_lat, mod_lat, w_out, *ln, tm=tm_lat, alpha=alpha, gdn_norm=nw)
    if not last:
        h_ctx = _post_call(streams[0], h_ctx, mod_ctx, w_out, *ln, tm=tm_ctx, alpha=alpha, gdn_norm=nw)
    return h_lat_new, h_ctx


def kernel(x, c, ctx, c_ctx, w_mod, b_mod, ln1_g, ln1_b, ln2_g, ln2_b, w_ffn_in, w_ffn_out, mla_w_in, mla_q_norm, mla_kv_norm, mla_w_q_up, mla_w_kv_up, mla_w_out, gdn_w_in, gdn_conv, gdn_a_log, gdn_dt_bias, gdn_norm, gdn_w_out):
    bsz, _, d = x.shape
    depth = w_mod.shape[0]
    alpha = (2.0 * depth) ** 0.25

    cc = jnp.zeros((8, d), F32).at[:bsz].set(c).at[bsz].set(c_ctx)
    mod = _mod_call(cc, w_mod, b_mod)

    row = lambda a, i: a[i].reshape(1, -1)
    h_lat, h_ctx = x, ctx
    for i in range(depth):
        last = i == depth - 1
        mod_lat = mod[i, :bsz].reshape(bsz, 1, N_MOD * d)
        mod_ctx = jnp.broadcast_to(mod[i, bsz].reshape(1, 1, N_MOD * d), (bsz, 1, N_MOD * d))
        ln = (row(ln1_g, i), row(ln1_b, i), w_ffn_in[i].astype(BF16), w_ffn_out[i].astype(BF16),
              row(ln2_g, i), row(ln2_b, i))
        j = i // 2
        if i % 2 == 0:
            h_lat, h_ctx = _mla_layer(h_lat, h_ctx, mod_lat, mod_ctx, ln, mla_w_in[j], mla_q_norm[j],
                                      mla_kv_norm[j], mla_w_q_up[j], mla_w_kv_up[j], mla_w_out[j],
                                      alpha=alpha, last=last)
        else:
            h_lat, h_ctx = _gdn_layer(h_lat, h_ctx, mod_lat, mod_ctx, ln, gdn_w_in[j], gdn_conv[j],
                                      gdn_a_log[j], gdn_dt_bias[j], gdn_norm[j], gdn_w_out[j],
                                      alpha=alpha, last=last)
    return h_lat
```

```python
import functools
import math

import jax
import jax.numpy as jnp
from jax import lax
from jax.experimental import pallas as pl
from jax.experimental.pallas import tpu as pltpu

F32 = jnp.float32
BF16 = jnp.bfloat16

N_MOD = 6
LN_EPS = 1e-5
RMS_EPS = 1e-6
GRID_W = 64
ROPE_THETA = 10000.0

MLA_NOPE = 128
MLA_ROPE = 64
MLA_V = 128
MLA_QK = MLA_NOPE + 2 * MLA_ROPE

GDN_DK = 128
GDN_DV = 128
GDN_CONV = 5
CHUNK = 64
CONV_HALO = 8

V7X_VMEM_LIMIT = 56 * 1024 * 1024
LANES = 128

TOKEN_TILE = 256
ATTN_Q_TILE = 4096
ATTN_KEY_SLABS = 4
ATTN_CHUNKS_PER_TRIP = 1
SCAN_KEY_HEADS = 8


def _cparams(sem):
    return pltpu.CompilerParams(dimension_semantics=sem, vmem_limit_bytes=V7X_VMEM_LIMIT)


def _const_spec(shape):
    nd = len(shape)
    return pl.BlockSpec(shape, lambda *_: (0,) * nd, pipeline_mode=pl.Buffered(1))


def _dot(a, b):
    return jnp.dot(a, b, preferred_element_type=F32)


def _dot_nt(a, b):
    return lax.dot_general(a, b, (((1,), (1,)), ((), ())), preferred_element_type=F32)


def _split_bf16(a):
    hi = a.astype(BF16)
    lo = (a - hi.astype(F32)).astype(BF16)
    return hi, lo


def _silu(x):
    return x * jax.nn.sigmoid(x)


def _layer_norm(x, g, b):
    mu = jnp.mean(x, axis=-1, keepdims=True)
    xc = x - mu
    var = jnp.mean(xc * xc, axis=-1, keepdims=True)
    return xc * lax.rsqrt(var + LN_EPS) * g + b


def _rms_norm(x, g):
    return x * lax.rsqrt(jnp.mean(x * x, axis=-1, keepdims=True) + RMS_EPS) * g


def _mod_kernel(c_ref, w_ref, b_ref, o_ref):
    a = _silu(c_ref[...]).astype(BF16)
    o_ref[0] = _dot(a, w_ref[0].astype(BF16)) + b_ref[0]


def _mod_call(cc, w_mod, b_mod):
    depth, d, n = w_mod.shape
    tn = 1536
    assert n % tn == 0
    return pl.pallas_call(
        _mod_kernel,
        grid=(depth, n // tn),
        in_specs=[
            pl.BlockSpec((8, d), lambda l, j: (0, 0)),
            pl.BlockSpec((1, d, tn), lambda l, j: (l, 0, j)),
            pl.BlockSpec((1, 1, tn), lambda l, j: (l, 0, j)),
        ],
        out_specs=pl.BlockSpec((1, 8, tn), lambda l, j: (l, 0, j)),
        out_shape=jax.ShapeDtypeStruct((depth, 8, n), F32),
        compiler_params=_cparams(("parallel", "parallel")),
        name="mod",
    )(cc, w_mod, b_mod.reshape(depth, 1, n))


def _mod_spec(d, j):
    return pl.BlockSpec((1, 1, d), lambda b, i: (b, 0, j))


VT_ROWS = MLA_V + 16


def _mla_proj_kernel(h_ref, sh_ref, sc_ref, win_ref, qn_ref, kvn_ref, wqt_ref, wk_ref, wvt_ref,
                     qmt_ref, km_ref, qt_ref, k_ref, vt_ref, *, heads, q_rank, kv_rank):
    x = h_ref[0]
    tm = x.shape[0]
    u = x * (1.0 + sc_ref[0]) + sh_ref[0]
    lat = _dot(u.astype(BF16), win_ref[...])
    cq = _rms_norm(lat[:, :q_rank], qn_ref[...]).astype(BF16)
    ckv = _rms_norm(lat[:, q_rank:q_rank + kv_rank], kvn_ref[...]).astype(BF16)
    kr2 = lat[:, q_rank + kv_rank:]
    qt_all = _dot_nt(wqt_ref[...], cq)
    k_all = _dot(ckv, wk_ref[...])
    vt_all = _dot_nt(wvt_ref[...], ckv)
    qmt = qmt_ref[...]
    t = kr2 * km_ref[...]
    krot2 = (t + pltpu.roll(t, MLA_ROPE, axis=1)).astype(BF16)
    ones_rows = jnp.where(lax.broadcasted_iota(jnp.int32, (VT_ROWS - MLA_V, tm), 0) == 0, 1.0, 0.0).astype(BF16)
    for hh in range(heads):
        qt_ref[0, hh] = (qt_all[hh * MLA_QK:(hh + 1) * MLA_QK, :] * qmt).astype(BF16)
        k_ref[0, hh, :, :MLA_NOPE] = k_all[:, hh * MLA_NOPE:(hh + 1) * MLA_NOPE].astype(BF16)
        k_ref[0, hh, :, MLA_NOPE:] = krot2
        vt_ref[0, hh, 0, :MLA_V, :] = vt_all[hh * MLA_V:(hh + 1) * MLA_V, :].astype(BF16)
        vt_ref[0, hh, 0, MLA_V:, :] = ones_rows


def _mla_proj_call(h, mod, w_in, qn, kvn, wqt, wk, wvt, qmt, km, *, tm, heads):
    b, t, d = h.shape
    q_rank, kv_rank = qn.shape[1], kvn.shape[1]
    assert t % tm == 0
    kern = functools.partial(_mla_proj_kernel, heads=heads, q_rank=q_rank, kv_rank=kv_rank)
    return pl.pallas_call(
        kern,
        grid=(b, t // tm),
        in_specs=[
            pl.BlockSpec((1, tm, d), lambda bb, i: (bb, i, 0)),
            _mod_spec(d, 0), _mod_spec(d, 1),
            _const_spec(w_in.shape), _const_spec(qn.shape), _const_spec(kvn.shape),
            _const_spec(wqt.shape), _const_spec(wk.shape), _const_spec(wvt.shape),
            pl.BlockSpec((MLA_QK, tm), lambda bb, i: (0, i)),
            pl.BlockSpec((tm, 2 * MLA_ROPE), lambda bb, i: (i, 0)),
        ],
        out_specs=[
            pl.BlockSpec((1, heads, MLA_QK, tm), lambda bb, i: (bb, 0, 0, i)),
            pl.BlockSpec((1, heads, tm, MLA_QK), lambda bb, i: (bb, 0, i, 0)),
            pl.BlockSpec((1, heads, 1, VT_ROWS, tm), lambda bb, i: (bb, 0, i, 0, 0)),
        ],
        out_shape=[
            jax.ShapeDtypeStruct((b, heads, MLA_QK, t), BF16),
            jax.ShapeDtypeStruct((b, heads, t, MLA_QK), BF16),
            jax.ShapeDtypeStruct((b, heads, t // tm, VT_ROWS, tm), BF16),
        ],
        compiler_params=_cparams(("parallel", "parallel")),
        name="mla_proj",
    )(h, mod, mod, w_in, qn, kvn, wqt, wk, wvt, qmt, km)


QGROUP = 256


def _attn_kernel(*refs, slabs_per_step, n_lat, n_groups, chunks_per_trip):
    if n_lat:
        qt_ref, kc_ref, vtc_ref, kl_ref, vtl_ref, o_ref, m_ref, acc_ref, st_ref = refs
    else:
        qt_ref, kc_ref, vtc_ref, o_ref, m_ref, acc_ref = refs

    def scores(kblk, g):
        return _dot(kblk, qt_ref[0, 0, :, g * QGROUP:(g + 1) * QGROUP])

    def softmax_pv(st, vt_slabs, g, first):
        slab = vt_slabs[0].shape[1]
        cols = slice(g * QGROUP, (g + 1) * QGROUP)
        cmax = jnp.max(st, axis=0, keepdims=True)
        if first:
            m_new = cmax
        else:
            m_old = m_ref[:, cols]
            m_new = jnp.maximum(m_old, cmax)
        pt = jnp.exp2(st - m_new).astype(BF16)
        pv = None
        for s, vt in enumerate(vt_slabs):
            term = _dot(vt, pt[s * slab:(s + 1) * slab, :])
            pv = term if pv is None else pv + term
        if first:
            acc_ref[:, cols] = pv
        else:
            acc_ref[:, cols] = acc_ref[:, cols] * jnp.exp2(m_old - m_new) + pv
        m_ref[:, cols] = m_new

    kc = kc_ref[0, 0]
    vtc = [vtc_ref[0, 0, s] for s in range(vtc_ref.shape[2])]
    sts = [scores(kc, g) for g in range(n_groups)]
    if n_lat:
        slab = vtl_ref.shape[4]
        tk = slabs_per_step * slab
        k0 = kl_ref[0, 0, pl.ds(0, tk), :]
        nxt = scores(k0, 0)
        for g in range(n_groups):
            cur, nxt = nxt, (scores(k0, g + 1) if g + 1 < n_groups else None)
            softmax_pv(sts[g], vtc, g, True)
            st_ref[g] = cur
    else:
        for g in range(n_groups):
            softmax_pv(sts[g], vtc, g, True)
    if n_lat:

        def chunk_step(j):
            off = pl.multiple_of(jnp.minimum(j + 1, n_lat - 1) * tk, tk)
            k_next = kl_ref[0, 0, pl.ds(off, tk), :]
            vts = [vtl_ref[0, 0, j * slabs_per_step + s] for s in range(slabs_per_step)]
            nxt = scores(k_next, 0)
            for g in range(n_groups):
                cur, nxt = nxt, (scores(k_next, g + 1) if g + 1 < n_groups else None)
                softmax_pv(st_ref[g], vts, g, False)
                st_ref[g] = cur

        def body(jj, carry):
            for u in range(chunks_per_trip):
                chunk_step(jj * chunks_per_trip + u)
            return carry
        lax.fori_loop(0, n_lat // chunks_per_trip, body, 0)
    acc = acc_ref[...]
    ot = acc[:MLA_V] * (1.0 / acc[MLA_V:MLA_V + 1])
    o_ref[0] = ot.T.astype(o_ref.dtype)


def _attn_call(qt, k_ctx, vt_ctx, k_lat=None, vt_lat=None, *, tq, slabs_per_step=ATTN_KEY_SLABS):
    b, h, dqk, t = qt.shape
    tc = k_ctx.shape[2]
    assert t % tq == 0 and tq % QGROUP == 0
    n_lat = 0
    in_specs = [
        pl.BlockSpec((1, 1, dqk, tq), lambda bb, hh, i: (bb, hh, 0, i)),
        pl.BlockSpec((1, 1, tc, dqk), lambda bb, hh, i: (bb, hh, 0, 0)),
        pl.BlockSpec((1, 1) + vt_ctx.shape[2:], lambda bb, hh, i: (bb, hh, 0, 0, 0)),
    ]
    args = [qt, k_ctx, vt_ctx]
    if k_lat is not None:
        tl = k_lat.shape[2]
        ns = vt_lat.shape[2]
        assert ns % slabs_per_step == 0
        n_lat = ns // slabs_per_step
        in_specs += [
            pl.BlockSpec((1, 1, tl, dqk), lambda bb, hh, i: (bb, hh, 0, 0)),
            pl.BlockSpec((1, 1) + vt_lat.shape[2:], lambda bb, hh, i: (bb, hh, 0, 0, 0)),
        ]
        args += [k_lat, vt_lat]
    n_groups = tq // QGROUP
    chunks_per_trip = ATTN_CHUNKS_PER_TRIP if n_lat % ATTN_CHUNKS_PER_TRIP == 0 else 1
    kern = functools.partial(_attn_kernel, slabs_per_step=slabs_per_step, n_lat=n_lat, n_groups=n_groups,
                             chunks_per_trip=chunks_per_trip)
    scratch = [pltpu.VMEM((1, tq), F32), pltpu.VMEM((VT_ROWS, tq), F32)]
    if n_lat:
        scratch.append(pltpu.VMEM((n_groups, slabs_per_step * vt_lat.shape[4], QGROUP), F32))
    return pl.pallas_call(
        kern,
        grid=(b, h, t // tq),
        in_specs=in_specs,
        out_specs=pl.BlockSpec((1, tq, MLA_V), lambda bb, hh, i: (bb, i, hh)),
        out_shape=jax.ShapeDtypeStruct((b, t, h * MLA_V), BF16),
        scratch_shapes=scratch,
        compiler_params=_cparams(("parallel", "parallel", "arbitrary")),
        name="attn_lat" if n_lat else "attn_ctx",
    )(*args)


POST_SPLIT = 2


def _post_tail(y_in_rows, h_ref, g1_ref, sh2_ref, sc2_ref, g2_ref, wout_ref, l1g_ref, l1b_ref,
               wfi_ref, wfo_ref, l2g_ref, l2b_ref, o_ref, *, alpha, d_ff):
    tm = h_ref.shape[1]
    rows_per = tm // POST_SPLIT

    def stages(r0):
        rows = slice(r0, r0 + rows_per)
        y_in = y_in_rows(rows)
        yield
        y = _dot(y_in, wout_ref[...])
        yield
        h1 = _layer_norm(alpha * h_ref[0, rows] + g1_ref[0] * y, l1g_ref[...], l1b_ref[...])
        u2 = (h1 * (1.0 + sc2_ref[0]) + sh2_ref[0]).astype(BF16)
        yield
        gu = _dot(u2, wfi_ref[...])
        yield
        act = (_silu(gu[:, :d_ff]) * gu[:, d_ff:]).astype(BF16)
        yield
        f = _dot(act, wfo_ref[...])
        yield
        o_ref[0, rows] = _layer_norm(alpha * h1 + g2_ref[0] * f, l2g_ref[...], l2b_ref[...])

    gens = [stages(r0) for r0 in range(0, tm, rows_per)]
    live = []
    while gens or live:
        if gens:
            live.append(gens.pop(0))
            for _ in range(2):
                next(live[-1], None)
        for g in list(live):
            if next(g, "done") == "done":
                live.remove(g)


def _post_mla_kernel(a_ref, *rest, alpha, d_ff):
    _post_tail(lambda rows: a_ref[0, rows], *rest, alpha=alpha, d_ff=d_ff)


def _post_gdn_kernel(of_ref, ob_ref, z_ref, nw_ref, *rest, alpha, d_ff, k_heads):
    nw = nw_ref[...]

    def y_in_rows(rows):
        parts = []
        for kh in range(k_heads):
            o2 = of_ref[0, kh, rows] + ob_ref[0, kh, rows]
            for v in range(2):
                o = o2[:, v * GDN_DV:(v + 1) * GDN_DV]
                vh = 2 * kh + v
                z = z_ref[0, rows, vh * GDN_DV:(vh + 1) * GDN_DV]
                parts.append((_rms_norm(o, nw) * _silu(z)).astype(BF16))
        return jnp.concatenate(parts, axis=1)

    _post_tail(y_in_rows, *rest, alpha=alpha, d_ff=d_ff)


def _post_call(mixer_in, h, mod, w_out, l1g, l1b, wfi, wfo, l2g, l2b, *, tm, alpha, gdn_norm=None):
    b, t, d = h.shape
    d_ff = wfo.shape[0]
    assert t % tm == 0
    row = lambda w: pl.BlockSpec((1, tm, w), lambda bb, i: (bb, i, 0))
    if gdn_norm is None:
        (a,) = mixer_in
        kern = functools.partial(_post_mla_kernel, alpha=alpha, d_ff=d_ff)
        head_specs = [row(a.shape[2])]
        head_args = [a]
    else:
        o_f, o_b, z = mixer_in
        kh = o_f.shape[1]
        kern = functools.partial(_post_gdn_kernel, alpha=alpha, d_ff=d_ff, k_heads=kh)
        hm = pl.BlockSpec((1, kh, tm, 2 * GDN_DV), lambda bb, i: (bb, 0, i, 0))
        head_specs = [hm, hm, row(z.shape[2]), _const_spec(gdn_norm.shape)]
        head_args = [o_f, o_b, z, gdn_norm]
    return pl.pallas_call(
        kern,
        grid=(b, t // tm),
        in_specs=head_specs + [
            row(d),
            _mod_spec(d, 2), _mod_spec(d, 3), _mod_spec(d, 4), _mod_spec(d, 5),
            _const_spec(w_out.shape), _const_spec(l1g.shape), _const_spec(l1b.shape),
            _const_spec(wfi.shape), _const_spec(wfo.shape),
            _const_spec(l2g.shape), _const_spec(l2b.shape),
        ],
        out_specs=row(d),
        out_shape=jax.ShapeDtypeStruct((b, t, d), F32),
        compiler_params=_cparams(("parallel", "parallel")),
        name="post_mla" if gdn_norm is None else "post_gdn",
    )(*head_args, h, mod, mod, mod, mod, w_out, l1g, l1b, wfi, wfo, l2g, l2b)


def _chunk_cumsum(g, suffix):
    n = g.shape[0]
    pos = lax.broadcasted_iota(jnp.int32, g.shape, 0) % CHUNK
    s = 1
    while s < CHUNK:
        if suffix:
            shifted = pltpu.roll(g, n - s, axis=0)
            ok = pos < CHUNK - s
        else:
            shifted = pltpu.roll(g, s, axis=0)
            ok = pos >= s
        g = g + jnp.where(ok, shifted, 0.0)
        s *= 2
    return g


CONV_ROWS = 64
CONV_COLS = 512


def _gdn_in_kernel(hp_ref, hc_ref, hn_ref, sh_ref, sc_ref, w_ref, cw_ref, al_ref, dtb_ref,
                   q_ref, k_ref, kt_ref, v_ref, gt_ref, gr_ref, z_ref, xe_ref, y_ref, *, k_heads, tm, z_w):
    i = pl.program_id(1)
    n_i = pl.num_programs(1)
    kw = k_heads * GDN_DK
    c_all = xe_ref.shape[1]
    n_chunks = tm // CHUNK
    half = GDN_CONV // 2
    h_ext = jnp.concatenate([hp_ref[0], hc_ref[0], hn_ref[0]], axis=0)
    u = (h_ext * (1.0 + sc_ref[0]) + sh_ref[0]).astype(BF16)
    keep_prev = jnp.where(i > 0, 1.0, 0.0)
    keep_next = jnp.where(i < n_i - 1, 1.0, 0.0)

    def project(c0):
        return _dot(u, w_ref[:, c0:c0 + CONV_COLS])

    def conv_tile(p, c0):
        cols = slice(c0, c0 + CONV_COLS)
        xe_ref[0:CONV_HALO, cols] = p[0:CONV_HALO] * keep_prev
        xe_ref[CONV_HALO:CONV_HALO + tm, cols] = p[CONV_HALO:CONV_HALO + tm]
        xe_ref[CONV_HALO + tm:, cols] = p[CONV_HALO + tm:] * keep_next
        w = cw_ref[:, cols]
        for r0 in range(0, tm, CONV_ROWS):
            acc = None
            for j in range(GDN_CONV):
                start = CONV_HALO - half + j + r0
                term = xe_ref[start:start + CONV_ROWS, cols] * w[j:j + 1, :]
                acc = term if acc is None else acc + term
            y_ref[r0:r0 + CONV_ROWS, cols] = _silu(acc)

    tiles = list(range(0, c_all, CONV_COLS))
    nxt = project(tiles[0])
    for t_idx, c0 in enumerate(tiles):
        cur = nxt
        if t_idx + 1 < len(tiles):
            nxt = project(tiles[t_idx + 1])
        else:
            nxt = _dot(u[CONV_HALO:CONV_HALO + tm], w_ref[:, c_all:])
        conv_tile(cur, c0)
    z_ref[0] = nxt[:, :z_w]
    ba = nxt[:, z_w:]
    for kh in range(k_heads):
        qh = y_ref[:, kh * GDN_DK:(kh + 1) * GDN_DK]
        q_ref[0, kh] = (qh * lax.rsqrt(jnp.sum(qh * qh, axis=-1, keepdims=True) + RMS_EPS)
                        * (GDN_DK ** -0.5)).astype(BF16)
        kk = y_ref[:, kw + kh * GDN_DK:kw + (kh + 1) * GDN_DK]
        kn = kk * lax.rsqrt(jnp.sum(kk * kk, axis=-1, keepdims=True) + RMS_EPS)
        k_ref[0, kh] = kn.astype(BF16)
        knt = kn.T.astype(BF16)
        for n in range(n_chunks):
            kt_ref[0, kh, n] = knt[:, n * CHUNK:(n + 1) * CHUNK]
        v_ref[0, kh] = y_ref[:, 2 * kw + kh * 2 * GDN_DV:2 * kw + (kh + 1) * 2 * GDN_DV].astype(BF16)
    lane = lax.broadcasted_iota(jnp.int32, ba.shape, 1) % LANES
    beta = jax.nn.sigmoid(ba)
    g = -jnp.exp(al_ref[...]) * jax.nn.softplus(ba + dtb_ref[...])
    g = jnp.where((lane >= 4) & (lane < 8), g, 0.0)
    gcum = jnp.where(lane < 6, _chunk_cumsum(g, False), _chunk_cumsum(g, True))
    gates = jnp.where(lane < 4, beta, gcum)
    for kh in range(k_heads):
        gk = gates[:, kh * LANES:(kh + 1) * LANES]
        gt_ref[0, kh] = gk
        rows = gk.T[0:8, :]
        for n in range(n_chunks):
            piece = rows[:, n * CHUNK:(n + 1) * CHUNK]
            gr_ref[0, kh, n] = jnp.concatenate([piece, piece], axis=1)


def _gdn_in_call(h, mod, w, conv_w, a_log_ext, dtb_ext, *, tm, k_heads, z_w):
    b, t, d = h.shape
    c = conv_w.shape[1]
    assert t % tm == 0 and tm % CHUNK == 0 and tm % CONV_HALO == 0
    assert tm % CONV_ROWS == 0 and c % CONV_COLS == 0
    r = tm // CONV_HALO
    nblk8 = t // CONV_HALO
    ncb = tm // CHUNK
    kern = functools.partial(_gdn_in_kernel, k_heads=k_heads, tm=tm, z_w=z_w)
    hm = lambda wd: pl.BlockSpec((1, k_heads, tm, wd), lambda bb, i: (bb, 0, i, 0))
    per_chunk = lambda rr, wd: pl.BlockSpec((1, k_heads, ncb, rr, wd), lambda bb, i: (bb, 0, i, 0, 0))
    return pl.pallas_call(
        kern,
        grid=(b, t // tm),
        in_specs=[
            pl.BlockSpec((1, CONV_HALO, d), lambda bb, i: (bb, jnp.maximum(i * r - 1, 0), 0)),
            pl.BlockSpec((1, tm, d), lambda bb, i: (bb, i, 0)),
            pl.BlockSpec((1, CONV_HALO, d), lambda bb, i: (bb, jnp.minimum((i + 1) * r, nblk8 - 1), 0)),
            _mod_spec(d, 0), _mod_spec(d, 1),
            _const_spec(w.shape), _const_spec(conv_w.shape),
            _const_spec(a_log_ext.shape), _const_spec(dtb_ext.shape),
        ],
        out_specs=[hm(GDN_DK), hm(GDN_DK), per_chunk(GDN_DK, CHUNK), hm(2 * GDN_DV), hm(LANES),
                   per_chunk(8, LANES), pl.BlockSpec((1, tm, z_w), lambda bb, i: (bb, i, 0))],
        out_shape=[
            jax.ShapeDtypeStruct((b, k_heads, t, GDN_DK), BF16),
            jax.ShapeDtypeStruct((b, k_heads, t, GDN_DK), BF16),
            jax.ShapeDtypeStruct((b, k_heads, t // CHUNK, GDN_DK, CHUNK), BF16),
            jax.ShapeDtypeStruct((b, k_heads, t, 2 * GDN_DV), BF16),
            jax.ShapeDtypeStruct((b, k_heads, t, LANES), F32),
            jax.ShapeDtypeStruct((b, k_heads, t // CHUNK, 8, LANES), F32),
            jax.ShapeDtypeStruct((b, t, z_w), F32),
        ],
        scratch_shapes=[pltpu.VMEM((tm + 2 * CONV_HALO, c), F32), pltpu.VMEM((tm, c), F32)],
        compiler_params=_cparams(("parallel", "parallel")),
        name="gdn_in",
    )(h, h, h, mod, mod, w, conv_w, a_log_ext, dtb_ext)


def _dot_split(a_hi, a_lo, b_hi, b_lo):
    return _dot(jnp.concatenate([a_hi, a_lo], axis=1), jnp.concatenate([b_hi, b_lo, b_hi, b_lo], axis=0))


def _gdn_scan_kernel(qf_ref, kf_ref, ktf_ref, vf_ref, gf_ref, grf_ref,
                     qb_ref, kb_ref, ktb_ref, vb_ref, gb_ref, grb_ref, s0_ref,
                     of_ref, ob_ref, sfin_ref, s_ref, *, hg):
    n = pl.program_id(2)

    @pl.when(n == 0)
    def _():
        s_ref[...] = s0_ref[0]

    c = CHUNK
    row = lax.broadcasted_iota(jnp.int32, (c, LANES), 0)
    col = lax.broadcasted_iota(jnp.int32, (c, LANES), 1) % c
    dirs = ((qf_ref, kf_ref, ktf_ref, vf_ref, gf_ref, grf_ref), (qb_ref, kb_ref, ktb_ref, vb_ref, gb_ref, grb_ref))

    eye = jnp.where(row == col, 1.0, 0.0)
    zero_rows = jnp.zeros((c, 2 * GDN_DV), BF16)
    tri_masks = ((row >= col, row > col), (row <= col, row < col))
    pairs = []
    blk_r, blk_c = row, col
    b = 1
    while b < c:
        pair = (blk_r // 2 == blk_c // 2) & (blk_r != blk_c)
        pairs.append(pair if b == 1 else jnp.where(pair, 1.0, 0.0).astype(BF16))
        blk_r, blk_c = blk_r // 2, blk_c // 2
        b *= 2

    def stages(heads):
        chains = []
        for j in heads:
            for d in range(2):
                q_ref, k_ref, kt_ref, v_ref, g_ref, gr_ref = dirs[d]
                qc, kc, v2, gt = q_ref[0, j], k_ref[0, j], v_ref[0, j], g_ref[0, j]
                gtt = gr_ref[0, j, 0]
                ktc = kt_ref[0, j, 0]
                kq = _dot_nt(jnp.concatenate([kc, qc], axis=0), jnp.concatenate([kc, kc], axis=0))
                for v in range(2):
                    chains.append(dict(j=j, d=d, v=v, qc=qc, kc=kc, vv=v2[:, v * GDN_DV:(v + 1) * GDN_DV],
                                       gt=gt, gtt=gtt, ktc=ktc, kk=kq[:c], qk=kq[c:]))
            yield
        for i, ch in enumerate(chains):
            slot = 2 * ch["d"] + ch["v"]
            incl, strict = tri_masks[ch["d"]]
            beta_col = jnp.broadcast_to(ch["gt"][:, slot:slot + 1], (c, LANES))
            g_col = jnp.broadcast_to(ch["gt"][:, 4 + slot:5 + slot], (c, LANES))
            g_row = ch["gtt"][4 + slot:5 + slot, :]
            decay = jnp.where(incl, jnp.exp(jnp.where(incl, g_col - g_row, 0.0)), 0.0)
            eg_col = jnp.exp(g_col)
            ch["g_row"] = g_row
            lmat = jnp.where(strict, ch["kk"] * beta_col * decay, 0.0)
            ch.update(
                lm=lmat.astype(BF16), tinv=eye - jnp.where(pairs[0], lmat, 0.0),
                attn=(ch["qk"] * decay)[:, :c].astype(BF16),
                eg_col=eg_col, g_col=g_col,
                rhs=jnp.concatenate([ch["vv"].astype(F32) * beta_col,
                                     ch["kc"].astype(F32) * (beta_col * eg_col)], axis=1).astype(BF16))
            if i % 4 == 3:
                yield
        for pair_bf in pairs[1:]:
            for ch in chains:
                ch["t_hi"], ch["t_lo"] = _split_bf16(ch["tinv"])
                ch["w"] = _dot(ch["lm"] * pair_bf, jnp.concatenate([ch["t_hi"], ch["t_lo"]], axis=0))
            yield
            for ch in chains:
                w_hi, w_lo = _split_bf16(ch["w"])
                ch["tinv"] = ch["tinv"] - _dot_split(ch["t_hi"], ch["t_lo"], w_hi, w_lo)
            yield
        for ch in chains:
            t_hi, t_lo = _split_bf16(ch["tinv"])
            r = ch["rhs"]
            ch["x"] = _dot(jnp.concatenate([t_hi, t_lo], axis=1),
                           jnp.concatenate([r, zero_rows, r, zero_rows], axis=0))
        yield
        for ch in chains:
            ch["s_idx"] = ch["j"] * 4 + 2 * ch["d"] + ch["v"]
            ch["s_old"] = s_ref[ch["s_idx"]]
            ch["wq"] = _dot(jnp.concatenate([ch["x"][:, GDN_DV:].astype(BF16), ch["qc"]], axis=0),
                            ch["s_old"].astype(BF16))
        yield
        for ch in chains:
            g_col = ch["g_col"]
            ch["g_last"] = g_col[0:1, :] if ch["d"] == 1 else g_col[c - 1:c, :]
            v_new = (ch["x"][:, :GDN_DV] - ch["wq"][:c]).astype(BF16)
            dec_row = jnp.exp(ch["g_last"] - ch["g_row"])[:, :c]
            k_dec_t = (ch["ktc"].astype(F32) * dec_row).astype(BF16)
            ch["av"] = _dot(ch["attn"], v_new)
            ch["kv"] = _dot(k_dec_t, v_new)
        yield
        outs = {}
        for ch in chains:
            s_ref[ch["s_idx"]] = ch["s_old"] * jnp.exp(ch["g_last"]) + ch["kv"]
            outs[ch["j"], ch["d"], ch["v"]] = ch["eg_col"] * ch["wq"][c:] + ch["av"]
        for j in heads:
            for d, o_ref in enumerate((of_ref, ob_ref)):
                o_ref[0, j] = jnp.concatenate([outs[j, d, 0], outs[j, d, 1]], axis=1)

    for _ in stages(list(range(hg))):
        pass

    @pl.when(n == pl.num_programs(2) - 1)
    def _():
        sfin_ref[0] = s_ref[...]


def _gdn_scan_call(q, k, kt, v, gates, gate_rows, s0, *, hg):
    b, kh, t, _ = q.shape
    assert kh % hg == 0 and t % CHUNK == 0
    nc = t // CHUNK
    ng = kh // hg
    fwd = lambda w: pl.BlockSpec((1, hg, CHUNK, w), lambda bb, g, n: (bb, g, n, 0))
    bwd = lambda w: pl.BlockSpec((1, hg, CHUNK, w), lambda bb, g, n: (bb, g, nc - 1 - n, 0))
    fwd_c = lambda r, w: pl.BlockSpec((1, hg, 1, r, w), lambda bb, g, n: (bb, g, n, 0, 0))
    bwd_c = lambda r, w: pl.BlockSpec((1, hg, 1, r, w), lambda bb, g, n: (bb, g, nc - 1 - n, 0, 0))
    st = pl.BlockSpec((1, hg * 4, GDN_DK, GDN_DV), lambda bb, g, n: (bb, g, 0, 0))
    kern = functools.partial(_gdn_scan_kernel, hg=hg)
    return pl.pallas_call(
        kern,
        grid=(b, ng, nc),
        in_specs=[fwd(GDN_DK), fwd(GDN_DK), fwd_c(GDN_DK, CHUNK), fwd(2 * GDN_DV), fwd(LANES), fwd_c(8, LANES),
                  bwd(GDN_DK), bwd(GDN_DK), bwd_c(GDN_DK, CHUNK), bwd(2 * GDN_DV), bwd(LANES), bwd_c(8, LANES),
                  st],
        out_specs=[fwd(2 * GDN_DV), bwd(2 * GDN_DV), st],
        out_shape=[
            jax.ShapeDtypeStruct((b, kh, t, 2 * GDN_DV), F32),
            jax.ShapeDtypeStruct((b, kh, t, 2 * GDN_DV), F32),
            jax.ShapeDtypeStruct(s0.shape, F32),
        ],
        scratch_shapes=[pltpu.VMEM((hg * 4, GDN_DK, GDN_DV), F32)],
        compiler_params=_cparams(("parallel", "parallel", "arbitrary")),
        name="gdn_scan",
    )(q, k, kt, v, gates, gate_rows, q, k, kt, v, gates, gate_rows, s0)


def _rope_tables(t, q_scale):
    assert t % GRID_W == 0
    rows = t // GRID_W
    n_freq = MLA_ROPE // 4
    inv_freq = ROPE_THETA ** (-(2.0 * jnp.arange(n_freq, dtype=F32)) / (MLA_ROPE // 2))
    ar = jnp.arange(rows, dtype=F32)[:, None] * inv_freq
    ac = jnp.arange(GRID_W, dtype=F32)[:, None] * inv_freq
    by_row = lambda x: jnp.broadcast_to(x[:, None, :], (rows, GRID_W, n_freq)).reshape(t, n_freq)
    by_col = lambda x: jnp.broadcast_to(x[None, :, :], (rows, GRID_W, n_freq)).reshape(t, n_freq)
    cr, sr, cc, sc = by_row(jnp.cos(ar)), by_row(jnp.sin(ar)), by_col(jnp.cos(ac)), by_col(jnp.sin(ac))
    cs = jnp.concatenate([cr, cr, cc, cc], axis=1)
    sn = jnp.concatenate([-sr, sr, -sc, sc], axis=1)
    km = jnp.concatenate([cs, sn], axis=1)
    qm = q_scale * jnp.concatenate([jnp.ones((t, MLA_NOPE), F32), cs, sn], axis=1)
    return qm, km


def _no_rope_tables(t, q_scale):
    one = jnp.ones((t, MLA_ROPE), F32)
    zero = jnp.zeros((t, MLA_ROPE), F32)
    km = jnp.concatenate([one, zero], axis=1)
    qm = q_scale * jnp.concatenate([jnp.ones((t, MLA_NOPE), F32), one, zero], axis=1)
    return qm, km


def _swap_perm():
    f = MLA_ROPE // 4
    return jnp.array(list(range(f, 2 * f)) + list(range(0, f)) + list(range(3 * f, 4 * f)) + list(range(2 * f, 3 * f)))


def _mla_layer(h_lat, h_ctx, mod_lat, mod_ctx, ln, w_in, q_norm, kv_norm, w_q_up, w_kv_up, w_out, *, alpha, last):
    bsz, seq, d = h_lat.shape
    tctx = h_ctx.shape[1]
    heads = d // 128
    q_rank, kv_rank = q_norm.shape[0], kv_norm.shape[0]
    tm_lat, tm_ctx = min(TOKEN_TILE, seq), min(TOKEN_TILE, tctx)
    q_scale = (MLA_NOPE + MLA_ROPE) ** -0.5 * math.log2(math.e)
    perm = _swap_perm()
    rope_cols = w_in[:, q_rank + kv_rank:]
    w_in_ext = jnp.concatenate([w_in, rope_cols[:, perm]], axis=1).astype(BF16)
    wq = w_q_up.reshape(q_rank, heads, MLA_NOPE + MLA_ROPE)
    wq_ext = jnp.concatenate([wq, wq[:, :, MLA_NOPE:][:, :, perm]], axis=2)
    wqt = wq_ext.reshape(q_rank, heads * MLA_QK).T.astype(BF16)
    wkv = w_kv_up.reshape(kv_rank, heads, MLA_NOPE + MLA_V)
    wk = wkv[:, :, :MLA_NOPE].reshape(kv_rank, -1).astype(BF16)
    wvt = wkv[:, :, MLA_NOPE:].reshape(kv_rank, -1).T.astype(BF16)
    qn, kvn = q_norm.reshape(1, -1), kv_norm.reshape(1, -1)
    qm_l, km_l = _rope_tables(seq, q_scale)
    qm_c, km_c = _no_rope_tables(tctx, q_scale)
    q_l, k_l, v_l = _mla_proj_call(h_lat, mod_lat, w_in_ext, qn, kvn, wqt, wk, wvt, qm_l.T, km_l,
                                   tm=tm_lat, heads=heads)
    q_c, k_c, v_c = _mla_proj_call(h_ctx, mod_ctx, w_in_ext, qn, kvn, wqt, wk, wvt, qm_c.T, km_c,
                                   tm=tm_ctx, heads=heads)
    o_l = _attn_call(q_l, k_c, v_c, k_l, v_l, tq=min(ATTN_Q_TILE, seq))
    w_out = w_out.astype(BF16)
    h_lat_new = _post_call((o_l,), h_lat, mod_lat, w_out, *ln, tm=tm_lat, alpha=alpha)
    if not last:
        o_c = _attn_call(q_c, k_c, v_c, tq=tm_ctx)
        h_ctx = _post_call((o_c,), h_ctx, mod_ctx, w_out, *ln, tm=tm_ctx, alpha=alpha)
    return h_lat_new, h_ctx


def _gdn_layer(h_lat, h_ctx, mod_lat, mod_ctx, ln, w_in, conv_w, a_log, dt_bias, norm_w, w_out, *, alpha, last):
    bsz, seq, d = h_lat.shape
    tctx = h_ctx.shape[1]
    k_heads = d // 128
    v_heads = 2 * k_heads
    kw, vw = k_heads * GDN_DK, v_heads * GDN_DV
    qkv_w = 2 * kw + vw
    tm_lat, tm_ctx = min(TOKEN_TILE, seq), min(TOKEN_TILE, tctx)
    w_ba = w_in[:, qkv_w + vw:].reshape(d, 2, 2, k_heads, 2)
    w_ba = jnp.transpose(w_ba, (0, 3, 2, 1, 4)).reshape(d, k_heads, 8)
    w_ba = jnp.pad(w_ba, ((0, 0), (0, 0), (0, LANES - 8))).reshape(d, k_heads * LANES)
    w_ext = jnp.concatenate([w_in[:, :qkv_w + vw].astype(BF16), w_ba.astype(BF16)], axis=1)

    def gate_lanes(p):
        p = jnp.transpose(p.astype(F32).reshape(2, k_heads, 2), (1, 0, 2)).reshape(k_heads, 4)
        p = jnp.pad(p, ((0, 0), (4, LANES - 8)))
        return p.reshape(1, k_heads * LANES)

    al_ext, dtb_ext = gate_lanes(a_log), gate_lanes(dt_bias)
    s = jnp.zeros((bsz, k_heads * 4, GDN_DK, GDN_DV), F32)
    streams = []
    for h_in, m_in, tm in ((h_ctx, mod_ctx, tm_ctx), (h_lat, mod_lat, tm_lat)):
        qh, kh, kt, vh, gates, gate_rows, z = _gdn_in_call(h_in, m_in, w_ext, conv_w, al_ext, dtb_ext,
                                                           tm=tm, k_heads=k_heads, z_w=vw)
        o_f, o_b, s = _gdn_scan_call(qh, kh, kt, vh, gates, gate_rows, s, hg=min(SCAN_KEY_HEADS, k_heads))
        streams.append((o_f, o_b, z))
    w_out = w_out.astype(BF16)
    nw = norm_w.reshape(1, -1)
    h_lat_new = _post_call(streams[1], h_lat, mod_lat, w_out, *ln, tm=tm_lat, alpha=alpha, gdn_norm=nw)
    if not last:
        h_ctx = _post_call(streams[0], h_ctx, mod_ctx, w_out, *ln, tm=tm_ctx, alpha=alpha, gdn_norm=nw)
    return h_lat_new, h_ctx


def kernel(x, c, ctx, c_ctx, w_mod, b_mod, ln1_g, ln1_b, ln2_g, ln2_b, w_ffn_in, w_ffn_out, mla_w_in, mla_q_norm, mla_kv_norm, mla_w_q_up, mla_w_kv_up, mla_w_out, gdn_w_in, gdn_conv, gdn_a_log, gdn_dt_bias, gdn_norm, gdn_w_out):
    bsz, _, d = x.shape
    depth = w_mod.shape[0]
    alpha = (2.0 * depth) ** 0.25

    cc = jnp.zeros((8, d), F32).at[:bsz].set(c).at[bsz].set(c_ctx)
    mod = _mod_call(cc, w_mod, b_mod)

    row = lambda a, i: a[i].reshape(1, -1)
    h_lat, h_ctx = x, ctx
    for i in range(depth):
        last = i == depth - 1
        mod_lat = mod[i, :bsz].reshape(bsz, 1, N_MOD * d)
        mod_ctx = jnp.broadcast_to(mod[i, bsz].reshape(1, 1, N_MOD * d), (bsz, 1, N_MOD * d))
        ln = (row(ln1_g, i), row(ln1_b, i), w_ffn_in[i].astype(BF16), w_ffn_out[i].astype(BF16),
              row(ln2_g, i), row(ln2_b, i))
        j = i // 2
        if i % 2 == 0:
            h_lat, h_ctx = _mla_layer(h_lat, h_ctx, mod_lat, mod_ctx, ln, mla_w_in[j], mla_q_norm[j],
                                      mla_kv_norm[j], mla_w_q_up[j], mla_w_kv_up[j], mla_w_out[j],
                                      alpha=alpha, last=last)
        else:
            h_lat, h_ctx = _gdn_layer(h_lat, h_ctx, mod_lat, mod_ctx, ln, gdn_w_in[j], gdn_conv[j],
                                      gdn_a_log[j], gdn_dt_bias[j], gdn_norm[j], gdn_w_out[j],
                                      alpha=alpha, last=last)
    return h_lat
```

```python
import functools
import math

import jax
import jax.numpy as jnp
from jax import lax
from jax.experimental import pallas as pl
from jax.experimental.pallas import tpu as pltpu

F32 = jnp.float32
BF16 = jnp.bfloat16

N_MOD = 6
LN_EPS = 1e-5
RMS_EPS = 1e-6
GRID_W = 64
ROPE_THETA = 10000.0

MLA_NOPE = 128
MLA_ROPE = 64
MLA_V = 128
MLA_QK = MLA_NOPE + 2 * MLA_ROPE

GDN_DK = 128
GDN_DV = 128
GDN_CONV = 5
CHUNK = 64
CONV_HALO = 8

V7X_VMEM_LIMIT = 56 * 1024 * 1024
LANES = 128

TOKEN_TILE = 256
ATTN_Q_TILE = 4096
ATTN_KEY_SLABS = 4
ATTN_CHUNKS_PER_TRIP = 1
SCAN_KEY_HEADS = 8


def _cparams(sem):
    return pltpu.CompilerParams(dimension_semantics=sem, vmem_limit_bytes=V7X_VMEM_LIMIT)


def _const_spec(shape):
    nd = len(shape)
    return pl.BlockSpec(shape, lambda *_: (0,) * nd, pipeline_mode=pl.Buffered(1))


def _dot(a, b):
    return jnp.dot(a, b, preferred_element_type=F32)


def _dot_nt(a, b):
    return lax.dot_general(a, b, (((1,), (1,)), ((), ())), preferred_element_type=F32)


def _split_bf16(a):
    hi = a.astype(BF16)
    lo = (a - hi.astype(F32)).astype(BF16)
    return hi, lo


def _silu(x):
    return x * jax.nn.sigmoid(x)


def _layer_norm(x, g, b):
    mu = jnp.mean(x, axis=-1, keepdims=True)
    xc = x - mu
    var = jnp.mean(xc * xc, axis=-1, keepdims=True)
    return xc * lax.rsqrt(var + LN_EPS) * g + b


def _rms_norm(x, g):
    return x * lax.rsqrt(jnp.mean(x * x, axis=-1, keepdims=True) + RMS_EPS) * g


def _mod_kernel(c_ref, w_ref, b_ref, o_ref):
    a = _silu(c_ref[...]).astype(BF16)
    o_ref[0] = _dot(a, w_ref[0].astype(BF16)) + b_ref[0]


def _mod_call(cc, w_mod, b_mod):
    depth, d, n = w_mod.shape
    tn = 1536
    assert n % tn == 0
    return pl.pallas_call(
        _mod_kernel,
        grid=(depth, n // tn),
        in_specs=[
            pl.BlockSpec((8, d), lambda l, j: (0, 0)),
            pl.BlockSpec((1, d, tn), lambda l, j: (l, 0, j)),
            pl.BlockSpec((1, 1, tn), lambda l, j: (l, 0, j)),
        ],
        out_specs=pl.BlockSpec((1, 8, tn), lambda l, j: (l, 0, j)),
        out_shape=jax.ShapeDtypeStruct((depth, 8, n), F32),
        compiler_params=_cparams(("parallel", "parallel")),
        name="mod",
    )(cc, w_mod, b_mod.reshape(depth, 1, n))


def _mod_spec(d, j):
    return pl.BlockSpec((1, 1, d), lambda b, i: (b, 0, j))


VT_ROWS = MLA_V + 16


def _mla_proj_kernel(h_ref, sh_ref, sc_ref, win_ref, qn_ref, kvn_ref, wqt_ref, wk_ref, wvt_ref,
                     qrt_ref, km_ref, qt_ref, k_ref, vt_ref, *, heads, q_rank, kv_rank, q_scale):
    x = h_ref[0]
    tm = x.shape[0]
    u = x * (1.0 + sc_ref[0]) + sh_ref[0]
    lat = _dot(u.astype(BF16), win_ref[...])
    cq = _rms_norm(lat[:, :q_rank], qn_ref[...]).astype(BF16)
    ckv = _rms_norm(lat[:, q_rank:q_rank + kv_rank], kvn_ref[...]).astype(BF16)
    kr2 = lat[:, q_rank + kv_rank:]
    qt_all = _dot_nt(wqt_ref[...], cq)
    k_all = _dot(ckv, wk_ref[...])
    vt_all = _dot_nt(wvt_ref[...], ckv)
    qrt = qrt_ref[...]
    t = kr2 * km_ref[...]
    krot2 = (t + pltpu.roll(t, MLA_ROPE, axis=1)).astype(BF16)
    ones_rows = jnp.where(lax.broadcasted_iota(jnp.int32, (VT_ROWS - MLA_V, tm), 0) == 0, 1.0, 0.0).astype(BF16)
    for hh in range(heads):
        qh = qt_all[hh * MLA_QK:(hh + 1) * MLA_QK, :]
        qt_ref[0, hh, :MLA_NOPE, :] = (qh[:MLA_NOPE] * q_scale).astype(BF16)
        qt_ref[0, hh, MLA_NOPE:, :] = (qh[MLA_NOPE:] * qrt).astype(BF16)
        k_ref[0, hh, :, :MLA_NOPE] = k_all[:, hh * MLA_NOPE:(hh + 1) * MLA_NOPE].astype(BF16)
        k_ref[0, hh, :, MLA_NOPE:] = krot2
        vt_ref[0, hh, 0, :MLA_V, :] = vt_all[hh * MLA_V:(hh + 1) * MLA_V, :].astype(BF16)
        vt_ref[0, hh, 0, MLA_V:, :] = ones_rows


def _mla_proj_call(h, mod, w_in, qn, kvn, wqt, wk, wvt, qrt, km, *, tm, heads, q_scale):
    b, t, d = h.shape
    q_rank, kv_rank = qn.shape[1], kvn.shape[1]
    assert t % tm == 0
    kern = functools.partial(_mla_proj_kernel, heads=heads, q_rank=q_rank, kv_rank=kv_rank, q_scale=q_scale)
    return pl.pallas_call(
        kern,
        grid=(b, t // tm),
        in_specs=[
            pl.BlockSpec((1, tm, d), lambda bb, i: (bb, i, 0)),
            _mod_spec(d, 0), _mod_spec(d, 1),
            _const_spec(w_in.shape), _const_spec(qn.shape), _const_spec(kvn.shape),
            _const_spec(wqt.shape), _const_spec(wk.shape), _const_spec(wvt.shape),
            pl.BlockSpec((2 * MLA_ROPE, tm), lambda bb, i: (0, i)),
            pl.BlockSpec((tm, 2 * MLA_ROPE), lambda bb, i: (i, 0)),
        ],
        out_specs=[
            pl.BlockSpec((1, heads, MLA_QK, tm), lambda bb, i: (bb, 0, 0, i)),
            pl.BlockSpec((1, heads, tm, MLA_QK), lambda bb, i: (bb, 0, i, 0)),
            pl.BlockSpec((1, heads, 1, VT_ROWS, tm), lambda bb, i: (bb, 0, i, 0, 0)),
        ],
        out_shape=[
            jax.ShapeDtypeStruct((b, heads, MLA_QK, t), BF16),
            jax.ShapeDtypeStruct((b, heads, t, MLA_QK), BF16),
            jax.ShapeDtypeStruct((b, heads, t // tm, VT_ROWS, tm), BF16),
        ],
        compiler_params=_cparams(("parallel", "parallel")),
        name="mla_proj",
    )(h, mod, mod, w_in, qn, kvn, wqt, wk, wvt, qrt, km)


QGROUP = 256


def _attn_kernel(*refs, slabs_per_step, n_lat, n_groups, chunks_per_trip):
    if n_lat:
        qt_ref, kc_ref, vtc_ref, kl_ref, vtl_ref, o_ref, m_ref, acc_ref, st_ref = refs
    else:
        qt_ref, kc_ref, vtc_ref, o_ref, m_ref, acc_ref = refs

    def scores(kblk, g):
        return _dot(kblk, qt_ref[0, 0, :, g * QGROUP:(g + 1) * QGROUP])

    def softmax_pv(st, vt_slabs, g, first):
        slab = vt_slabs[0].shape[1]
        cols = slice(g * QGROUP, (g + 1) * QGROUP)
        cmax = jnp.max(st, axis=0, keepdims=True)
        if first:
            m_new = cmax
        else:
            m_old = m_ref[:, cols]
            m_new = jnp.maximum(m_old, cmax)
        pt = jnp.exp2(st - m_new).astype(BF16)
        pv = None
        for s, vt in enumerate(vt_slabs):
            term = _dot(vt, pt[s * slab:(s + 1) * slab, :])
            pv = term if pv is None else pv + term
        if first:
            acc_ref[:, cols] = pv
        else:
            acc_ref[:, cols] = acc_ref[:, cols] * jnp.exp2(m_old - m_new) + pv
        m_ref[:, cols] = m_new

    kc = kc_ref[0, 0]
    vtc = [vtc_ref[0, 0, s] for s in range(vtc_ref.shape[2])]
    sts = [scores(kc, g) for g in range(n_groups)]
    if n_lat:
        slab = vtl_ref.shape[4]
        tk = slabs_per_step * slab
        k0 = kl_ref[0, 0, pl.ds(0, tk), :]
        nxt = scores(k0, 0)
        for g in range(n_groups):
            cur, nxt = nxt, (scores(k0, g + 1) if g + 1 < n_groups else None)
            softmax_pv(sts[g], vtc, g, True)
            st_ref[g] = cur
    else:
        for g in range(n_groups):
            softmax_pv(sts[g], vtc, g, True)
    if n_lat:

        def chunk_step(j):
            off = pl.multiple_of(jnp.minimum(j + 1, n_lat - 1) * tk, tk)
            k_next = kl_ref[0, 0, pl.ds(off, tk), :]
            vts = [vtl_ref[0, 0, j * slabs_per_step + s] for s in range(slabs_per_step)]
            nxt = scores(k_next, 0)
            for g in range(n_groups):
                cur, nxt = nxt, (scores(k_next, g + 1) if g + 1 < n_groups else None)
                softmax_pv(st_ref[g], vts, g, False)
                st_ref[g] = cur

        def body(jj, carry):
            for u in range(chunks_per_trip):
                chunk_step(jj * chunks_per_trip + u)
            return carry
        lax.fori_loop(0, n_lat // chunks_per_trip, body, 0)
    acc = acc_ref[...]
    ot = acc[:MLA_V] * (1.0 / acc[MLA_V:MLA_V + 1])
    o_ref[0] = ot.T.astype(o_ref.dtype)


def _attn_call(qt, k_ctx, vt_ctx, k_lat=None, vt_lat=None, *, tq, slabs_per_step=ATTN_KEY_SLABS):
    b, h, dqk, t = qt.shape
    tc = k_ctx.shape[2]
    assert t % tq == 0 and tq % QGROUP == 0
    n_lat = 0
    in_specs = [
        pl.BlockSpec((1, 1, dqk, tq), lambda bb, hh, i: (bb, hh, 0, i)),
        pl.BlockSpec((1, 1, tc, dqk), lambda bb, hh, i: (bb, hh, 0, 0)),
        pl.BlockSpec((1, 1) + vt_ctx.shape[2:], lambda bb, hh, i: (bb, hh, 0, 0, 0)),
    ]
    args = [qt, k_ctx, vt_ctx]
    if k_lat is not None:
        tl = k_lat.shape[2]
        ns = vt_lat.shape[2]
        assert ns % slabs_per_step == 0
        n_lat = ns // slabs_per_step
        in_specs += [
            pl.BlockSpec((1, 1, tl, dqk), lambda bb, hh, i: (bb, hh, 0, 0)),
            pl.BlockSpec((1, 1) + vt_lat.shape[2:], lambda bb, hh, i: (bb, hh, 0, 0, 0)),
        ]
        args += [k_lat, vt_lat]
    n_groups = tq // QGROUP
    chunks_per_trip = ATTN_CHUNKS_PER_TRIP if n_lat % ATTN_CHUNKS_PER_TRIP == 0 else 1
    kern = functools.partial(_attn_kernel, slabs_per_step=slabs_per_step, n_lat=n_lat, n_groups=n_groups,
                             chunks_per_trip=chunks_per_trip)
    scratch = [pltpu.VMEM((1, tq), F32), pltpu.VMEM((VT_ROWS, tq), F32)]
    if n_lat:
        scratch.append(pltpu.VMEM((n_groups, slabs_per_step * vt_lat.shape[4], QGROUP), F32))
    return pl.pallas_call(
        kern,
        grid=(b, h, t // tq),
        in_specs=in_specs,
        out_specs=pl.BlockSpec((1, tq, MLA_V), lambda bb, hh, i: (bb, i, hh)),
        out_shape=jax.ShapeDtypeStruct((b, t, h * MLA_V), BF16),
        scratch_shapes=scratch,
        compiler_params=_cparams(("parallel", "parallel", "arbitrary")),
        name="attn_lat" if n_lat else "attn_ctx",
    )(*args)


POST_SPLIT = 2


def _post_tail(y_in_rows, h_ref, g1_ref, sh2_ref, sc2_ref, g2_ref, wout_ref, l1g_ref, l1b_ref,
               wfi_ref, wfo_ref, l2g_ref, l2b_ref, o_ref, *, alpha, d_ff):
    tm = h_ref.shape[1]
    rows_per = tm // POST_SPLIT

    def stages(r0):
        rows = slice(r0, r0 + rows_per)
        y_in = y_in_rows(rows)
        yield
        y = _dot(y_in, wout_ref[...])
        yield
        h1 = _layer_norm(alpha * h_ref[0, rows] + g1_ref[0] * y, l1g_ref[...], l1b_ref[...])
        u2 = (h1 * (1.0 + sc2_ref[0]) + sh2_ref[0]).astype(BF16)
        yield
        gu = _dot(u2, wfi_ref[...])
        yield
        act = (_silu(gu[:, :d_ff]) * gu[:, d_ff:]).astype(BF16)
        yield
        f = _dot(act, wfo_ref[...])
        yield
        o_ref[0, rows] = _layer_norm(alpha * h1 + g2_ref[0] * f, l2g_ref[...], l2b_ref[...])

    gens = [stages(r0) for r0 in range(0, tm, rows_per)]
    live = []
    while gens or live:
        if gens:
            live.append(gens.pop(0))
            for _ in range(2):
                next(live[-1], None)
        for g in list(live):
            if next(g, "done") == "done":
                live.remove(g)


def _post_mla_kernel(a_ref, *rest, alpha, d_ff):
    _post_tail(lambda rows: a_ref[0, rows], *rest, alpha=alpha, d_ff=d_ff)


def _post_gdn_kernel(of_ref, ob_ref, z_ref, nw_ref, *rest, alpha, d_ff, k_heads):
    nw = nw_ref[...]

    def y_in_rows(rows):
        parts = []
        for kh in range(k_heads):
            o2 = of_ref[0, kh, rows] + ob_ref[0, kh, rows]
            for v in range(2):
                o = o2[:, v * GDN_DV:(v + 1) * GDN_DV]
                vh = 2 * kh + v
                z = z_ref[0, rows, vh * GDN_DV:(vh + 1) * GDN_DV]
                parts.append((_rms_norm(o, nw) * _silu(z)).astype(BF16))
        return jnp.concatenate(parts, axis=1)

    _post_tail(y_in_rows, *rest, alpha=alpha, d_ff=d_ff)


def _post_call(mixer_in, h, mod, w_out, l1g, l1b, wfi, wfo, l2g, l2b, *, tm, alpha, gdn_norm=None):
    b, t, d = h.shape
    d_ff = wfo.shape[0]
    assert t % tm == 0
    row = lambda w: pl.BlockSpec((1, tm, w), lambda bb, i: (bb, i, 0))
    if gdn_norm is None:
        (a,) = mixer_in
        kern = functools.partial(_post_mla_kernel, alpha=alpha, d_ff=d_ff)
        head_specs = [row(a.shape[2])]
        head_args = [a]
    else:
        o_f, o_b, z = mixer_in
        kh = o_f.shape[1]
        kern = functools.partial(_post_gdn_kernel, alpha=alpha, d_ff=d_ff, k_heads=kh)
        hm = pl.BlockSpec((1, kh, tm, 2 * GDN_DV), lambda bb, i: (bb, 0, i, 0))
        head_specs = [hm, hm, row(z.shape[2]), _const_spec(gdn_norm.shape)]
        head_args = [o_f, o_b, z, gdn_norm]
    return pl.pallas_call(
        kern,
        grid=(b, t // tm),
        in_specs=head_specs + [
            row(d),
            _mod_spec(d, 2), _mod_spec(d, 3), _mod_spec(d, 4), _mod_spec(d, 5),
            _const_spec(w_out.shape), _const_spec(l1g.shape), _const_spec(l1b.shape),
            _const_spec(wfi.shape), _const_spec(wfo.shape),
            _const_spec(l2g.shape), _const_spec(l2b.shape),
        ],
        out_specs=row(d),
        out_shape=jax.ShapeDtypeStruct((b, t, d), F32),
        compiler_params=_cparams(("parallel", "parallel")),
        name="post_mla" if gdn_norm is None else "post_gdn",
    )(*head_args, h, mod, mod, mod, mod, w_out, l1g, l1b, wfi, wfo, l2g, l2b)


def _chunk_cumsum(g, suffix):
    n = g.shape[0]
    pos = lax.broadcasted_iota(jnp.int32, g.shape, 0) % CHUNK
    s = 1
    while s < CHUNK:
        if suffix:
            shifted = pltpu.roll(g, n - s, axis=0)
            ok = pos < CHUNK - s
        else:
            shifted = pltpu.roll(g, s, axis=0)
            ok = pos >= s
        g = g + jnp.where(ok, shifted, 0.0)
        s *= 2
    return g


CONV_ROWS = 64
CONV_COLS = 512


def _gdn_in_kernel(hp_ref, hc_ref, hn_ref, sh_ref, sc_ref, w_ref, cw_ref, al_ref, dtb_ref,
                   q_ref, k_ref, kt_ref, v_ref, gt_ref, gr_ref, z_ref, xe_ref, y_ref, *, k_heads, tm, z_w):
    i = pl.program_id(1)
    n_i = pl.num_programs(1)
    kw = k_heads * GDN_DK
    c_all = xe_ref.shape[1]
    n_chunks = tm // CHUNK
    half = GDN_CONV // 2
    h_ext = jnp.concatenate([hp_ref[0], hc_ref[0], hn_ref[0]], axis=0)
    u = (h_ext * (1.0 + sc_ref[0]) + sh_ref[0]).astype(BF16)
    keep_prev = jnp.where(i > 0, 1.0, 0.0)
    keep_next = jnp.where(i < n_i - 1, 1.0, 0.0)

    def project(c0):
        return _dot(u, w_ref[:, c0:c0 + CONV_COLS])

    def conv_tile(p, c0):
        cols = slice(c0, c0 + CONV_COLS)
        xe_ref[0:CONV_HALO, cols] = p[0:CONV_HALO] * keep_prev
        xe_ref[CONV_HALO:CONV_HALO + tm, cols] = p[CONV_HALO:CONV_HALO + tm]
        xe_ref[CONV_HALO + tm:, cols] = p[CONV_HALO + tm:] * keep_next
        w = cw_ref[:, cols]
        for r0 in range(0, tm, CONV_ROWS):
            acc = None
            for j in range(GDN_CONV):
                start = CONV_HALO - half + j + r0
                term = xe_ref[start:start + CONV_ROWS, cols] * w[j:j + 1, :]
                acc = term if acc is None else acc + term
            y_ref[r0:r0 + CONV_ROWS, cols] = _silu(acc)

    tiles = list(range(0, c_all, CONV_COLS))
    nxt = project(tiles[0])
    for t_idx, c0 in enumerate(tiles):
        cur = nxt
        if t_idx + 1 < len(tiles):
            nxt = project(tiles[t_idx + 1])
        else:
            nxt = _dot(u[CONV_HALO:CONV_HALO + tm], w_ref[:, c_all:])
        conv_tile(cur, c0)
    z_ref[0] = nxt[:, :z_w]
    ba = nxt[:, z_w:]
    for kh in range(k_heads):
        qh = y_ref[:, kh * GDN_DK:(kh + 1) * GDN_DK]
        q_ref[0, kh] = (qh * lax.rsqrt(jnp.sum(qh * qh, axis=-1, keepdims=True) + RMS_EPS)
                        * (GDN_DK ** -0.5)).astype(BF16)
        kk = y_ref[:, kw + kh * GDN_DK:kw + (kh + 1) * GDN_DK]
        kn = kk * lax.rsqrt(jnp.sum(kk * kk, axis=-1, keepdims=True) + RMS_EPS)
        k_ref[0, kh] = kn.astype(BF16)
        knt = kn.T.astype(BF16)
        for n in range(n_chunks):
            kt_ref[0, kh, n] = knt[:, n * CHUNK:(n + 1) * CHUNK]
        v_ref[0, kh] = y_ref[:, 2 * kw + kh * 2 * GDN_DV:2 * kw + (kh + 1) * 2 * GDN_DV].astype(BF16)
    lane = lax.broadcasted_iota(jnp.int32, ba.shape, 1) % LANES
    beta = jax.nn.sigmoid(ba)
    g = -jnp.exp(al_ref[...]) * jax.nn.softplus(ba + dtb_ref[...])
    g = jnp.where((lane >= 4) & (lane < 8), g, 0.0)
    gcum = jnp.where(lane < 6, _chunk_cumsum(g, False), _chunk_cumsum(g, True))
    gates = jnp.where(lane < 4, beta, gcum)
    for kh in range(k_heads):
        gk = gates[:, kh * LANES:(kh + 1) * LANES]
        gt_ref[0, kh] = gk
        rows = gk.T[0:8, :]
        for n in range(n_chunks):
            piece = rows[:, n * CHUNK:(n + 1) * CHUNK]
            gr_ref[0, kh, n] = jnp.concatenate([piece, piece], axis=1)


def _gdn_in_call(h, mod, w, conv_w, a_log_ext, dtb_ext, *, tm, k_heads, z_w):
    b, t, d = h.shape
    c = conv_w.shape[1]
    assert t % tm == 0 and tm % CHUNK == 0 and tm % CONV_HALO == 0
    assert tm % CONV_ROWS == 0 and c % CONV_COLS == 0
    r = tm // CONV_HALO
    nblk8 = t // CONV_HALO
    ncb = tm // CHUNK
    kern = functools.partial(_gdn_in_kernel, k_heads=k_heads, tm=tm, z_w=z_w)
    hm = lambda wd: pl.BlockSpec((1, k_heads, tm, wd), lambda bb, i: (bb, 0, i, 0))
    per_chunk = lambda rr, wd: pl.BlockSpec((1, k_heads, ncb, rr, wd), lambda bb, i: (bb, 0, i, 0, 0))
    return pl.pallas_call(
        kern,
        grid=(b, t // tm),
        in_specs=[
            pl.BlockSpec((1, CONV_HALO, d), lambda bb, i: (bb, jnp.maximum(i * r - 1, 0), 0)),
            pl.BlockSpec((1, tm, d), lambda bb, i: (bb, i, 0)),
            pl.BlockSpec((1, CONV_HALO, d), lambda bb, i: (bb, jnp.minimum((i + 1) * r, nblk8 - 1), 0)),
            _mod_spec(d, 0), _mod_spec(d, 1),
            _const_spec(w.shape), _const_spec(conv_w.shape),
            _const_spec(a_log_ext.shape), _const_spec(dtb_ext.shape),
        ],
        out_specs=[hm(GDN_DK), hm(GDN_DK), per_chunk(GDN_DK, CHUNK), hm(2 * GDN_DV), hm(LANES),
                   per_chunk(8, LANES), pl.BlockSpec((1, tm, z_w), lambda bb, i: (bb, i, 0))],
        out_shape=[
            jax.ShapeDtypeStruct((b, k_heads, t, GDN_DK), BF16),
            jax.ShapeDtypeStruct((b, k_heads, t, GDN_DK), BF16),
            jax.ShapeDtypeStruct((b, k_heads, t // CHUNK, GDN_DK, CHUNK), BF16),
            jax.ShapeDtypeStruct((b, k_heads, t, 2 * GDN_DV), BF16),
            jax.ShapeDtypeStruct((b, k_heads, t, LANES), F32),
            jax.ShapeDtypeStruct((b, k_heads, t // CHUNK, 8, LANES), F32),
            jax.ShapeDtypeStruct((b, t, z_w), F32),
        ],
        scratch_shapes=[pltpu.VMEM((tm + 2 * CONV_HALO, c), F32), pltpu.VMEM((tm, c), F32)],
        compiler_params=_cparams(("parallel", "parallel")),
        name="gdn_in",
    )(h, h, h, mod, mod, w, conv_w, a_log_ext, dtb_ext)


def _dot_split(a_hi, a_lo, b_hi, b_lo):
    return _dot(jnp.concatenate([a_hi, a_lo], axis=1), jnp.concatenate([b_hi, b_lo, b_hi, b_lo], axis=0))


def _gdn_scan_kernel(qf_ref, kf_ref, ktf_ref, vf_ref, gf_ref, grf_ref,
                     qb_ref, kb_ref, ktb_ref, vb_ref, gb_ref, grb_ref, s0_ref,
                     of_ref, ob_ref, sfin_ref, s_ref, *, hg):
    n = pl.program_id(2)

    @pl.when(n == 0)
    def _():
        s_ref[...] = s0_ref[0]

    c = CHUNK
    row = lax.broadcasted_iota(jnp.int32, (c, LANES), 0)
    col = lax.broadcasted_iota(jnp.int32, (c, LANES), 1) % c
    dirs = ((qf_ref, kf_ref, ktf_ref, vf_ref, gf_ref, grf_ref), (qb_ref, kb_ref, ktb_ref, vb_ref, gb_ref, grb_ref))

    eye = jnp.where(row == col, 1.0, 0.0)
    zero_rows = jnp.zeros((c, 2 * GDN_DV), BF16)
    tri_masks = ((row >= col, row > col), (row <= col, row < col))
    pairs = []
    blk_r, blk_c = row, col
    b = 1
    while b < c:
        pair = (blk_r // 2 == blk_c // 2) & (blk_r != blk_c)
        pairs.append(pair if b == 1 else jnp.where(pair, 1.0, 0.0).astype(BF16))
        blk_r, blk_c = blk_r // 2, blk_c // 2
        b *= 2

    def stages(heads):
        chains = []
        for j in heads:
            for d in range(2):
                q_ref, k_ref, kt_ref, v_ref, g_ref, gr_ref = dirs[d]
                qc, kc, v2, gt = q_ref[0, j], k_ref[0, j], v_ref[0, j], g_ref[0, j]
                gtt = gr_ref[0, j, 0]
                ktc = kt_ref[0, j, 0]
                kq = _dot_nt(jnp.concatenate([kc, qc], axis=0), jnp.concatenate([kc, kc], axis=0))
                for v in range(2):
                    chains.append(dict(j=j, d=d, v=v, qc=qc, kc=kc, vv=v2[:, v * GDN_DV:(v + 1) * GDN_DV],
                                       gt=gt, gtt=gtt, ktc=ktc, kk=kq[:c], qk=kq[c:]))
            yield
        for i, ch in enumerate(chains):
            slot = 2 * ch["d"] + ch["v"]
            incl, strict = tri_masks[ch["d"]]
            beta_col = jnp.broadcast_to(ch["gt"][:, slot:slot + 1], (c, LANES))
            g_col = jnp.broadcast_to(ch["gt"][:, 4 + slot:5 + slot], (c, LANES))
            g_row = ch["gtt"][4 + slot:5 + slot, :]
            decay = jnp.where(incl, jnp.exp(jnp.where(incl, g_col - g_row, 0.0)), 0.0)
            eg_col = jnp.exp(g_col)
            ch["g_row"] = g_row
            lmat = jnp.where(strict, ch["kk"] * beta_col * decay, 0.0)
            ch.update(
                lm=lmat.astype(BF16), tinv=eye - jnp.where(pairs[0], lmat, 0.0),
                attn=(ch["qk"] * decay)[:, :c].astype(BF16),
                eg_col=eg_col, g_col=g_col,
                rhs=jnp.concatenate([ch["vv"].astype(F32) * beta_col,
                                     ch["kc"].astype(F32) * (beta_col * eg_col)], axis=1).astype(BF16))
            if i % 4 == 3:
                yield
        for pair_bf in pairs[1:]:
            for ch in chains:
                ch["t_hi"], ch["t_lo"] = _split_bf16(ch["tinv"])
                ch["w"] = _dot(ch["lm"] * pair_bf, jnp.concatenate([ch["t_hi"], ch["t_lo"]], axis=0))
            yield
            for ch in chains:
                w_hi, w_lo = _split_bf16(ch["w"])
                ch["tinv"] = ch["tinv"] - _dot_split(ch["t_hi"], ch["t_lo"], w_hi, w_lo)
            yield
        for ch in chains:
            t_hi, t_lo = _split_bf16(ch["tinv"])
            r = ch["rhs"]
            ch["x"] = _dot(jnp.concatenate([t_hi, t_lo], axis=1),
                           jnp.concatenate([r, zero_rows, r, zero_rows], axis=0))
        yield
        for ch in chains:
            ch["s_idx"] = ch["j"] * 4 + 2 * ch["d"] + ch["v"]
            ch["s_old"] = s_ref[ch["s_idx"]]
            ch["wq"] = _dot(jnp.concatenate([ch["x"][:, GDN_DV:].astype(BF16), ch["qc"]], axis=0),
                            ch["s_old"].astype(BF16))
        yield
        for ch in chains:
            g_col = ch["g_col"]
            ch["g_last"] = g_col[0:1, :] if ch["d"] == 1 else g_col[c - 1:c, :]
            v_new = (ch["x"][:, :GDN_DV] - ch["wq"][:c]).astype(BF16)
            dec_row = jnp.exp(ch["g_last"] - ch["g_row"])[:, :c]
            k_dec_t = (ch["ktc"].astype(F32) * dec_row).astype(BF16)
            ch["av"] = _dot(ch["attn"], v_new)
            ch["kv"] = _dot(k_dec_t, v_new)
        yield
        outs = {}
        for ch in chains:
            s_ref[ch["s_idx"]] = ch["s_old"] * jnp.exp(ch["g_last"]) + ch["kv"]
            outs[ch["j"], ch["d"], ch["v"]] = ch["eg_col"] * ch["wq"][c:] + ch["av"]
        for j in heads:
            for d, o_ref in enumerate((of_ref, ob_ref)):
                o_ref[0, j] = jnp.concatenate([outs[j, d, 0], outs[j, d, 1]], axis=1)

    for _ in stages(list(range(hg))):
        pass

    @pl.when(n == pl.num_programs(2) - 1)
    def _():
        sfin_ref[0] = s_ref[...]


def _gdn_scan_call(q, k, kt, v, gates, gate_rows, s0, *, hg):
    b, kh, t, _ = q.shape
    assert kh % hg == 0 and t % CHUNK == 0
    nc = t // CHUNK
    ng = kh // hg
    fwd = lambda w: pl.BlockSpec((1, hg, CHUNK, w), lambda bb, g, n: (bb, g, n, 0))
    bwd = lambda w: pl.BlockSpec((1, hg, CHUNK, w), lambda bb, g, n: (bb, g, nc - 1 - n, 0))
    fwd_c = lambda r, w: pl.BlockSpec((1, hg, 1, r, w), lambda bb, g, n: (bb, g, n, 0, 0))
    bwd_c = lambda r, w: pl.BlockSpec((1, hg, 1, r, w), lambda bb, g, n: (bb, g, nc - 1 - n, 0, 0))
    st = pl.BlockSpec((1, hg * 4, GDN_DK, GDN_DV), lambda bb, g, n: (bb, g, 0, 0))
    kern = functools.partial(_gdn_scan_kernel, hg=hg)
    return pl.pallas_call(
        kern,
        grid=(b, ng, nc),
        in_specs=[fwd(GDN_DK), fwd(GDN_DK), fwd_c(GDN_DK, CHUNK), fwd(2 * GDN_DV), fwd(LANES), fwd_c(8, LANES),
                  bwd(GDN_DK), bwd(GDN_DK), bwd_c(GDN_DK, CHUNK), bwd(2 * GDN_DV), bwd(LANES), bwd_c(8, LANES),
                  st],
        out_specs=[fwd(2 * GDN_DV), bwd(2 * GDN_DV), st],
        out_shape=[
            jax.ShapeDtypeStruct((b, kh, t, 2 * GDN_DV), F32),
            jax.ShapeDtypeStruct((b, kh, t, 2 * GDN_DV), F32),
            jax.ShapeDtypeStruct(s0.shape, F32),
        ],
        scratch_shapes=[pltpu.VMEM((hg * 4, GDN_DK, GDN_DV), F32)],
        compiler_params=_cparams(("parallel", "parallel", "arbitrary")),
        name="gdn_scan",
    )(q, k, kt, v, gates, gate_rows, q, k, kt, v, gates, gate_rows, s0)


def _rope_tables(t, q_scale):
    assert t % GRID_W == 0
    rows = t // GRID_W
    n_freq = MLA_ROPE // 4
    inv_freq = ROPE_THETA ** (-(2.0 * jnp.arange(n_freq, dtype=F32)) / (MLA_ROPE // 2))
    ar = jnp.arange(rows, dtype=F32)[:, None] * inv_freq
    ac = jnp.arange(GRID_W, dtype=F32)[:, None] * inv_freq
    by_row = lambda x: jnp.broadcast_to(x[:, None, :], (rows, GRID_W, n_freq)).reshape(t, n_freq)
    by_col = lambda x: jnp.broadcast_to(x[None, :, :], (rows, GRID_W, n_freq)).reshape(t, n_freq)
    cos_r, sin_r, cos_c, sin_c = jnp.cos(ar), jnp.sin(ar), jnp.cos(ac), jnp.sin(ac)
    cr, sr, cc, sc = by_row(cos_r), by_row(sin_r), by_col(cos_c), by_col(sin_c)
    km = jnp.concatenate([cr, cr, cc, cc, -sr, sr, -sc, sc], axis=1)
    by_row_t = lambda x: jnp.broadcast_to(x.T[:, :, None], (n_freq, rows, GRID_W)).reshape(n_freq, t)
    by_col_t = lambda x: jnp.broadcast_to(x.T[:, None, :], (n_freq, rows, GRID_W)).reshape(n_freq, t)
    crt, srt, cct, sct = by_row_t(cos_r), by_row_t(sin_r), by_col_t(cos_c), by_col_t(sin_c)
    qrt = q_scale * jnp.concatenate([crt, crt, cct, cct, -srt, srt, -sct, sct], axis=0)
    return qrt, km


def _no_rope_tables(t, q_scale):
    km = jnp.concatenate([jnp.ones((t, MLA_ROPE), F32), jnp.zeros((t, MLA_ROPE), F32)], axis=1)
    qrt = q_scale * jnp.concatenate([jnp.ones((MLA_ROPE, t), F32), jnp.zeros((MLA_ROPE, t), F32)], axis=0)
    return qrt, km


def _swap_perm():
    f = MLA_ROPE // 4
    return jnp.array(list(range(f, 2 * f)) + list(range(0, f)) + list(range(3 * f, 4 * f)) + list(range(2 * f, 3 * f)))


def _mla_layer(h_lat, h_ctx, mod_lat, mod_ctx, ln, w_in, q_norm, kv_norm, w_q_up, w_kv_up, w_out, *, alpha, last):
    bsz, seq, d = h_lat.shape
    tctx = h_ctx.shape[1]
    heads = d // 128
    q_rank, kv_rank = q_norm.shape[0], kv_norm.shape[0]
    tm_lat, tm_ctx = min(TOKEN_TILE, seq), min(TOKEN_TILE, tctx)
    q_scale = (MLA_NOPE + MLA_ROPE) ** -0.5 * math.log2(math.e)
    perm = _swap_perm()
    rope_cols = w_in[:, q_rank + kv_rank:]
    w_in_ext = jnp.concatenate([w_in, rope_cols[:, perm]], axis=1).astype(BF16)
    wq = w_q_up.reshape(q_rank, heads, MLA_NOPE + MLA_ROPE)
    wq_ext = jnp.concatenate([wq, wq[:, :, MLA_NOPE:][:, :, perm]], axis=2)
    wqt = wq_ext.reshape(q_rank, heads * MLA_QK).T.astype(BF16)
    wkv = w_kv_up.reshape(kv_rank, heads, MLA_NOPE + MLA_V)
    wk = wkv[:, :, :MLA_NOPE].reshape(kv_rank, -1).astype(BF16)
    wvt = wkv[:, :, MLA_NOPE:].reshape(kv_rank, -1).T.astype(BF16)
    qn, kvn = q_norm.reshape(1, -1), kv_norm.reshape(1, -1)
    qrt_l, km_l = _rope_tables(seq, q_scale)
    qrt_c, km_c = _no_rope_tables(tctx, q_scale)
    q_l, k_l, v_l = _mla_proj_call(h_lat, mod_lat, w_in_ext, qn, kvn, wqt, wk, wvt, qrt_l, km_l,
                                   tm=tm_lat, heads=heads, q_scale=q_scale)
    q_c, k_c, v_c = _mla_proj_call(h_ctx, mod_ctx, w_in_ext, qn, kvn, wqt, wk, wvt, qrt_c, km_c,
                                   tm=tm_ctx, heads=heads, q_scale=q_scale)
    o_l = _attn_call(q_l, k_c, v_c, k_l, v_l, tq=min(ATTN_Q_TILE, seq))
    w_out = w_out.astype(BF16)
    h_lat_new = _post_call((o_l,), h_lat, mod_lat, w_out, *ln, tm=tm_lat, alpha=alpha)
    if not last:
        o_c = _attn_call(q_c, k_c, v_c, tq=tm_ctx)
        h_ctx = _post_call((o_c,), h_ctx, mod_ctx, w_out, *ln, tm=tm_ctx, alpha=alpha)
    return h_lat_new, h_ctx


def _gdn_layer(h_lat, h_ctx, mod_lat, mod_ctx, ln, w_in, conv_w, a_log, dt_bias, norm_w, w_out, *, alpha, last):
    bsz, seq, d = h_lat.shape
    tctx = h_ctx.shape[1]
    k_heads = d // 128
    v_heads = 2 * k_heads
    kw, vw = k_heads * GDN_DK, v_heads * GDN_DV
    qkv_w = 2 * kw + vw
    tm_lat, tm_ctx = min(TOKEN_TILE, seq), min(TOKEN_TILE, tctx)
    w_ba = w_in[:, qkv_w + vw:].reshape(d, 2, 2, k_heads, 2)
    w_ba = jnp.transpose(w_ba, (0, 3, 2, 1, 4)).reshape(d, k_heads, 8)
    w_ba = jnp.pad(w_ba, ((0, 0), (0, 0), (0, LANES - 8))).reshape(d, k_heads * LANES)
    w_ext = jnp.concatenate([w_in[:, :qkv_w + vw].astype(BF16), w_ba.astype(BF16)], axis=1)

    def gate_lanes(p):
        p = jnp.transpose(p.astype(F32).reshape(2, k_heads, 2), (1, 0, 2)).reshape(k_heads, 4)
        p = jnp.pad(p, ((0, 0), (4, LANES - 8)))
        return p.reshape(1, k_heads * LANES)

    al_ext, dtb_ext = gate_lanes(a_log), gate_lanes(dt_bias)
    s = jnp.zeros((bsz, k_heads * 4, GDN_DK, GDN_DV), F32)
    streams = []
    for h_in, m_in, tm in ((h_ctx, mod_ctx, tm_ctx), (h_lat, mod_lat, tm_lat)):
        qh, kh, kt, vh, gates, gate_rows, z = _gdn_in_call(h_in, m_in, w_ext, conv_w, al_ext, dtb_ext,
                                                           tm=tm, k_heads=k_heads, z_w=vw)
        o_f, o_b, s = _gdn_scan_call(qh, kh, kt, vh, gates, gate_rows, s, hg=min(SCAN_KEY_HEADS, k_heads))
        streams.append((o_f, o_b, z))
    w_out = w_out.astype(BF16)
    nw = norm_w.reshape(1, -1)
    h_lat_new = _post_call(streams[1], h_lat, mod_lat, w_out, *ln, tm=tm_lat, alpha=alpha, gdn_norm=nw)
    if not last:
        h_ctx = _post_call(streams[0], h_ctx, mod_ctx, w_out, *ln, tm=tm_ctx, alpha=alpha, gdn_norm=nw)
    return h_lat_new, h_ctx


def kernel(x, c, ctx, c_ctx, w_mod, b_mod, ln1_g, ln1_b, ln2_g, ln2_b, w_ffn_in, w_ffn_out, mla_w_in, mla_q_norm, mla_kv_norm, mla_w_q_up, mla_w_kv_up, mla_w_out, gdn_w_in, gdn_conv, gdn_a_log, gdn_dt_bias, gdn_norm, gdn_w_out):
    bsz, _, d = x.shape
    depth = w_mod.shape[0]
    alpha = (2.0 * depth) ** 0.25

    cc = jnp.zeros((8, d), F32).at[:bsz].set(c).at[bsz].set(c_ctx)
    mod = _mod_call(cc, w_mod, b_mod)

    row = lambda a, i: a[i].reshape(1, -1)
    h_lat, h_ctx = x, ctx
    for i in range(depth):
        last = i == depth - 1
        mod_lat = mod[i, :bsz].reshape(bsz, 1, N_MOD * d)
        mod_ctx = jnp.broadcast_to(mod[i, bsz].reshape(1, 1, N_MOD * d), (bsz, 1, N_MOD * d))
        ln = (row(ln1_g, i), row(ln1_b, i), w_ffn_in[i].astype(BF16), w_ffn_out[i].astype(BF16),
              row(ln2_g, i), row(ln2_b, i))
        j = i // 2
        if i % 2 == 0:
            h_lat, h_ctx = _mla_layer(h_lat, h_ctx, mod_lat, mod_ctx, ln, mla_w_in[j], mla_q_norm[j],
                                      mla_kv_norm[j], mla_w_q_up[j], mla_w_kv_up[j], mla_w_out[j],
                                      alpha=alpha, last=last)
        else:
            h_lat, h_ctx = _gdn_layer(h_lat, h_ctx, mod_lat, mod_ctx, ln, gdn_w_in[j], gdn_conv[j],
                                      gdn_a_log[j], gdn_dt_bias[j], gdn_norm[j], gdn_w_out[j],
                                      alpha=alpha, last=last)
    return h_lat
```

```python
import functools
import math

import jax
import jax.numpy as jnp
from jax import lax
from jax.experimental import pallas as pl
from jax.experimental.pallas import tpu as pltpu

F32 = jnp.float32
BF16 = jnp.bfloat16

N_MOD = 6
LN_EPS = 1e-5
RMS_EPS = 1e-6
GRID_W = 64
ROPE_THETA = 10000.0

MLA_NOPE = 128
MLA_ROPE = 64
MLA_V = 128
MLA_QK = MLA_NOPE + 2 * MLA_ROPE

GDN_DK = 128
GDN_DV = 128
GDN_CONV = 5
CHUNK = 64
CONV_HALO = 8

V7X_VMEM_LIMIT = 56 * 1024 * 1024
LANES = 128

TOKEN_TILE = 256
ATTN_Q_TILE = 4096
ATTN_KEY_SLABS = 4
ATTN_CHUNKS_PER_TRIP = 1
SCAN_KEY_HEADS = 8


def _cparams(sem):
    return pltpu.CompilerParams(dimension_semantics=sem, vmem_limit_bytes=V7X_VMEM_LIMIT)


def _const_spec(shape):
    nd = len(shape)
    return pl.BlockSpec(shape, lambda *_: (0,) * nd, pipeline_mode=pl.Buffered(1))


def _dot(a, b):
    return jnp.dot(a, b, preferred_element_type=F32)


def _dot_nt(a, b):
    return lax.dot_general(a, b, (((1,), (1,)), ((), ())), preferred_element_type=F32)


def _split_bf16(a):
    hi = a.astype(BF16)
    lo = (a - hi.astype(F32)).astype(BF16)
    return hi, lo


def _silu(x):
    return x * jax.nn.sigmoid(x)


def _layer_norm(x, g, b):
    mu = jnp.mean(x, axis=-1, keepdims=True)
    xc = x - mu
    var = jnp.mean(xc * xc, axis=-1, keepdims=True)
    return xc * lax.rsqrt(var + LN_EPS) * g + b


def _rms_norm(x, g):
    return x * lax.rsqrt(jnp.mean(x * x, axis=-1, keepdims=True) + RMS_EPS) * g


def _mod_kernel(c_ref, w_ref, b_ref, o_ref):
    a = _silu(c_ref[...]).astype(BF16)
    o_ref[0] = _dot(a, w_ref[0].astype(BF16)) + b_ref[0]


def _mod_call(cc, w_mod, b_mod):
    depth, d, n = w_mod.shape
    tn = 1536
    assert n % tn == 0
    return pl.pallas_call(
        _mod_kernel,
        grid=(depth, n // tn),
        in_specs=[
            pl.BlockSpec((8, d), lambda l, j: (0, 0)),
            pl.BlockSpec((1, d, tn), lambda l, j: (l, 0, j)),
            pl.BlockSpec((1, 1, tn), lambda l, j: (l, 0, j)),
        ],
        out_specs=pl.BlockSpec((1, 8, tn), lambda l, j: (l, 0, j)),
        out_shape=jax.ShapeDtypeStruct((depth, 8, n), F32),
        compiler_params=_cparams(("parallel", "parallel")),
        name="mod",
    )(cc, w_mod, b_mod.reshape(depth, 1, n))


def _mod_spec(d, j):
    return pl.BlockSpec((1, 1, d), lambda b, i: (b, 0, j))


VT_ROWS = MLA_V + 16


def _mla_proj_kernel(h_ref, sh_ref, sc_ref, win_ref, qn_ref, kvn_ref, wqt_ref, wk_ref, wvt_ref,
                     qrt_ref, km_ref, qt_ref, k_ref, vt_ref, *, heads, q_rank, kv_rank, q_scale):
    x = h_ref[0]
    tm = x.shape[0]
    u = x * (1.0 + sc_ref[0]) + sh_ref[0]
    lat = _dot(u.astype(BF16), win_ref[...])
    cq = _rms_norm(lat[:, :q_rank], qn_ref[...]).astype(BF16)
    ckv = _rms_norm(lat[:, q_rank:q_rank + kv_rank], kvn_ref[...]).astype(BF16)
    kr2 = lat[:, q_rank + kv_rank:]
    qt_all = _dot_nt(wqt_ref[...], cq)
    k_all = _dot(ckv, wk_ref[...])
    vt_all = _dot_nt(wvt_ref[...], ckv)
    qrt = qrt_ref[...]
    t = kr2 * km_ref[...]
    krot2 = (t + pltpu.roll(t, MLA_ROPE, axis=1)).astype(BF16)
    ones_rows = jnp.where(lax.broadcasted_iota(jnp.int32, (VT_ROWS - MLA_V, tm), 0) == 0, 1.0, 0.0).astype(BF16)
    for hh in range(heads):
        qh = qt_all[hh * MLA_QK:(hh + 1) * MLA_QK, :]
        qt_ref[0, hh, :MLA_NOPE, :] = (qh[:MLA_NOPE] * q_scale).astype(BF16)
        qt_ref[0, hh, MLA_NOPE:, :] = (qh[MLA_NOPE:] * qrt).astype(BF16)
        k_ref[0, hh, :, :MLA_NOPE] = k_all[:, hh * MLA_NOPE:(hh + 1) * MLA_NOPE].astype(BF16)
        k_ref[0, hh, :, MLA_NOPE:] = krot2
        vt_ref[0, hh, 0, :MLA_V, :] = vt_all[hh * MLA_V:(hh + 1) * MLA_V, :].astype(BF16)
        vt_ref[0, hh, 0, MLA_V:, :] = ones_rows


def _mla_proj_call(h, mod, w_in, qn, kvn, wqt, wk, wvt, qrt, km, *, tm, heads, q_scale):
    b, t, d = h.shape
    q_rank, kv_rank = qn.shape[1], kvn.shape[1]
    assert t % tm == 0
    kern = functools.partial(_mla_proj_kernel, heads=heads, q_rank=q_rank, kv_rank=kv_rank, q_scale=q_scale)
    return pl.pallas_call(
        kern,
        grid=(b, t // tm),
        in_specs=[
            pl.BlockSpec((1, tm, d), lambda bb, i: (bb, i, 0)),
            _mod_spec(d, 0), _mod_spec(d, 1),
            _const_spec(w_in.shape), _const_spec(qn.shape), _const_spec(kvn.shape),
            _const_spec(wqt.shape), _const_spec(wk.shape), _const_spec(wvt.shape),
            pl.BlockSpec((2 * MLA_ROPE, tm), lambda bb, i: (0, i)),
            pl.BlockSpec((tm, 2 * MLA_ROPE), lambda bb, i: (i, 0)),
        ],
        out_specs=[
            pl.BlockSpec((1, heads, MLA_QK, tm), lambda bb, i: (bb, 0, 0, i)),
            pl.BlockSpec((1, heads, tm, MLA_QK), lambda bb, i: (bb, 0, i, 0)),
            pl.BlockSpec((1, heads, 1, VT_ROWS, tm), lambda bb, i: (bb, 0, i, 0, 0)),
        ],
        out_shape=[
            jax.ShapeDtypeStruct((b, heads, MLA_QK, t), BF16),
            jax.ShapeDtypeStruct((b, heads, t, MLA_QK), BF16),
            jax.ShapeDtypeStruct((b, heads, t // tm, VT_ROWS, tm), BF16),
        ],
        compiler_params=_cparams(("parallel", "parallel")),
        name="mla_proj",
    )(h, mod, mod, w_in, qn, kvn, wqt, wk, wvt, qrt, km)


QGROUP = 256


def _attn_kernel(*refs, slabs_per_step, n_lat, n_groups, chunks_per_trip):
    if n_lat:
        qt_ref, kc_ref, vtc_ref, kl_ref, vtl_ref, o_ref, m_ref, acc_ref, st_ref = refs
    else:
        qt_ref, kc_ref, vtc_ref, o_ref, m_ref, acc_ref = refs

    def scores(kblk, g):
        return _dot(kblk, qt_ref[0, 0, :, g * QGROUP:(g + 1) * QGROUP])

    def softmax_pv(st, vt_slabs, g, first):
        slab = vt_slabs[0].shape[1]
        cols = slice(g * QGROUP, (g + 1) * QGROUP)
        cmax = jnp.max(st, axis=0, keepdims=True)
        if first:
            m_new = cmax
        else:
            m_old = m_ref[:, cols]
            m_new = jnp.maximum(m_old, cmax)
        pt = jnp.exp2(st - m_new).astype(BF16)
        pv = None
        for s, vt in enumerate(vt_slabs):
            term = _dot(vt, pt[s * slab:(s + 1) * slab, :])
            pv = term if pv is None else pv + term
        if first:
            acc_ref[:, cols] = pv
        else:
            acc_ref[:, cols] = acc_ref[:, cols] * jnp.exp2(m_old - m_new) + pv
        m_ref[:, cols] = m_new

    kc = kc_ref[0, 0]
    vtc = [vtc_ref[0, 0, s] for s in range(vtc_ref.shape[2])]
    sts = [scores(kc, g) for g in range(n_groups)]
    if n_lat:
        slab = vtl_ref.shape[4]
        tk = slabs_per_step * slab
        k0 = kl_ref[0, 0, pl.ds(0, tk), :]
        nxt = scores(k0, 0)
        for g in range(n_groups):
            cur, nxt = nxt, (scores(k0, g + 1) if g + 1 < n_groups else None)
            softmax_pv(sts[g], vtc, g, True)
            st_ref[g] = cur
    else:
        for g in range(n_groups):
            softmax_pv(sts[g], vtc, g, True)
    if n_lat:

        def chunk_step(j):
            off = pl.multiple_of(jnp.minimum(j + 1, n_lat - 1) * tk, tk)
            k_next = kl_ref[0, 0, pl.ds(off, tk), :]
            vts = [vtl_ref[0, 0, j * slabs_per_step + s] for s in range(slabs_per_step)]
            nxt = scores(k_next, 0)
            for g in range(n_groups):
                cur, nxt = nxt, (scores(k_next, g + 1) if g + 1 < n_groups else None)
                softmax_pv(st_ref[g], vts, g, False)
                st_ref[g] = cur

        def body(jj, carry):
            for u in range(chunks_per_trip):
                chunk_step(jj * chunks_per_trip + u)
            return carry
        lax.fori_loop(0, n_lat // chunks_per_trip, body, 0)
    acc = acc_ref[...]
    ot = acc[:MLA_V] * (1.0 / acc[MLA_V:MLA_V + 1])
    o_ref[0] = ot.T.astype(o_ref.dtype)


def _attn_call(qt, k_ctx, vt_ctx, k_lat=None, vt_lat=None, *, tq, slabs_per_step=ATTN_KEY_SLABS):
    b, h, dqk, t = qt.shape
    tc = k_ctx.shape[2]
    assert t % tq == 0 and tq % QGROUP == 0
    n_lat = 0
    in_specs = [
        pl.BlockSpec((1, 1, dqk, tq), lambda bb, hh, i: (bb, hh, 0, i)),
        pl.BlockSpec((1, 1, tc, dqk), lambda bb, hh, i: (bb, hh, 0, 0)),
        pl.BlockSpec((1, 1) + vt_ctx.shape[2:], lambda bb, hh, i: (bb, hh, 0, 0, 0)),
    ]
    args = [qt, k_ctx, vt_ctx]
    if k_lat is not None:
        tl = k_lat.shape[2]
        ns = vt_lat.shape[2]
        assert ns % slabs_per_step == 0
        n_lat = ns // slabs_per_step
        in_specs += [
            pl.BlockSpec((1, 1, tl, dqk), lambda bb, hh, i: (bb, hh, 0, 0)),
            pl.BlockSpec((1, 1) + vt_lat.shape[2:], lambda bb, hh, i: (bb, hh, 0, 0, 0)),
        ]
        args += [k_lat, vt_lat]
    n_groups = tq // QGROUP
    chunks_per_trip = ATTN_CHUNKS_PER_TRIP if n_lat % ATTN_CHUNKS_PER_TRIP == 0 else 1
    kern = functools.partial(_attn_kernel, slabs_per_step=slabs_per_step, n_lat=n_lat, n_groups=n_groups,
                             chunks_per_trip=chunks_per_trip)
    scratch = [pltpu.VMEM((1, tq), F32), pltpu.VMEM((VT_ROWS, tq), F32)]
    if n_lat:
        scratch.append(pltpu.VMEM((n_groups, slabs_per_step * vt_lat.shape[4], QGROUP), F32))
    return pl.pallas_call(
        kern,
        grid=(b, h, t // tq),
        in_specs=in_specs,
        out_specs=pl.BlockSpec((1, tq, MLA_V), lambda bb, hh, i: (bb, i, hh)),
        out_shape=jax.ShapeDtypeStruct((b, t, h * MLA_V), BF16),
        scratch_shapes=scratch,
        compiler_params=_cparams(("parallel", "parallel", "arbitrary")),
        name="attn_lat" if n_lat else "attn_ctx",
    )(*args)


POST_SPLIT = 2


def _post_tail(y_in_rows, h_ref, g1_ref, sh2_ref, sc2_ref, g2_ref, wout_ref, l1g_ref, l1b_ref,
               wfi_ref, wfo_ref, l2g_ref, l2b_ref, o_ref, *, alpha, d_ff):
    tm = h_ref.shape[1]
    rows_per = tm // POST_SPLIT

    def stages(r0):
        rows = slice(r0, r0 + rows_per)
        y_in = y_in_rows(rows)
        yield
        y = _dot(y_in, wout_ref[...])
        yield
        h1 = _layer_norm(alpha * h_ref[0, rows] + g1_ref[0] * y, l1g_ref[...], l1b_ref[...])
        u2 = (h1 * (1.0 + sc2_ref[0]) + sh2_ref[0]).astype(BF16)
        yield
        gu = _dot(u2, wfi_ref[...])
        yield
        act = (_silu(gu[:, :d_ff]) * gu[:, d_ff:]).astype(BF16)
        yield
        f = _dot(act, wfo_ref[...])
        yield
        o_ref[0, rows] = _layer_norm(alpha * h1 + g2_ref[0] * f, l2g_ref[...], l2b_ref[...])

    gens = [stages(r0) for r0 in range(0, tm, rows_per)]
    live = []
    while gens or live:
        if gens:
            live.append(gens.pop(0))
            for _ in range(2):
                next(live[-1], None)
        for g in list(live):
            if next(g, "done") == "done":
                live.remove(g)


def _post_mla_kernel(a_ref, *rest, alpha, d_ff):
    _post_tail(lambda rows: a_ref[0, rows], *rest, alpha=alpha, d_ff=d_ff)


def _post_gdn_kernel(of_ref, ob_ref, z_ref, nw_ref, *rest, alpha, d_ff, k_heads):
    nw = nw_ref[...]

    def y_in_rows(rows):
        parts = []
        for kh in range(k_heads):
            o2 = of_ref[0, kh, rows] + ob_ref[0, kh, rows]
            for v in range(2):
                o = o2[:, v * GDN_DV:(v + 1) * GDN_DV]
                vh = 2 * kh + v
                z = z_ref[0, rows, vh * GDN_DV:(vh + 1) * GDN_DV]
                parts.append((_rms_norm(o, nw) * _silu(z)).astype(BF16))
        return jnp.concatenate(parts, axis=1)

    _post_tail(y_in_rows, *rest, alpha=alpha, d_ff=d_ff)


def _post_call(mixer_in, h, mod, w_out, l1g, l1b, wfi, wfo, l2g, l2b, *, tm, alpha, gdn_norm=None):
    b, t, d = h.shape
    d_ff = wfo.shape[0]
    assert t % tm == 0
    row = lambda w: pl.BlockSpec((1, tm, w), lambda bb, i: (bb, i, 0))
    if gdn_norm is None:
        (a,) = mixer_in
        kern = functools.partial(_post_mla_kernel, alpha=alpha, d_ff=d_ff)
        head_specs = [row(a.shape[2])]
        head_args = [a]
    else:
        o_f, o_b, z = mixer_in
        kh = o_f.shape[1]
        kern = functools.partial(_post_gdn_kernel, alpha=alpha, d_ff=d_ff, k_heads=kh)
        hm = pl.BlockSpec((1, kh, tm, 2 * GDN_DV), lambda bb, i: (bb, 0, i, 0))
        head_specs = [hm, hm, row(z.shape[2]), _const_spec(gdn_norm.shape)]
        head_args = [o_f, o_b, z, gdn_norm]
    return pl.pallas_call(
        kern,
        grid=(b, t // tm),
        in_specs=head_specs + [
            row(d),
            _mod_spec(d, 2), _mod_spec(d, 3), _mod_spec(d, 4), _mod_spec(d, 5),
            _const_spec(w_out.shape), _const_spec(l1g.shape), _const_spec(l1b.shape),
            _const_spec(wfi.shape), _const_spec(wfo.shape),
            _const_spec(l2g.shape), _const_spec(l2b.shape),
        ],
        out_specs=row(d),
        out_shape=jax.ShapeDtypeStruct((b, t, d), F32),
        compiler_params=_cparams(("parallel", "parallel")),
        name="post_mla" if gdn_norm is None else "post_gdn",
    )(*head_args, h, mod, mod, mod, mod, w_out, l1g, l1b, wfi, wfo, l2g, l2b)


def _chunk_cumsum(g, suffix):
    n = g.shape[0]
    pos = lax.broadcasted_iota(jnp.int32, g.shape, 0) % CHUNK
    s = 1
    while s < CHUNK:
        if suffix:
            shifted = pltpu.roll(g, n - s, axis=0)
            ok = pos < CHUNK - s
        else:
            shifted = pltpu.roll(g, s, axis=0)
            ok = pos >= s
        g = g + jnp.where(ok, shifted, 0.0)
        s *= 2
    return g


CONV_ROWS = 256
CONV_COLS = 256


def _gdn_in_kernel(hp_ref, hc_ref, hn_ref, sh_ref, sc_ref, w_ref, cw_ref, al_ref, dtb_ref,
                   q_ref, k_ref, kt_ref, v_ref, gt_ref, gr_ref, z_ref, xe_ref, y_ref, *, k_heads, tm, z_w):
    i = pl.program_id(1)
    n_i = pl.num_programs(1)
    kw = k_heads * GDN_DK
    c_all = xe_ref.shape[1]
    n_chunks = tm // CHUNK
    half = GDN_CONV // 2
    h_ext = jnp.concatenate([hp_ref[0], hc_ref[0], hn_ref[0]], axis=0)
    u = (h_ext * (1.0 + sc_ref[0]) + sh_ref[0]).astype(BF16)
    keep_prev = jnp.where(i > 0, 1.0, 0.0)
    keep_next = jnp.where(i < n_i - 1, 1.0, 0.0)

    def project(c0):
        return _dot(u, w_ref[:, c0:c0 + CONV_COLS])

    def conv_tile(p, c0):
        cols = slice(c0, c0 + CONV_COLS)
        xe_ref[0:CONV_HALO, cols] = p[0:CONV_HALO] * keep_prev
        xe_ref[CONV_HALO:CONV_HALO + tm, cols] = p[CONV_HALO:CONV_HALO + tm]
        xe_ref[CONV_HALO + tm:, cols] = p[CONV_HALO + tm:] * keep_next
        w = cw_ref[:, cols]
        for r0 in range(0, tm, CONV_ROWS):
            acc = None
            for j in range(GDN_CONV):
                start = CONV_HALO - half + j + r0
                term = xe_ref[start:start + CONV_ROWS, cols] * w[j:j + 1, :]
                acc = term if acc is None else acc + term
            y_ref[r0:r0 + CONV_ROWS, cols] = _silu(acc)

    tiles = list(range(0, c_all, CONV_COLS))
    nxt = project(tiles[0])
    for t_idx, c0 in enumerate(tiles):
        cur = nxt
        if t_idx + 1 < len(tiles):
            nxt = project(tiles[t_idx + 1])
        else:
            nxt = _dot(u[CONV_HALO:CONV_HALO + tm], w_ref[:, c_all:])
        conv_tile(cur, c0)
    z_ref[0] = nxt[:, :z_w]
    ba = nxt[:, z_w:]
    for kh in range(k_heads):
        qh = y_ref[:, kh * GDN_DK:(kh + 1) * GDN_DK]
        q_ref[0, kh] = (qh * lax.rsqrt(jnp.sum(qh * qh, axis=-1, keepdims=True) + RMS_EPS)
                        * (GDN_DK ** -0.5)).astype(BF16)
        kk = y_ref[:, kw + kh * GDN_DK:kw + (kh + 1) * GDN_DK]
        kn = kk * lax.rsqrt(jnp.sum(kk * kk, axis=-1, keepdims=True) + RMS_EPS)
        k_ref[0, kh] = kn.astype(BF16)
        knt = kn.T.astype(BF16)
        for n in range(n_chunks):
            kt_ref[0, kh, n] = knt[:, n * CHUNK:(n + 1) * CHUNK]
        v_ref[0, kh] = y_ref[:, 2 * kw + kh * 2 * GDN_DV:2 * kw + (kh + 1) * 2 * GDN_DV].astype(BF16)
    lane = lax.broadcasted_iota(jnp.int32, ba.shape, 1) % LANES
    beta = jax.nn.sigmoid(ba)
    g = -jnp.exp(al_ref[...]) * jax.nn.softplus(ba + dtb_ref[...])
    g = jnp.where((lane >= 4) & (lane < 8), g, 0.0)
    gcum = jnp.where(lane < 6, _chunk_cumsum(g, False), _chunk_cumsum(g, True))
    gates = jnp.where(lane < 4, beta, gcum)
    for kh in range(k_heads):
        gk = gates[:, kh * LANES:(kh + 1) * LANES]
        gt_ref[0, kh] = gk
        rows = gk.T[0:8, :]
        for n in range(n_chunks):
            piece = rows[:, n * CHUNK:(n + 1) * CHUNK]
            gr_ref[0, kh, n] = jnp.concatenate([piece, piece], axis=1)


def _gdn_in_call(h, mod, w, conv_w, a_log_ext, dtb_ext, *, tm, k_heads, z_w):
    b, t, d = h.shape
    c = conv_w.shape[1]
    assert t % tm == 0 and tm % CHUNK == 0 and tm % CONV_HALO == 0
    assert tm % CONV_ROWS == 0 and c % CONV_COLS == 0
    r = tm // CONV_HALO
    nblk8 = t // CONV_HALO
    ncb = tm // CHUNK
    kern = functools.partial(_gdn_in_kernel, k_heads=k_heads, tm=tm, z_w=z_w)
    hm = lambda wd: pl.BlockSpec((1, k_heads, tm, wd), lambda bb, i: (bb, 0, i, 0))
    per_chunk = lambda rr, wd: pl.BlockSpec((1, k_heads, ncb, rr, wd), lambda bb, i: (bb, 0, i, 0, 0))
    return pl.pallas_call(
        kern,
        grid=(b, t // tm),
        in_specs=[
            pl.BlockSpec((1, CONV_HALO, d), lambda bb, i: (bb, jnp.maximum(i * r - 1, 0), 0)),
            pl.BlockSpec((1, tm, d), lambda bb, i: (bb, i, 0)),
            pl.BlockSpec((1, CONV_HALO, d), lambda bb, i: (bb, jnp.minimum((i + 1) * r, nblk8 - 1), 0)),
            _mod_spec(d, 0), _mod_spec(d, 1),
            _const_spec(w.shape), _const_spec(conv_w.shape),
            _const_spec(a_log_ext.shape), _const_spec(dtb_ext.shape),
        ],
        out_specs=[hm(GDN_DK), hm(GDN_DK), per_chunk(GDN_DK, CHUNK), hm(2 * GDN_DV), hm(LANES),
                   per_chunk(8, LANES), pl.BlockSpec((1, tm, z_w), lambda bb, i: (bb, i, 0))],
        out_shape=[
            jax.ShapeDtypeStruct((b, k_heads, t, GDN_DK), BF16),
            jax.ShapeDtypeStruct((b, k_heads, t, GDN_DK), BF16),
            jax.ShapeDtypeStruct((b, k_heads, t // CHUNK, GDN_DK, CHUNK), BF16),
            jax.ShapeDtypeStruct((b, k_heads, t, 2 * GDN_DV), BF16),
            jax.ShapeDtypeStruct((b, k_heads, t, LANES), F32),
            jax.ShapeDtypeStruct((b, k_heads, t // CHUNK, 8, LANES), F32),
            jax.ShapeDtypeStruct((b, t, z_w), F32),
        ],
        scratch_shapes=[pltpu.VMEM((tm + 2 * CONV_HALO, c), F32), pltpu.VMEM((tm, c), F32)],
        compiler_params=_cparams(("parallel", "parallel")),
        name="gdn_in",
    )(h, h, h, mod, mod, w, conv_w, a_log_ext, dtb_ext)


def _dot_split(a_hi, a_lo, b_hi, b_lo):
    return _dot(jnp.concatenate([a_hi, a_lo], axis=1), jnp.concatenate([b_hi, b_lo, b_hi, b_lo], axis=0))


def _gdn_scan_kernel(qf_ref, kf_ref, ktf_ref, vf_ref, gf_ref, grf_ref,
                     qb_ref, kb_ref, ktb_ref, vb_ref, gb_ref, grb_ref, s0_ref,
                     of_ref, ob_ref, sfin_ref, s_ref, *, hg):
    n = pl.program_id(2)

    @pl.when(n == 0)
    def _():
        s_ref[...] = s0_ref[0]

    c = CHUNK
    row = lax.broadcasted_iota(jnp.int32, (c, LANES), 0)
    col = lax.broadcasted_iota(jnp.int32, (c, LANES), 1) % c
    dirs = ((qf_ref, kf_ref, ktf_ref, vf_ref, gf_ref, grf_ref), (qb_ref, kb_ref, ktb_ref, vb_ref, gb_ref, grb_ref))

    eye = jnp.where(row == col, 1.0, 0.0)
    zero_rows = jnp.zeros((c, 2 * GDN_DV), BF16)
    tri_masks = ((row >= col, row > col), (row <= col, row < col))
    pairs = []
    blk_r, blk_c = row, col
    b = 1
    while b < c:
        pair = (blk_r // 2 == blk_c // 2) & (blk_r != blk_c)
        pairs.append(pair if b == 1 else jnp.where(pair, 1.0, 0.0).astype(BF16))
        blk_r, blk_c = blk_r // 2, blk_c // 2
        b *= 2

    def stages(heads):
        chains = []
        for j in heads:
            for d in range(2):
                q_ref, k_ref, kt_ref, v_ref, g_ref, gr_ref = dirs[d]
                qc, kc, v2, gt = q_ref[0, j], k_ref[0, j], v_ref[0, j], g_ref[0, j]
                gtt = gr_ref[0, j, 0]
                ktc = kt_ref[0, j, 0]
                kq = _dot_nt(jnp.concatenate([kc, qc], axis=0), jnp.concatenate([kc, kc], axis=0))
                for v in range(2):
                    chains.append(dict(j=j, d=d, v=v, qc=qc, kc=kc, vv=v2[:, v * GDN_DV:(v + 1) * GDN_DV],
                                       gt=gt, gtt=gtt, ktc=ktc, kk=kq[:c], qk=kq[c:]))
            yield
        for i, ch in enumerate(chains):
            slot = 2 * ch["d"] + ch["v"]
            incl, strict = tri_masks[ch["d"]]
            beta_col = jnp.broadcast_to(ch["gt"][:, slot:slot + 1], (c, LANES))
            g_col = jnp.broadcast_to(ch["gt"][:, 4 + slot:5 + slot], (c, LANES))
            g_row = ch["gtt"][4 + slot:5 + slot, :]
            decay = jnp.where(incl, jnp.exp(jnp.where(incl, g_col - g_row, 0.0)), 0.0)
            eg_col = jnp.exp(g_col)
            ch["g_row"] = g_row
            lmat = jnp.where(strict, ch["kk"] * beta_col * decay, 0.0)
            ch.update(
                lm=lmat.astype(BF16), tinv=eye - jnp.where(pairs[0], lmat, 0.0),
                attn=(ch["qk"] * decay)[:, :c].astype(BF16),
                eg_col=eg_col, g_col=g_col,
                rhs=jnp.concatenate([ch["vv"].astype(F32) * beta_col,
                                     ch["kc"].astype(F32) * (beta_col * eg_col)], axis=1).astype(BF16))
            if i % 4 == 3:
                yield
        for pair_bf in pairs[1:]:
            for ch in chains:
                ch["t_hi"], ch["t_lo"] = _split_bf16(ch["tinv"])
                ch["w"] = _dot(ch["lm"] * pair_bf, jnp.concatenate([ch["t_hi"], ch["t_lo"]], axis=0))
            yield
            for ch in chains:
                w_hi, w_lo = _split_bf16(ch["w"])
                ch["tinv"] = ch["tinv"] - _dot_split(ch["t_hi"], ch["t_lo"], w_hi, w_lo)
            yield
        for ch in chains:
            t_hi, t_lo = _split_bf16(ch["tinv"])
            r = ch["rhs"]
            ch["x"] = _dot(jnp.concatenate([t_hi, t_lo], axis=1),
                           jnp.concatenate([r, zero_rows, r, zero_rows], axis=0))
        yield
        for ch in chains:
            ch["s_idx"] = ch["j"] * 4 + 2 * ch["d"] + ch["v"]
            ch["s_old"] = s_ref[ch["s_idx"]]
            ch["wq"] = _dot(jnp.concatenate([ch["x"][:, GDN_DV:].astype(BF16), ch["qc"]], axis=0),
                            ch["s_old"].astype(BF16))
        yield
        for ch in chains:
            g_col = ch["g_col"]
            ch["g_last"] = g_col[0:1, :] if ch["d"] == 1 else g_col[c - 1:c, :]
            v_new = (ch["x"][:, :GDN_DV] - ch["wq"][:c]).astype(BF16)
            dec_row = jnp.exp(ch["g_last"] - ch["g_row"])[:, :c]
            k_dec_t = (ch["ktc"].astype(F32) * dec_row).astype(BF16)
            ch["av"] = _dot(ch["attn"], v_new)
            ch["kv"] = _dot(k_dec_t, v_new)
        yield
        outs = {}
        for ch in chains:
            s_ref[ch["s_idx"]] = ch["s_old"] * jnp.exp(ch["g_last"]) + ch["kv"]
            outs[ch["j"], ch["d"], ch["v"]] = ch["eg_col"] * ch["wq"][c:] + ch["av"]
        for j in heads:
            for d, o_ref in enumerate((of_ref, ob_ref)):
                o_ref[0, j] = jnp.concatenate([outs[j, d, 0], outs[j, d, 1]], axis=1)

    for _ in stages(list(range(hg))):
        pass

    @pl.when(n == pl.num_programs(2) - 1)
    def _():
        sfin_ref[0] = s_ref[...]


def _gdn_scan_call(q, k, kt, v, gates, gate_rows, s0, *, hg):
    b, kh, t, _ = q.shape
    assert kh % hg == 0 and t % CHUNK == 0
    nc = t // CHUNK
    ng = kh // hg
    fwd = lambda w: pl.BlockSpec((1, hg, CHUNK, w), lambda bb, g, n: (bb, g, n, 0))
    bwd = lambda w: pl.BlockSpec((1, hg, CHUNK, w), lambda bb, g, n: (bb, g, nc - 1 - n, 0))
    fwd_c = lambda r, w: pl.BlockSpec((1, hg, 1, r, w), lambda bb, g, n: (bb, g, n, 0, 0))
    bwd_c = lambda r, w: pl.BlockSpec((1, hg, 1, r, w), lambda bb, g, n: (bb, g, nc - 1 - n, 0, 0))
    st = pl.BlockSpec((1, hg * 4, GDN_DK, GDN_DV), lambda bb, g, n: (bb, g, 0, 0))
    kern = functools.partial(_gdn_scan_kernel, hg=hg)
    return pl.pallas_call(
        kern,
        grid=(b, ng, nc),
        in_specs=[fwd(GDN_DK), fwd(GDN_DK), fwd_c(GDN_DK, CHUNK), fwd(2 * GDN_DV), fwd(LANES), fwd_c(8, LANES),
                  bwd(GDN_DK), bwd(GDN_DK), bwd_c(GDN_DK, CHUNK), bwd(2 * GDN_DV), bwd(LANES), bwd_c(8, LANES),
                  st],
        out_specs=[fwd(2 * GDN_DV), bwd(2 * GDN_DV), st],
        out_shape=[
            jax.ShapeDtypeStruct((b, kh, t, 2 * GDN_DV), F32),
            jax.ShapeDtypeStruct((b, kh, t, 2 * GDN_DV), F32),
            jax.ShapeDtypeStruct(s0.shape, F32),
        ],
        scratch_shapes=[pltpu.VMEM((hg * 4, GDN_DK, GDN_DV), F32)],
        compiler_params=_cparams(("parallel", "parallel", "arbitrary")),
        name="gdn_scan",
    )(q, k, kt, v, gates, gate_rows, q, k, kt, v, gates, gate_rows, s0)


def _rope_tables(t, q_scale):
    assert t % GRID_W == 0
    rows = t // GRID_W
    n_freq = MLA_ROPE // 4
    inv_freq = ROPE_THETA ** (-(2.0 * jnp.arange(n_freq, dtype=F32)) / (MLA_ROPE // 2))
    ar = jnp.arange(rows, dtype=F32)[:, None] * inv_freq
    ac = jnp.arange(GRID_W, dtype=F32)[:, None] * inv_freq
    by_row = lambda x: jnp.broadcast_to(x[:, None, :], (rows, GRID_W, n_freq)).reshape(t, n_freq)
    by_col = lambda x: jnp.broadcast_to(x[None, :, :], (rows, GRID_W, n_freq)).reshape(t, n_freq)
    cos_r, sin_r, cos_c, sin_c = jnp.cos(ar), jnp.sin(ar), jnp.cos(ac), jnp.sin(ac)
    cr, sr, cc, sc = by_row(cos_r), by_row(sin_r), by_col(cos_c), by_col(sin_c)
    km = jnp.concatenate([cr, cr, cc, cc, -sr, sr, -sc, sc], axis=1)
    by_row_t = lambda x: jnp.broadcast_to(x.T[:, :, None], (n_freq, rows, GRID_W)).reshape(n_freq, t)
    by_col_t = lambda x: jnp.broadcast_to(x.T[:, None, :], (n_freq, rows, GRID_W)).reshape(n_freq, t)
    crt, srt, cct, sct = by_row_t(cos_r), by_row_t(sin_r), by_col_t(cos_c), by_col_t(sin_c)
    qrt = q_scale * jnp.concatenate([crt, crt, cct, cct, -srt, srt, -sct, sct], axis=0)
    return qrt, km


def _no_rope_tables(t, q_scale):
    km = jnp.concatenate([jnp.ones((t, MLA_ROPE), F32), jnp.zeros((t, MLA_ROPE), F32)], axis=1)
    qrt = q_scale * jnp.concatenate([jnp.ones((MLA_ROPE, t), F32), jnp.zeros((MLA_ROPE, t), F32)], axis=0)
    return qrt, km


def _swap_perm():
    f = MLA_ROPE // 4
    return jnp.array(list(range(f, 2 * f)) + list(range(0, f)) + list(range(3 * f, 4 * f)) + list(range(2 * f, 3 * f)))


def _mla_layer(h_lat, h_ctx, mod_lat, mod_ctx, ln, w_in, q_norm, kv_norm, w_q_up, w_kv_up, w_out, *, alpha, last):
    bsz, seq, d = h_lat.shape
    tctx = h_ctx.shape[1]
    heads = d // 128
    q_rank, kv_rank = q_norm.shape[0], kv_norm.shape[0]
    tm_lat, tm_ctx = min(TOKEN_TILE, seq), min(TOKEN_TILE, tctx)
    q_scale = (MLA_NOPE + MLA_ROPE) ** -0.5 * math.log2(math.e)
    perm = _swap_perm()
    rope_cols = w_in[:, q_rank + kv_rank:]
    w_in_ext = jnp.concatenate([w_in, rope_cols[:, perm]], axis=1).astype(BF16)
    wq = w_q_up.reshape(q_rank, heads, MLA_NOPE + MLA_ROPE)
    wq_ext = jnp.concatenate([wq, wq[:, :, MLA_NOPE:][:, :, perm]], axis=2)
    wqt = wq_ext.reshape(q_rank, heads * MLA_QK).T.astype(BF16)
    wkv = w_kv_up.reshape(kv_rank, heads, MLA_NOPE + MLA_V)
    wk = wkv[:, :, :MLA_NOPE].reshape(kv_rank, -1).astype(BF16)
    wvt = wkv[:, :, MLA_NOPE:].reshape(kv_rank, -1).T.astype(BF16)
    qn, kvn = q_norm.reshape(1, -1), kv_norm.reshape(1, -1)
    qrt_l, km_l = _rope_tables(seq, q_scale)
    qrt_c, km_c = _no_rope_tables(tctx, q_scale)
    q_l, k_l, v_l = _mla_proj_call(h_lat, mod_lat, w_in_ext, qn, kvn, wqt, wk, wvt, qrt_l, km_l,
                                   tm=tm_lat, heads=heads, q_scale=q_scale)
    q_c, k_c, v_c = _mla_proj_call(h_ctx, mod_ctx, w_in_ext, qn, kvn, wqt, wk, wvt, qrt_c, km_c,
                                   tm=tm_ctx, heads=heads, q_scale=q_scale)
    o_l = _attn_call(q_l, k_c, v_c, k_l, v_l, tq=min(ATTN_Q_TILE, seq))
    w_out = w_out.astype(BF16)
    h_lat_new = _post_call((o_l,), h_lat, mod_lat, w_out, *ln, tm=tm_lat, alpha=alpha)
    if not last:
        o_c = _attn_call(q_c, k_c, v_c, tq=tm_ctx)
        h_ctx = _post_call((o_c,), h_ctx, mod_ctx, w_out, *ln, tm=tm_ctx, alpha=alpha)
    return h_lat_new, h_ctx


def _gdn_layer(h_lat, h_ctx, mod_lat, mod_ctx, ln, w_in, conv_w, a_log, dt_bias, norm_w, w_out, *, alpha, last):
    bsz, seq, d = h_lat.shape
    tctx = h_ctx.shape[1]
    k_heads = d // 128
    v_heads = 2 * k_heads
    kw, vw = k_heads * GDN_DK, v_heads * GDN_DV
    qkv_w = 2 * kw + vw
    tm_lat, tm_ctx = min(TOKEN_TILE, seq), min(TOKEN_TILE, tctx)
    w_ba = w_in[:, qkv_w + vw:].reshape(d, 2, 2, k_heads, 2)
    w_ba = jnp.transpose(w_ba, (0, 3, 2, 1, 4)).reshape(d, k_heads, 8)
    w_ba = jnp.pad(w_ba, ((0, 0), (0, 0), (0, LANES - 8))).reshape(d, k_heads * LANES)
    w_ext = jnp.concatenate([w_in[:, :qkv_w + vw].astype(BF16), w_ba.astype(BF16)], axis=1)

    def gate_lanes(p):
        p = jnp.transpose(p.astype(F32).reshape(2, k_heads, 2), (1, 0, 2)).reshape(k_heads, 4)
        p = jnp.pad(p, ((0, 0), (4, LANES - 8)))
        return p.reshape(1, k_heads * LANES)

    al_ext, dtb_ext = gate_lanes(a_log), gate_lanes(dt_bias)
    s = jnp.zeros((bsz, k_heads * 4, GDN_DK, GDN_DV), F32)
    streams = []
    for h_in, m_in, tm in ((h_ctx, mod_ctx, tm_ctx), (h_lat, mod_lat, tm_lat)):
        qh, kh, kt, vh, gates, gate_rows, z = _gdn_in_call(h_in, m_in, w_ext, conv_w, al_ext, dtb_ext,
                                                           tm=tm, k_heads=k_heads, z_w=vw)
        o_f, o_b, s = _gdn_scan_call(qh, kh, kt, vh, gates, gate_rows, s, hg=min(SCAN_KEY_HEADS, k_heads))
        streams.append((o_f, o_b, z))
    w_out = w_out.astype(BF16)
    nw = norm_w.reshape(1, -1)
    h_lat_new = _post_call(streams[1], h_lat, mod_lat, w_out, *ln, tm=tm_lat, alpha=alpha, gdn_norm=nw)
    if not last:
        h_ctx = _post_call(streams[0], h_ctx, mod_ctx, w_out, *ln, tm=tm_ctx, alpha=alpha, gdn_norm=nw)
    return h_lat_new, h_ctx


def kernel(x, c, ctx, c_ctx, w_mod, b_mod, ln1_g, ln1_b, ln2_g, ln2_b, w_ffn_in, w_ffn_out, mla_w_in, mla_q_norm, mla_kv_norm, mla_w_q_up, mla_w_kv_up, mla_w_out, gdn_w_in, gdn_conv, gdn_a_log, gdn_dt_bias, gdn_norm, gdn_w_out):
    bsz, _, d = x.shape
    depth = w_mod.shape[0]
    alpha = (2.0 * depth) ** 0.25

    cc = jnp.zeros((8, d), F32).at[:bsz].set(c).at[bsz].set(c_ctx)
    mod = _mod_call(cc, w_mod, b_mod)

    row = lambda a, i: a[i].reshape(1, -1)
    h_lat, h_ctx = x, ctx
    for i in range(depth):
        last = i == depth - 1
        mod_lat = mod[i, :bsz].reshape(bsz, 1, N_MOD * d)
        mod_ctx = jnp.broadcast_to(mod[i, bsz].reshape(1, 1, N_MOD * d), (bsz, 1, N_MOD * d))
        ln = (row(ln1_g, i), row(ln1_b, i), w_ffn_in[i].astype(BF16), w_ffn_out[i].astype(BF16),
              row(ln2_g, i), row(ln2_b, i))
        j = i // 2
        if i % 2 == 0:
            h_lat, h_ctx = _mla_layer(h_lat, h_ctx, mod_lat, mod_ctx, ln, mla_w_in[j], mla_q_norm[j],
                                      mla_kv_norm[j], mla_w_q_up[j], mla_w_kv_up[j], mla_w_out[j],
                                      alpha=alpha, last=last)
        else:
            h_lat, h_ctx = _gdn_layer(h_lat, h_ctx, mod_lat, mod_ctx, ln, gdn_w_in[j], gdn_conv[j],
                                      gdn_a_log[j], gdn_dt_bias[j], gdn_norm[j], gdn_w_out[j],
                                      alpha=alpha, last=last)
    return h_lat
```

```python
import functools
import math

import jax
import jax.numpy as jnp
from jax import lax
from jax.experimental import pallas as pl
from jax.experimental.pallas import tpu as pltpu

F32 = jnp.float32
BF16 = jnp.bfloat16

N_MOD = 6
LN_EPS = 1e-5
RMS_EPS = 1e-6
GRID_W = 64
ROPE_THETA = 10000.0

MLA_NOPE = 128
MLA_ROPE = 64
MLA_V = 128
MLA_QK = MLA_NOPE + 2 * MLA_ROPE

GDN_DK = 128
GDN_DV = 128
GDN_CONV = 5
CHUNK = 64
CONV_HALO = 8

V7X_VMEM_LIMIT = 56 * 1024 * 1024
LANES = 128

TOKEN_TILE = 256
ATTN_Q_TILE = 4096
ATTN_KEY_SLABS = 4
ATTN_CHUNKS_PER_TRIP = 1
SCAN_KEY_HEADS = 8


def _cparams(sem):
    return pltpu.CompilerParams(dimension_semantics=sem, vmem_limit_bytes=V7X_VMEM_LIMIT)


def _const_spec(shape):
    nd = len(shape)
    return pl.BlockSpec(shape, lambda *_: (0,) * nd, pipeline_mode=pl.Buffered(1))


def _dot(a, b):
    return jnp.dot(a, b, preferred_element_type=F32)


def _dot_nt(a, b):
    return lax.dot_general(a, b, (((1,), (1,)), ((), ())), preferred_element_type=F32)


def _split_bf16(a):
    hi = a.astype(BF16)
    lo = (a - hi.astype(F32)).astype(BF16)
    return hi, lo


def _silu(x):
    return x * jax.nn.sigmoid(x)


def _layer_norm(x, g, b):
    mu = jnp.mean(x, axis=-1, keepdims=True)
    xc = x - mu
    var = jnp.mean(xc * xc, axis=-1, keepdims=True)
    return xc * lax.rsqrt(var + LN_EPS) * g + b


def _rms_norm(x, g):
    return x * lax.rsqrt(jnp.mean(x * x, axis=-1, keepdims=True) + RMS_EPS) * g


def _mod_kernel(c_ref, w_ref, b_ref, o_ref):
    a = _silu(c_ref[...]).astype(BF16)
    o_ref[0] = _dot(a, w_ref[0].astype(BF16)) + b_ref[0]


def _mod_call(cc, w_mod, b_mod):
    depth, d, n = w_mod.shape
    tn = 1536
    assert n % tn == 0
    return pl.pallas_call(
        _mod_kernel,
        grid=(depth, n // tn),
        in_specs=[
            pl.BlockSpec((8, d), lambda l, j: (0, 0)),
            pl.BlockSpec((1, d, tn), lambda l, j: (l, 0, j)),
            pl.BlockSpec((1, 1, tn), lambda l, j: (l, 0, j)),
        ],
        out_specs=pl.BlockSpec((1, 8, tn), lambda l, j: (l, 0, j)),
        out_shape=jax.ShapeDtypeStruct((depth, 8, n), F32),
        compiler_params=_cparams(("parallel", "parallel")),
        name="mod",
    )(cc, w_mod, b_mod.reshape(depth, 1, n))


def _mod_spec(d, j):
    return pl.BlockSpec((1, 1, d), lambda b, i: (b, 0, j))


VT_ROWS = MLA_V + 16


def _mla_proj_kernel(h_ref, sh_ref, sc_ref, win_ref, qn_ref, kvn_ref, wqt_ref, wk_ref, wvt_ref,
                     qrt_ref, km_ref, qt_ref, k_ref, vt_ref, *, heads, q_rank, kv_rank, q_scale):
    x = h_ref[0]
    tm = x.shape[0]
    u = x * (1.0 + sc_ref[0]) + sh_ref[0]
    lat = _dot(u.astype(BF16), win_ref[...])
    cq = _rms_norm(lat[:, :q_rank], qn_ref[...]).astype(BF16)
    ckv = _rms_norm(lat[:, q_rank:q_rank + kv_rank], kvn_ref[...]).astype(BF16)
    kr2 = lat[:, q_rank + kv_rank:]
    qt_all = _dot_nt(wqt_ref[...], cq)
    k_all = _dot(ckv, wk_ref[...])
    vt_all = _dot_nt(wvt_ref[...], ckv)
    qrt = qrt_ref[...]
    t = kr2 * km_ref[...]
    krot2 = (t + pltpu.roll(t, MLA_ROPE, axis=1)).astype(BF16)
    ones_rows = jnp.where(lax.broadcasted_iota(jnp.int32, (VT_ROWS - MLA_V, tm), 0) == 0, 1.0, 0.0).astype(BF16)
    for hh in range(heads):
        qh = qt_all[hh * MLA_QK:(hh + 1) * MLA_QK, :]
        qt_ref[0, hh, :MLA_NOPE, :] = (qh[:MLA_NOPE] * q_scale).astype(BF16)
        qt_ref[0, hh, MLA_NOPE:, :] = (qh[MLA_NOPE:] * qrt).astype(BF16)
        k_ref[0, hh, :, :MLA_NOPE] = k_all[:, hh * MLA_NOPE:(hh + 1) * MLA_NOPE].astype(BF16)
        k_ref[0, hh, :, MLA_NOPE:] = krot2
        vt_ref[0, hh, 0, :MLA_V, :] = vt_all[hh * MLA_V:(hh + 1) * MLA_V, :].astype(BF16)
        vt_ref[0, hh, 0, MLA_V:, :] = ones_rows


def _mla_proj_call(h, mod, w_in, qn, kvn, wqt, wk, wvt, qrt, km, *, tm, heads, q_scale):
    b, t, d = h.shape
    q_rank, kv_rank = qn.shape[1], kvn.shape[1]
    assert t % tm == 0
    kern = functools.partial(_mla_proj_kernel, heads=heads, q_rank=q_rank, kv_rank=kv_rank, q_scale=q_scale)
    return pl.pallas_call(
        kern,
        grid=(b, t // tm),
        in_specs=[
            pl.BlockSpec((1, tm, d), lambda bb, i: (bb, i, 0)),
            _mod_spec(d, 0), _mod_spec(d, 1),
            _const_spec(w_in.shape), _const_spec(qn.shape), _const_spec(kvn.shape),
            _const_spec(wqt.shape), _const_spec(wk.shape), _const_spec(wvt.shape),
            pl.BlockSpec((2 * MLA_ROPE, tm), lambda bb, i: (0, i)),
            pl.BlockSpec((tm, 2 * MLA_ROPE), lambda bb, i: (i, 0)),
        ],
        out_specs=[
            pl.BlockSpec((1, heads, MLA_QK, tm), lambda bb, i: (bb, 0, 0, i)),
            pl.BlockSpec((1, heads, tm, MLA_QK), lambda bb, i: (bb, 0, i, 0)),
            pl.BlockSpec((1, heads, 1, VT_ROWS, tm), lambda bb, i: (bb, 0, i, 0, 0)),
        ],
        out_shape=[
            jax.ShapeDtypeStruct((b, heads, MLA_QK, t), BF16),
            jax.ShapeDtypeStruct((b, heads, t, MLA_QK), BF16),
            jax.ShapeDtypeStruct((b, heads, t // tm, VT_ROWS, tm), BF16),
        ],
        compiler_params=_cparams(("parallel", "parallel")),
        name="mla_proj",
    )(h, mod, mod, w_in, qn, kvn, wqt, wk, wvt, qrt, km)


QGROUP = 256


def _attn_kernel(*refs, slabs_per_step, n_lat, n_groups, chunks_per_trip):
    if n_lat:
        qt_ref, kc_ref, vtc_ref, kl_ref, vtl_ref, o_ref, m_ref, acc_ref, st_ref = refs
    else:
        qt_ref, kc_ref, vtc_ref, o_ref, m_ref, acc_ref = refs

    def scores(kblk, g):
        return _dot(kblk, qt_ref[0, 0, :, g * QGROUP:(g + 1) * QGROUP])

    def softmax_pv(st, vt_slabs, g, first):
        slab = vt_slabs[0].shape[1]
        cols = slice(g * QGROUP, (g + 1) * QGROUP)
        cmax = jnp.max(st, axis=0, keepdims=True)
        if first:
            m_new = cmax
        else:
            m_old = m_ref[:, cols]
            m_new = jnp.maximum(m_old, cmax)
        pt = jnp.exp2(st - m_new).astype(BF16)
        pv = None
        for s, vt in enumerate(vt_slabs):
            term = _dot(vt, pt[s * slab:(s + 1) * slab, :])
            pv = term if pv is None else pv + term
        if first:
            acc_ref[:, cols] = pv
        else:
            acc_ref[:, cols] = acc_ref[:, cols] * jnp.exp2(m_old - m_new) + pv
        m_ref[:, cols] = m_new

    kc = kc_ref[0, 0]
    vtc = [vtc_ref[0, 0, s] for s in range(vtc_ref.shape[2])]
    sts = [scores(kc, g) for g in range(n_groups)]
    if n_lat:
        slab = vtl_ref.shape[4]
        tk = slabs_per_step * slab
        k0 = kl_ref[0, 0, pl.ds(0, tk), :]
        nxt = scores(k0, 0)
        for g in range(n_groups):
            cur, nxt = nxt, (scores(k0, g + 1) if g + 1 < n_groups else None)
            softmax_pv(sts[g], vtc, g, True)
            st_ref[g] = cur
    else:
        for g in range(n_groups):
            softmax_pv(sts[g], vtc, g, True)
    if n_lat:

        def chunk_step(j):
            off = pl.multiple_of(jnp.minimum(j + 1, n_lat - 1) * tk, tk)
            k_next = kl_ref[0, 0, pl.ds(off, tk), :]
            vts = [vtl_ref[0, 0, j * slabs_per_step + s] for s in range(slabs_per_step)]
            nxt = scores(k_next, 0)
            for g in range(n_groups):
                cur, nxt = nxt, (scores(k_next, g + 1) if g + 1 < n_groups else None)
                softmax_pv(st_ref[g], vts, g, False)
                st_ref[g] = cur

        def body(jj, carry):
            for u in range(chunks_per_trip):
                chunk_step(jj * chunks_per_trip + u)
            return carry
        lax.fori_loop(0, n_lat // chunks_per_trip, body, 0)
    acc = acc_ref[...]
    ot = acc[:MLA_V] * (1.0 / acc[MLA_V:MLA_V + 1])
    o_ref[0] = ot.T.astype(o_ref.dtype)


def _attn_call(qt, k_ctx, vt_ctx, k_lat=None, vt_lat=None, *, tq, slabs_per_step=ATTN_KEY_SLABS):
    b, h, dqk, t = qt.shape
    tc = k_ctx.shape[2]
    assert t % tq == 0 and tq % QGROUP == 0
    n_lat = 0
    in_specs = [
        pl.BlockSpec((1, 1, dqk, tq), lambda bb, hh, i: (bb, hh, 0, i)),
        pl.BlockSpec((1, 1, tc, dqk), lambda bb, hh, i: (bb, hh, 0, 0)),
        pl.BlockSpec((1, 1) + vt_ctx.shape[2:], lambda bb, hh, i: (bb, hh, 0, 0, 0)),
    ]
    args = [qt, k_ctx, vt_ctx]
    if k_lat is not None:
        tl = k_lat.shape[2]
        ns = vt_lat.shape[2]
        assert ns % slabs_per_step == 0
        n_lat = ns // slabs_per_step
        in_specs += [
            pl.BlockSpec((1, 1, tl, dqk), lambda bb, hh, i: (bb, hh, 0, 0)),
            pl.BlockSpec((1, 1) + vt_lat.shape[2:], lambda bb, hh, i: (bb, hh, 0, 0, 0)),
        ]
        args += [k_lat, vt_lat]
    n_groups = tq // QGROUP
    chunks_per_trip = ATTN_CHUNKS_PER_TRIP if n_lat % ATTN_CHUNKS_PER_TRIP == 0 else 1
    kern = functools.partial(_attn_kernel, slabs_per_step=slabs_per_step, n_lat=n_lat, n_groups=n_groups,
                             chunks_per_trip=chunks_per_trip)
    scratch = [pltpu.VMEM((1, tq), F32), pltpu.VMEM((VT_ROWS, tq), F32)]
    if n_lat:
        scratch.append(pltpu.VMEM((n_groups, slabs_per_step * vt_lat.shape[4], QGROUP), F32))
    return pl.pallas_call(
        kern,
        grid=(b, h, t // tq),
        in_specs=in_specs,
        out_specs=pl.BlockSpec((1, tq, MLA_V), lambda bb, hh, i: (bb, i, hh)),
        out_shape=jax.ShapeDtypeStruct((b, t, h * MLA_V), BF16),
        scratch_shapes=scratch,
        compiler_params=_cparams(("parallel", "parallel", "arbitrary")),
        name="attn_lat" if n_lat else "attn_ctx",
    )(*args)


POST_SPLIT = 2


def _post_tail(y_in_rows, h_ref, g1_ref, sh2_ref, sc2_ref, g2_ref, wout_ref, l1g_ref, l1b_ref,
               wfi_ref, wfo_ref, l2g_ref, l2b_ref, o_ref, *, alpha, d_ff):
    tm = h_ref.shape[1]
    rows_per = tm // POST_SPLIT

    def stages(r0):
        rows = slice(r0, r0 + rows_per)
        y_in = y_in_rows(rows)
        yield
        y = _dot(y_in, wout_ref[...])
        yield
        h1 = _layer_norm(alpha * h_ref[0, rows] + g1_ref[0] * y, l1g_ref[...], l1b_ref[...])
        u2 = (h1 * (1.0 + sc2_ref[0]) + sh2_ref[0]).astype(BF16)
        yield
        gu = _dot(u2, wfi_ref[...])
        yield
        act = (_silu(gu[:, :d_ff]) * gu[:, d_ff:]).astype(BF16)
        yield
        f = _dot(act, wfo_ref[...])
        yield
        o_ref[0, rows] = _layer_norm(alpha * h1 + g2_ref[0] * f, l2g_ref[...], l2b_ref[...])

    gens = [stages(r0) for r0 in range(0, tm, rows_per)]
    live = []
    while gens or live:
        if gens:
            live.append(gens.pop(0))
            for _ in range(2):
                next(live[-1], None)
        for g in list(live):
            if next(g, "done") == "done":
                live.remove(g)


def _post_mla_kernel(a_ref, *rest, alpha, d_ff):
    _post_tail(lambda rows: a_ref[0, rows], *rest, alpha=alpha, d_ff=d_ff)


def _post_gdn_kernel(of_ref, ob_ref, z_ref, nw_ref, *rest, alpha, d_ff, k_heads):
    nw = nw_ref[...]

    def y_in_rows(rows):
        parts = []
        for kh in range(k_heads):
            o2 = of_ref[0, kh, rows] + ob_ref[0, kh, rows]
            for v in range(2):
                o = o2[:, v * GDN_DV:(v + 1) * GDN_DV]
                vh = 2 * kh + v
                z = z_ref[0, rows, vh * GDN_DV:(vh + 1) * GDN_DV]
                parts.append((_rms_norm(o, nw) * _silu(z)).astype(BF16))
        return jnp.concatenate(parts, axis=1)

    _post_tail(y_in_rows, *rest, alpha=alpha, d_ff=d_ff)


def _post_call(mixer_in, h, mod, w_out, l1g, l1b, wfi, wfo, l2g, l2b, *, tm, alpha, gdn_norm=None):
    b, t, d = h.shape
    d_ff = wfo.shape[0]
    assert t % tm == 0
    row = lambda w: pl.BlockSpec((1, tm, w), lambda bb, i: (bb, i, 0))
    if gdn_norm is None:
        (a,) = mixer_in
        kern = functools.partial(_post_mla_kernel, alpha=alpha, d_ff=d_ff)
        head_specs = [row(a.shape[2])]
        head_args = [a]
    else:
        o_f, o_b, z = mixer_in
        kh = o_f.shape[1]
        kern = functools.partial(_post_gdn_kernel, alpha=alpha, d_ff=d_ff, k_heads=kh)
        hm = pl.BlockSpec((1, kh, tm, 2 * GDN_DV), lambda bb, i: (bb, 0, i, 0))
        head_specs = [hm, hm, row(z.shape[2]), _const_spec(gdn_norm.shape)]
        head_args = [o_f, o_b, z, gdn_norm]
    return pl.pallas_call(
        kern,
        grid=(b, t // tm),
        in_specs=head_specs + [
            row(d),
            _mod_spec(d, 2), _mod_spec(d, 3), _mod_spec(d, 4), _mod_spec(d, 5),
            _const_spec(w_out.shape), _const_spec(l1g.shape), _const_spec(l1b.shape),
            _const_spec(wfi.shape), _const_spec(wfo.shape),
            _const_spec(l2g.shape), _const_spec(l2b.shape),
        ],
        out_specs=row(d),
        out_shape=jax.ShapeDtypeStruct((b, t, d), F32),
        compiler_params=_cparams(("parallel", "parallel")),
        name="post_mla" if gdn_norm is None else "post_gdn",
    )(*head_args, h, mod, mod, mod, mod, w_out, l1g, l1b, wfi, wfo, l2g, l2b)


def _chunk_cumsum(g, suffix):
    n = g.shape[0]
    pos = lax.broadcasted_iota(jnp.int32, g.shape, 0) % CHUNK
    s = 1
    while s < CHUNK:
        if suffix:
            shifted = pltpu.roll(g, n - s, axis=0)
            ok = pos < CHUNK - s
        else:
            shifted = pltpu.roll(g, s, axis=0)
            ok = pos >= s
        g = g + jnp.where(ok, shifted, 0.0)
        s *= 2
    return g


CONV_ROWS = 256
CONV_COLS = 256


def _gdn_in_kernel(hp_ref, hc_ref, hn_ref, sh_ref, sc_ref, w_ref, wba_ref, cw_ref, al_ref, dtb_ref,
                   q_ref, k_ref, kt_ref, v_ref, gt_ref, gr_ref, z_ref, xe_ref, y_ref, *, k_heads, tm, z_w):
    i = pl.program_id(1)
    n_i = pl.num_programs(1)
    kw = k_heads * GDN_DK
    c_all = xe_ref.shape[1]
    n_chunks = tm // CHUNK
    half = GDN_CONV // 2
    h_ext = jnp.concatenate([hp_ref[0], hc_ref[0], hn_ref[0]], axis=0)
    u = (h_ext * (1.0 + sc_ref[0]) + sh_ref[0]).astype(BF16)
    keep_prev = jnp.where(i > 0, 1.0, 0.0)
    keep_next = jnp.where(i < n_i - 1, 1.0, 0.0)

    def project(c0):
        return _dot(u, w_ref[:, c0:c0 + CONV_COLS])

    def conv_tile(p, c0):
        cols = slice(c0, c0 + CONV_COLS)
        xe_ref[0:CONV_HALO, cols] = p[0:CONV_HALO] * keep_prev
        xe_ref[CONV_HALO:CONV_HALO + tm, cols] = p[CONV_HALO:CONV_HALO + tm]
        xe_ref[CONV_HALO + tm:, cols] = p[CONV_HALO + tm:] * keep_next
        w = cw_ref[:, cols]
        for r0 in range(0, tm, CONV_ROWS):
            acc = None
            for j in range(GDN_CONV):
                start = CONV_HALO - half + j + r0
                term = xe_ref[start:start + CONV_ROWS, cols] * w[j:j + 1, :]
                acc = term if acc is None else acc + term
            y_ref[r0:r0 + CONV_ROWS, cols] = _silu(acc)

    tiles = list(range(0, c_all, CONV_COLS))
    nxt = project(tiles[0])
    for t_idx, c0 in enumerate(tiles):
        cur = nxt
        if t_idx + 1 < len(tiles):
            nxt = project(tiles[t_idx + 1])
        else:
            u_cur = u[CONV_HALO:CONV_HALO + tm]
            nxt = _dot(u_cur, w_ref[:, c_all:])
            ba = _dot(u_cur, wba_ref[...])
        conv_tile(cur, c0)
    z_ref[0] = nxt
    for kh in range(k_heads):
        qh = y_ref[:, kh * GDN_DK:(kh + 1) * GDN_DK]
        q_ref[0, kh] = (qh * lax.rsqrt(jnp.sum(qh * qh, axis=-1, keepdims=True) + RMS_EPS)
                        * (GDN_DK ** -0.5)).astype(BF16)
        kk = y_ref[:, kw + kh * GDN_DK:kw + (kh + 1) * GDN_DK]
        kn = kk * lax.rsqrt(jnp.sum(kk * kk, axis=-1, keepdims=True) + RMS_EPS)
        k_ref[0, kh] = kn.astype(BF16)
        knt = kn.T.astype(BF16)
        for n in range(n_chunks):
            kt_ref[0, kh, n] = knt[:, n * CHUNK:(n + 1) * CHUNK]
        v_ref[0, kh] = y_ref[:, 2 * kw + kh * 2 * GDN_DV:2 * kw + (kh + 1) * 2 * GDN_DV].astype(BF16)
    lane = lax.broadcasted_iota(jnp.int32, ba.shape, 1) % LANES
    beta = jax.nn.sigmoid(ba)
    g = -jnp.exp(al_ref[...]) * jax.nn.softplus(ba + dtb_ref[...])
    g = jnp.where((lane >= 4) & (lane < 8), g, 0.0)
    gcum = jnp.where(lane < 6, _chunk_cumsum(g, False), _chunk_cumsum(g, True))
    gates = jnp.where(lane < 4, beta, gcum)
    for kh in range(k_heads):
        gk = gates[:, kh * LANES:(kh + 1) * LANES]
        gt_ref[0, kh] = gk
        rows = gk.T[0:8, :]
        for n in range(n_chunks):
            piece = rows[:, n * CHUNK:(n + 1) * CHUNK]
            gr_ref[0, kh, n] = jnp.concatenate([piece, piece], axis=1)


def _gdn_in_call(h, mod, w, w_ba, conv_w, a_log_ext, dtb_ext, *, tm, k_heads, z_w):
    b, t, d = h.shape
    c = conv_w.shape[1]
    assert t % tm == 0 and tm % CHUNK == 0 and tm % CONV_HALO == 0
    assert tm % CONV_ROWS == 0 and c % CONV_COLS == 0
    r = tm // CONV_HALO
    nblk8 = t // CONV_HALO
    ncb = tm // CHUNK
    kern = functools.partial(_gdn_in_kernel, k_heads=k_heads, tm=tm, z_w=z_w)
    hm = lambda wd: pl.BlockSpec((1, k_heads, tm, wd), lambda bb, i: (bb, 0, i, 0))
    per_chunk = lambda rr, wd: pl.BlockSpec((1, k_heads, ncb, rr, wd), lambda bb, i: (bb, 0, i, 0, 0))
    return pl.pallas_call(
        kern,
        grid=(b, t // tm),
        in_specs=[
            pl.BlockSpec((1, CONV_HALO, d), lambda bb, i: (bb, jnp.maximum(i * r - 1, 0), 0)),
            pl.BlockSpec((1, tm, d), lambda bb, i: (bb, i, 0)),
            pl.BlockSpec((1, CONV_HALO, d), lambda bb, i: (bb, jnp.minimum((i + 1) * r, nblk8 - 1), 0)),
            _mod_spec(d, 0), _mod_spec(d, 1),
            _const_spec(w.shape), _const_spec(w_ba.shape), _const_spec(conv_w.shape),
            _const_spec(a_log_ext.shape), _const_spec(dtb_ext.shape),
        ],
        out_specs=[hm(GDN_DK), hm(GDN_DK), per_chunk(GDN_DK, CHUNK), hm(2 * GDN_DV), hm(LANES),
                   per_chunk(8, LANES), pl.BlockSpec((1, tm, z_w), lambda bb, i: (bb, i, 0))],
        out_shape=[
            jax.ShapeDtypeStruct((b, k_heads, t, GDN_DK), BF16),
            jax.ShapeDtypeStruct((b, k_heads, t, GDN_DK), BF16),
            jax.ShapeDtypeStruct((b, k_heads, t // CHUNK, GDN_DK, CHUNK), BF16),
            jax.ShapeDtypeStruct((b, k_heads, t, 2 * GDN_DV), BF16),
            jax.ShapeDtypeStruct((b, k_heads, t, LANES), F32),
            jax.ShapeDtypeStruct((b, k_heads, t // CHUNK, 8, LANES), F32),
            jax.ShapeDtypeStruct((b, t, z_w), F32),
        ],
        scratch_shapes=[pltpu.VMEM((tm + 2 * CONV_HALO, c), F32), pltpu.VMEM((tm, c), F32)],
        compiler_params=_cparams(("parallel", "parallel")),
        name="gdn_in",
    )(h, h, h, mod, mod, w, w_ba, conv_w, a_log_ext, dtb_ext)


def _dot_split(a_hi, a_lo, b_hi, b_lo):
    return _dot(jnp.concatenate([a_hi, a_lo], axis=1), jnp.concatenate([b_hi, b_lo, b_hi, b_lo], axis=0))


def _gdn_scan_kernel(qf_ref, kf_ref, ktf_ref, vf_ref, gf_ref, grf_ref,
                     qb_ref, kb_ref, ktb_ref, vb_ref, gb_ref, grb_ref, s0_ref,
                     of_ref, ob_ref, sfin_ref, s_ref, *, hg):
    n = pl.program_id(2)

    @pl.when(n == 0)
    def _():
        s_ref[...] = s0_ref[0]

    c = CHUNK
    row = lax.broadcasted_iota(jnp.int32, (c, LANES), 0)
    col = lax.broadcasted_iota(jnp.int32, (c, LANES), 1) % c
    dirs = ((qf_ref, kf_ref, ktf_ref, vf_ref, gf_ref, grf_ref), (qb_ref, kb_ref, ktb_ref, vb_ref, gb_ref, grb_ref))

    eye = jnp.where(row == col, 1.0, 0.0)
    zero_rows = jnp.zeros((c, 2 * GDN_DV), BF16)
    tri_masks = ((row >= col, row > col), (row <= col, row < col))
    pairs = []
    blk_r, blk_c = row, col
    b = 1
    while b < c:
        pair = (blk_r // 2 == blk_c // 2) & (blk_r != blk_c)
        pairs.append(pair if b == 1 else jnp.where(pair, 1.0, 0.0).astype(BF16))
        blk_r, blk_c = blk_r // 2, blk_c // 2
        b *= 2

    def stages(heads):
        chains = []
        for j in heads:
            for d in range(2):
                q_ref, k_ref, kt_ref, v_ref, g_ref, gr_ref = dirs[d]
                qc, kc, v2, gt = q_ref[0, j], k_ref[0, j], v_ref[0, j], g_ref[0, j]
                gtt = gr_ref[0, j, 0]
                ktc = kt_ref[0, j, 0]
                kq = _dot_nt(jnp.concatenate([kc, qc], axis=0), jnp.concatenate([kc, kc], axis=0))
                for v in range(2):
                    chains.append(dict(j=j, d=d, v=v, qc=qc, kc=kc, vv=v2[:, v * GDN_DV:(v + 1) * GDN_DV],
                                       gt=gt, gtt=gtt, ktc=ktc, kk=kq[:c], qk=kq[c:]))
            yield
        for i, ch in enumerate(chains):
            slot = 2 * ch["d"] + ch["v"]
            incl, strict = tri_masks[ch["d"]]
            beta_col = jnp.broadcast_to(ch["gt"][:, slot:slot + 1], (c, LANES))
            g_col = jnp.broadcast_to(ch["gt"][:, 4 + slot:5 + slot], (c, LANES))
            g_row = ch["gtt"][4 + slot:5 + slot, :]
            decay = jnp.where(incl, jnp.exp(jnp.where(incl, g_col - g_row, 0.0)), 0.0)
            eg_col = jnp.exp(g_col)
            ch["g_row"] = g_row
            lmat = jnp.where(strict, ch["kk"] * beta_col * decay, 0.0)
            ch.update(
                lm=lmat.astype(BF16), tinv=eye - jnp.where(pairs[0], lmat, 0.0),
                attn=(ch["qk"] * decay)[:, :c].astype(BF16),
                eg_col=eg_col, g_col=g_col,
                rhs=jnp.concatenate([ch["vv"].astype(F32) * beta_col,
                                     ch["kc"].astype(F32) * (beta_col * eg_col)], axis=1).astype(BF16))
            if i % 4 == 3:
                yield
        for pair_bf in pairs[1:]:
            for ch in chains:
                ch["t_hi"], ch["t_lo"] = _split_bf16(ch["tinv"])
                ch["w"] = _dot(ch["lm"] * pair_bf, jnp.concatenate([ch["t_hi"], ch["t_lo"]], axis=0))
            yield
            for ch in chains:
                w_hi, w_lo = _split_bf16(ch["w"])
                ch["tinv"] = ch["tinv"] - _dot_split(ch["t_hi"], ch["t_lo"], w_hi, w_lo)
            yield
        for ch in chains:
            t_hi, t_lo = _split_bf16(ch["tinv"])
            r = ch["rhs"]
            ch["x"] = _dot(jnp.concatenate([t_hi, t_lo], axis=1),
                           jnp.concatenate([r, zero_rows, r, zero_rows], axis=0))
        yield
        for ch in chains:
            ch["s_idx"] = ch["j"] * 4 + 2 * ch["d"] + ch["v"]
            ch["s_old"] = s_ref[ch["s_idx"]]
            ch["wq"] = _dot(jnp.concatenate([ch["x"][:, GDN_DV:].astype(BF16), ch["qc"]], axis=0),
                            ch["s_old"].astype(BF16))
        yield
        for ch in chains:
            g_col = ch["g_col"]
            ch["g_last"] = g_col[0:1, :] if ch["d"] == 1 else g_col[c - 1:c, :]
            v_new = (ch["x"][:, :GDN_DV] - ch["wq"][:c]).astype(BF16)
            dec_row = jnp.exp(ch["g_last"] - ch["g_row"])[:, :c]
            k_dec_t = (ch["ktc"].astype(F32) * dec_row).astype(BF16)
            ch["av"] = _dot(ch["attn"], v_new)
            ch["kv"] = _dot(k_dec_t, v_new)
        yield
        outs = {}
        for ch in chains:
            s_ref[ch["s_idx"]] = ch["s_old"] * jnp.exp(ch["g_last"]) + ch["kv"]
            outs[ch["j"], ch["d"], ch["v"]] = ch["eg_col"] * ch["wq"][c:] + ch["av"]
        for j in heads:
            for d, o_ref in enumerate((of_ref, ob_ref)):
                o_ref[0, j] = jnp.concatenate([outs[j, d, 0], outs[j, d, 1]], axis=1)

    for _ in stages(list(range(hg))):
        pass

    @pl.when(n == pl.num_programs(2) - 1)
    def _():
        sfin_ref[0] = s_ref[...]


def _gdn_scan_call(q, k, kt, v, gates, gate_rows, s0, *, hg):
    b, kh, t, _ = q.shape
    assert kh % hg == 0 and t % CHUNK == 0
    nc = t // CHUNK
    ng = kh // hg
    fwd = lambda w: pl.BlockSpec((1, hg, CHUNK, w), lambda bb, g, n: (bb, g, n, 0))
    bwd = lambda w: pl.BlockSpec((1, hg, CHUNK, w), lambda bb, g, n: (bb, g, nc - 1 - n, 0))
    fwd_c = lambda r, w: pl.BlockSpec((1, hg, 1, r, w), lambda bb, g, n: (bb, g, n, 0, 0))
    bwd_c = lambda r, w: pl.BlockSpec((1, hg, 1, r, w), lambda bb, g, n: (bb, g, nc - 1 - n, 0, 0))
    st = pl.BlockSpec((1, hg * 4, GDN_DK, GDN_DV), lambda bb, g, n: (bb, g, 0, 0))
    kern = functools.partial(_gdn_scan_kernel, hg=hg)
    return pl.pallas_call(
        kern,
        grid=(b, ng, nc),
        in_specs=[fwd(GDN_DK), fwd(GDN_DK), fwd_c(GDN_DK, CHUNK), fwd(2 * GDN_DV), fwd(LANES), fwd_c(8, LANES),
                  bwd(GDN_DK), bwd(GDN_DK), bwd_c(GDN_DK, CHUNK), bwd(2 * GDN_DV), bwd(LANES), bwd_c(8, LANES),
                  st],
        out_specs=[fwd(2 * GDN_DV), bwd(2 * GDN_DV), st],
        out_shape=[
            jax.ShapeDtypeStruct((b, kh, t, 2 * GDN_DV), F32),
            jax.ShapeDtypeStruct((b, kh, t, 2 * GDN_DV), F32),
            jax.ShapeDtypeStruct(s0.shape, F32),
        ],
        scratch_shapes=[pltpu.VMEM((hg * 4, GDN_DK, GDN_DV), F32)],
        compiler_params=_cparams(("parallel", "parallel", "arbitrary")),
        name="gdn_scan",
    )(q, k, kt, v, gates, gate_rows, q, k, kt, v, gates, gate_rows, s0)


def _rope_tables(t, q_scale):
    assert t % GRID_W == 0
    rows = t // GRID_W
    n_freq = MLA_ROPE // 4
    inv_freq = ROPE_THETA ** (-(2.0 * jnp.arange(n_freq, dtype=F32)) / (MLA_ROPE // 2))
    ar = jnp.arange(rows, dtype=F32)[:, None] * inv_freq
    ac = jnp.arange(GRID_W, dtype=F32)[:, None] * inv_freq
    by_row = lambda x: jnp.broadcast_to(x[:, None, :], (rows, GRID_W, n_freq)).reshape(t, n_freq)
    by_col = lambda x: jnp.broadcast_to(x[None, :, :], (rows, GRID_W, n_freq)).reshape(t, n_freq)
    cos_r, sin_r, cos_c, sin_c = jnp.cos(ar), jnp.sin(ar), jnp.cos(ac), jnp.sin(ac)
    cr, sr, cc, sc = by_row(cos_r), by_row(sin_r), by_col(cos_c), by_col(sin_c)
    km = jnp.concatenate([cr, cr, cc, cc, -sr, sr, -sc, sc], axis=1)
    by_row_t = lambda x: jnp.broadcast_to(x.T[:, :, None], (n_freq, rows, GRID_W)).reshape(n_freq, t)
    by_col_t = lambda x: jnp.broadcast_to(x.T[:, None, :], (n_freq, rows, GRID_W)).reshape(n_freq, t)
    crt, srt, cct, sct = by_row_t(cos_r), by_row_t(sin_r), by_col_t(cos_c), by_col_t(sin_c)
    qrt = q_scale * jnp.concatenate([crt, crt, cct, cct, -srt, srt, -sct, sct], axis=0)
    return qrt, km


def _no_rope_tables(t, q_scale):
    km = jnp.concatenate([jnp.ones((t, MLA_ROPE), F32), jnp.zeros((t, MLA_ROPE), F32)], axis=1)
    qrt = q_scale * jnp.concatenate([jnp.ones((MLA_ROPE, t), F32), jnp.zeros((MLA_ROPE, t), F32)], axis=0)
    return qrt, km


def _swap_perm():
    f = MLA_ROPE // 4
    return jnp.array(list(range(f, 2 * f)) + list(range(0, f)) + list(range(3 * f, 4 * f)) + list(range(2 * f, 3 * f)))


def _mla_layer(h_lat, h_ctx, mod_lat, mod_ctx, ln, w_in, q_norm, kv_norm, w_q_up, w_kv_up, w_out, *, alpha, last):
    bsz, seq, d = h_lat.shape
    tctx = h_ctx.shape[1]
    heads = d // 128
    q_rank, kv_rank = q_norm.shape[0], kv_norm.shape[0]
    tm_lat, tm_ctx = min(TOKEN_TILE, seq), min(TOKEN_TILE, tctx)
    q_scale = (MLA_NOPE + MLA_ROPE) ** -0.5 * math.log2(math.e)
    perm = _swap_perm()
    rope_cols = w_in[:, q_rank + kv_rank:]
    w_in_ext = jnp.concatenate([w_in, rope_cols[:, perm]], axis=1).astype(BF16)
    wq = w_q_up.reshape(q_rank, heads, MLA_NOPE + MLA_ROPE)
    wq_ext = jnp.concatenate([wq, wq[:, :, MLA_NOPE:][:, :, perm]], axis=2)
    wqt = wq_ext.reshape(q_rank, heads * MLA_QK).T.astype(BF16)
    wkv = w_kv_up.reshape(kv_rank, heads, MLA_NOPE + MLA_V)
    wk = wkv[:, :, :MLA_NOPE].reshape(kv_rank, -1).astype(BF16)
    wvt = wkv[:, :, MLA_NOPE:].reshape(kv_rank, -1).T.astype(BF16)
    qn, kvn = q_norm.reshape(1, -1), kv_norm.reshape(1, -1)
    qrt_l, km_l = _rope_tables(seq, q_scale)
    qrt_c, km_c = _no_rope_tables(tctx, q_scale)
    q_l, k_l, v_l = _mla_proj_call(h_lat, mod_lat, w_in_ext, qn, kvn, wqt, wk, wvt, qrt_l, km_l,
                                   tm=tm_lat, heads=heads, q_scale=q_scale)
    q_c, k_c, v_c = _mla_proj_call(h_ctx, mod_ctx, w_in_ext, qn, kvn, wqt, wk, wvt, qrt_c, km_c,
                                   tm=tm_ctx, heads=heads, q_scale=q_scale)
    o_l = _attn_call(q_l, k_c, v_c, k_l, v_l, tq=min(ATTN_Q_TILE, seq))
    w_out = w_out.astype(BF16)
    h_lat_new = _post_call((o_l,), h_lat, mod_lat, w_out, *ln, tm=tm_lat, alpha=alpha)
    if not last:
        o_c = _attn_call(q_c, k_c, v_c, tq=tm_ctx)
        h_ctx = _post_call((o_c,), h_ctx, mod_ctx, w_out, *ln, tm=tm_ctx, alpha=alpha)
    return h_lat_new, h_ctx


def _gdn_layer(h_lat, h_ctx, mod_lat, mod_ctx, ln, w_in, conv_w, a_log, dt_bias, norm_w, w_out, *, alpha, last):
    bsz, seq, d = h_lat.shape
    tctx = h_ctx.shape[1]
    k_heads = d // 128
    v_heads = 2 * k_heads
    kw, vw = k_heads * GDN_DK, v_heads * GDN_DV
    qkv_w = 2 * kw + vw
    tm_lat, tm_ctx = min(TOKEN_TILE, seq), min(TOKEN_TILE, tctx)
    w_ba = w_in[:, qkv_w + vw:].reshape(d, 2, 2, k_heads, 2)
    w_ba = jnp.transpose(w_ba, (0, 3, 2, 1, 4)).reshape(d, k_heads, 8)
    w_ba = jnp.pad(w_ba, ((0, 0), (0, 0), (0, LANES - 8))).reshape(d, k_heads * LANES)
    w_main = w_in[:, :qkv_w + vw].astype(BF16)
    w_ba = w_ba.astype(BF16)

    def gate_lanes(p):
        p = jnp.transpose(p.astype(F32).reshape(2, k_heads, 2), (1, 0, 2)).reshape(k_heads, 4)
        p = jnp.pad(p, ((0, 0), (4, LANES - 8)))
        return p.reshape(1, k_heads * LANES)

    al_ext, dtb_ext = gate_lanes(a_log), gate_lanes(dt_bias)
    s = jnp.zeros((bsz, k_heads * 4, GDN_DK, GDN_DV), F32)
    streams = []
    for h_in, m_in, tm in ((h_ctx, mod_ctx, tm_ctx), (h_lat, mod_lat, tm_lat)):
        qh, kh, kt, vh, gates, gate_rows, z = _gdn_in_call(h_in, m_in, w_main, w_ba, conv_w, al_ext, dtb_ext,
                                                           tm=tm, k_heads=k_heads, z_w=vw)
        o_f, o_b, s = _gdn_scan_call(qh, kh, kt, vh, gates, gate_rows, s, hg=min(SCAN_KEY_HEADS, k_heads))
        streams.append((o_f, o_b, z))
    w_out = w_out.astype(BF16)
    nw = norm_w.reshape(1, -1)
    h_lat_new = _post_call(streams[1], h_lat, mod_lat, w_out, *ln, tm=tm_lat, alpha=alpha, gdn_norm=nw)
    if not last:
        h_ctx = _post_call(streams[0], h_ctx, mod_ctx, w_out, *ln, tm=tm_ctx, alpha=alpha, gdn_norm=nw)
    return h_lat_new, h_ctx


def kernel(x, c, ctx, c_ctx, w_mod, b_mod, ln1_g, ln1_b, ln2_g, ln2_b, w_ffn_in, w_ffn_out, mla_w_in, mla_q_norm, mla_kv_norm, mla_w_q_up, mla_w_kv_up, mla_w_out, gdn_w_in, gdn_conv, gdn_a_log, gdn_dt_bias, gdn_norm, gdn_w_out):
    bsz, _, d = x.shape
    depth = w_mod.shape[0]
    alpha = (2.0 * depth) ** 0.25

    cc = jnp.zeros((8, d), F32).at[:bsz].set(c).at[bsz].set(c_ctx)
    mod = _mod_call(cc, w_mod, b_mod)

    row = lambda a, i: a[i].reshape(1, -1)
    h_lat, h_ctx = x, ctx
    for i in range(depth):
        last = i == depth - 1
        mod_lat = mod[i, :bsz].reshape(bsz, 1, N_MOD * d)
        mod_ctx = jnp.broadcast_to(mod[i, bsz].reshape(1, 1, N_MOD * d), (bsz, 1, N_MOD * d))
        ln = (row(ln1_g, i), row(ln1_b, i), w_ffn_in[i].astype(BF16), w_ffn_out[i].astype(BF16),
              row(ln2_g, i), row(ln2_b, i))
        j = i // 2
        if i % 2 == 0:
            h_lat, h_ctx = _mla_layer(h_lat, h_ctx, mod_lat, mod_ctx, ln, mla_w_in[j], mla_q_norm[j],
                                      mla_kv_norm[j], mla_w_q_up[j], mla_w_kv_up[j], mla_w_out[j],
                                      alpha=alpha, last=last)
        else:
            h_lat, h_ctx = _gdn_layer(h_lat, h_ctx, mod_lat, mod_ctx, ln, gdn_w_in[j], gdn_conv[j],
                                      gdn_a_log[j], gdn_dt_bias[j], gdn_norm[j], gdn_w_out[j],
                                      alpha=alpha, last=last)
    return h_lat
```

```python
import functools
import math

import jax
import jax.numpy as jnp
from jax import lax
from jax.experimental import pallas as pl
from jax.experimental.pallas import tpu as pltpu

F32 = jnp.float32
BF16 = jnp.bfloat16

N_MOD = 6
LN_EPS = 1e-5
RMS_EPS = 1e-6
GRID_W = 64
ROPE_THETA = 10000.0

MLA_NOPE = 128
MLA_ROPE = 64
MLA_V = 128
MLA_QK = MLA_NOPE + 2 * MLA_ROPE

GDN_DK = 128
GDN_DV = 128
GDN_CONV = 5
CHUNK = 64
CONV_HALO = 8

V7X_VMEM_LIMIT = 56 * 1024 * 1024
LANES = 128

TOKEN_TILE = 256
ATTN_Q_TILE = 4096
ATTN_KEY_SLABS = 4
ATTN_CHUNKS_PER_TRIP = 1
SCAN_KEY_HEADS = 8


def _cparams(sem):
    return pltpu.CompilerParams(dimension_semantics=sem, vmem_limit_bytes=V7X_VMEM_LIMIT)


def _const_spec(shape):
    nd = len(shape)
    return pl.BlockSpec(shape, lambda *_: (0,) * nd, pipeline_mode=pl.Buffered(1))


def _dot(a, b):
    return jnp.dot(a, b, preferred_element_type=F32)


def _dot_nt(a, b):
    return lax.dot_general(a, b, (((1,), (1,)), ((), ())), preferred_element_type=F32)


def _split_bf16(a):
    hi = a.astype(BF16)
    lo = (a - hi.astype(F32)).astype(BF16)
    return hi, lo


def _silu(x):
    return x * jax.nn.sigmoid(x)


def _layer_norm(x, g, b):
    mu = jnp.mean(x, axis=-1, keepdims=True)
    xc = x - mu
    var = jnp.mean(xc * xc, axis=-1, keepdims=True)
    return xc * lax.rsqrt(var + LN_EPS) * g + b


def _rms_norm(x, g):
    return x * lax.rsqrt(jnp.mean(x * x, axis=-1, keepdims=True) + RMS_EPS) * g


def _mod_kernel(c_ref, w_ref, b_ref, o_ref):
    a = _silu(c_ref[...]).astype(BF16)
    o_ref[0] = _dot(a, w_ref[0].astype(BF16)) + b_ref[0]


def _mod_call(cc, w_mod, b_mod):
    depth, d, n = w_mod.shape
    tn = 1536
    assert n % tn == 0
    return pl.pallas_call(
        _mod_kernel,
        grid=(depth, n // tn),
        in_specs=[
            pl.BlockSpec((8, d), lambda l, j: (0, 0)),
            pl.BlockSpec((1, d, tn), lambda l, j: (l, 0, j)),
            pl.BlockSpec((1, 1, tn), lambda l, j: (l, 0, j)),
        ],
        out_specs=pl.BlockSpec((1, 8, tn), lambda l, j: (l, 0, j)),
        out_shape=jax.ShapeDtypeStruct((depth, 8, n), F32),
        compiler_params=_cparams(("parallel", "parallel")),
        name="mod",
    )(cc, w_mod, b_mod.reshape(depth, 1, n))


def _mod_spec(d, j):
    return pl.BlockSpec((1, 1, d), lambda b, i: (b, 0, j))


VT_ROWS = MLA_V + 16


def _mla_proj_kernel(h_ref, sh_ref, sc_ref, win_ref, qn_ref, kvn_ref, wqt_ref, wk_ref, wvt_ref,
                     qrt_ref, km_ref, qt_ref, k_ref, vt_ref, *, heads, q_rank, kv_rank, q_scale):
    x = h_ref[0]
    tm = x.shape[0]
    u = x * (1.0 + sc_ref[0]) + sh_ref[0]
    lat = _dot(u.astype(BF16), win_ref[...])
    cq = _rms_norm(lat[:, :q_rank], qn_ref[...]).astype(BF16)
    ckv = _rms_norm(lat[:, q_rank:q_rank + kv_rank], kvn_ref[...]).astype(BF16)
    kr2 = lat[:, q_rank + kv_rank:]
    qt_all = _dot_nt(wqt_ref[...], cq)
    k_all = _dot(ckv, wk_ref[...])
    vt_all = _dot_nt(wvt_ref[...], ckv)
    qrt = qrt_ref[...]
    t = kr2 * km_ref[...]
    krot2 = (t + pltpu.roll(t, MLA_ROPE, axis=1)).astype(BF16)
    ones_rows = jnp.where(lax.broadcasted_iota(jnp.int32, (VT_ROWS - MLA_V, tm), 0) == 0, 1.0, 0.0).astype(BF16)
    for hh in range(heads):
        qh = qt_all[hh * MLA_QK:(hh + 1) * MLA_QK, :]
        qt_ref[0, hh, :MLA_NOPE, :] = (qh[:MLA_NOPE] * q_scale).astype(BF16)
        qt_ref[0, hh, MLA_NOPE:, :] = (qh[MLA_NOPE:] * qrt).astype(BF16)
        k_ref[0, hh, :, :MLA_NOPE] = k_all[:, hh * MLA_NOPE:(hh + 1) * MLA_NOPE].astype(BF16)
        k_ref[0, hh, :, MLA_NOPE:] = krot2
        vt_ref[0, hh, 0, :MLA_V, :] = vt_all[hh * MLA_V:(hh + 1) * MLA_V, :].astype(BF16)
        vt_ref[0, hh, 0, MLA_V:, :] = ones_rows


def _mla_proj_call(h, mod, w_in, qn, kvn, wqt, wk, wvt, qrt, km, *, tm, heads, q_scale):
    b, t, d = h.shape
    q_rank, kv_rank = qn.shape[1], kvn.shape[1]
    assert t % tm == 0
    kern = functools.partial(_mla_proj_kernel, heads=heads, q_rank=q_rank, kv_rank=kv_rank, q_scale=q_scale)
    return pl.pallas_call(
        kern,
        grid=(b, t // tm),
        in_specs=[
            pl.BlockSpec((1, tm, d), lambda bb, i: (bb, i, 0)),
            _mod_spec(d, 0), _mod_spec(d, 1),
            _const_spec(w_in.shape), _const_spec(qn.shape), _const_spec(kvn.shape),
            _const_spec(wqt.shape), _const_spec(wk.shape), _const_spec(wvt.shape),
            pl.BlockSpec((2 * MLA_ROPE, tm), lambda bb, i: (0, i)),
            pl.BlockSpec((tm, 2 * MLA_ROPE), lambda bb, i: (i, 0)),
        ],
        out_specs=[
            pl.BlockSpec((1, heads, MLA_QK, tm), lambda bb, i: (bb, 0, 0, i)),
            pl.BlockSpec((1, heads, tm, MLA_QK), lambda bb, i: (bb, 0, i, 0)),
            pl.BlockSpec((1, heads, 1, VT_ROWS, tm), lambda bb, i: (bb, 0, i, 0, 0)),
        ],
        out_shape=[
            jax.ShapeDtypeStruct((b, heads, MLA_QK, t), BF16),
            jax.ShapeDtypeStruct((b, heads, t, MLA_QK), BF16),
            jax.ShapeDtypeStruct((b, heads, t // tm, VT_ROWS, tm), BF16),
        ],
        compiler_params=_cparams(("parallel", "parallel")),
        name="mla_proj",
    )(h, mod, mod, w_in, qn, kvn, wqt, wk, wvt, qrt, km)


QGROUP = 256


def _attn_kernel(*refs, slabs_per_step, n_lat, n_groups, chunks_per_trip):
    if n_lat:
        qt_ref, kc_ref, vtc_ref, kl_ref, vtl_ref, o_ref, m_ref, acc_ref, st_ref = refs
    else:
        qt_ref, kc_ref, vtc_ref, o_ref, m_ref, acc_ref = refs

    def scores(kblk, g):
        return _dot(kblk, qt_ref[0, 0, :, g * QGROUP:(g + 1) * QGROUP])

    def softmax_pv(st, vt_slabs, g, first):
        slab = vt_slabs[0].shape[1]
        cols = slice(g * QGROUP, (g + 1) * QGROUP)
        cmax = jnp.max(st, axis=0, keepdims=True)
        if first:
            m_new = cmax
        else:
            m_old = m_ref[:, cols]
            m_new = jnp.maximum(m_old, cmax)
        pt = jnp.exp2(st - m_new).astype(BF16)
        pv = None
        for s, vt in enumerate(vt_slabs):
            term = _dot(vt, pt[s * slab:(s + 1) * slab, :])
            pv = term if pv is None else pv + term
        if first:
            acc_ref[:, cols] = pv
        else:
            acc_ref[:, cols] = acc_ref[:, cols] * jnp.exp2(m_old - m_new) + pv
        m_ref[:, cols] = m_new

    kc = kc_ref[0, 0]
    vtc = [vtc_ref[0, 0, s] for s in range(vtc_ref.shape[2])]
    sts = [scores(kc, g) for g in range(n_groups)]
    if n_lat:
        slab = vtl_ref.shape[4]
        tk = slabs_per_step * slab
        k0 = kl_ref[0, 0, pl.ds(0, tk), :]
        nxt = scores(k0, 0)
        for g in range(n_groups):
            cur, nxt = nxt, (scores(k0, g + 1) if g + 1 < n_groups else None)
            softmax_pv(sts[g], vtc, g, True)
            st_ref[g] = cur
    else:
        for g in range(n_groups):
            softmax_pv(sts[g], vtc, g, True)
    if n_lat:

        def chunk_step(j):
            off = pl.multiple_of(jnp.minimum(j + 1, n_lat - 1) * tk, tk)
            k_next = kl_ref[0, 0, pl.ds(off, tk), :]
            vts = [vtl_ref[0, 0, j * slabs_per_step + s] for s in range(slabs_per_step)]
            nxt = scores(k_next, 0)
            for g in range(n_groups):
                cur, nxt = nxt, (scores(k_next, g + 1) if g + 1 < n_groups else None)
                softmax_pv(st_ref[g], vts, g, False)
                st_ref[g] = cur

        def body(jj, carry):
            for u in range(chunks_per_trip):
                chunk_step(jj * chunks_per_trip + u)
            return carry
        lax.fori_loop(0, n_lat // chunks_per_trip, body, 0)
    acc = acc_ref[...]
    ot = acc[:MLA_V] * (1.0 / acc[MLA_V:MLA_V + 1])
    o_ref[0] = ot.T.astype(o_ref.dtype)


def _attn_call(qt, k_ctx, vt_ctx, k_lat=None, vt_lat=None, *, tq, slabs_per_step=ATTN_KEY_SLABS):
    b, h, dqk, t = qt.shape
    tc = k_ctx.shape[2]
    assert t % tq == 0 and tq % QGROUP == 0
    n_lat = 0
    in_specs = [
        pl.BlockSpec((1, 1, dqk, tq), lambda bb, hh, i: (bb, hh, 0, i)),
        pl.BlockSpec((1, 1, tc, dqk), lambda bb, hh, i: (bb, hh, 0, 0)),
        pl.BlockSpec((1, 1) + vt_ctx.shape[2:], lambda bb, hh, i: (bb, hh, 0, 0, 0)),
    ]
    args = [qt, k_ctx, vt_ctx]
    if k_lat is not None:
        tl = k_lat.shape[2]
        ns = vt_lat.shape[2]
        assert ns % slabs_per_step == 0
        n_lat = ns // slabs_per_step
        in_specs += [
            pl.BlockSpec((1, 1, tl, dqk), lambda bb, hh, i: (bb, hh, 0, 0)),
            pl.BlockSpec((1, 1) + vt_lat.shape[2:], lambda bb, hh, i: (bb, hh, 0, 0, 0)),
        ]
        args += [k_lat, vt_lat]
    n_groups = tq // QGROUP
    chunks_per_trip = ATTN_CHUNKS_PER_TRIP if n_lat % ATTN_CHUNKS_PER_TRIP == 0 else 1
    kern = functools.partial(_attn_kernel, slabs_per_step=slabs_per_step, n_lat=n_lat, n_groups=n_groups,
                             chunks_per_trip=chunks_per_trip)
    scratch = [pltpu.VMEM((1, tq), F32), pltpu.VMEM((VT_ROWS, tq), F32)]
    if n_lat:
        scratch.append(pltpu.VMEM((n_groups, slabs_per_step * vt_lat.shape[4], QGROUP), F32))
    return pl.pallas_call(
        kern,
        grid=(b, h, t // tq),
        in_specs=in_specs,
        out_specs=pl.BlockSpec((1, tq, MLA_V), lambda bb, hh, i: (bb, i, hh)),
        out_shape=jax.ShapeDtypeStruct((b, t, h * MLA_V), BF16),
        scratch_shapes=scratch,
        compiler_params=_cparams(("parallel", "parallel", "arbitrary")),
        name="attn_lat" if n_lat else "attn_ctx",
    )(*args)


POST_SPLIT = 2


def _post_tail(y_in_rows, h_ref, g1_ref, sh2_ref, sc2_ref, g2_ref, wout_ref, l1g_ref, l1b_ref,
               wfi_ref, wfo_ref, l2g_ref, l2b_ref, o_ref, *, alpha, d_ff):
    tm = h_ref.shape[1]
    rows_per = tm // POST_SPLIT

    def stages(r0):
        rows = slice(r0, r0 + rows_per)
        y_in = y_in_rows(rows)
        yield
        y = _dot(y_in, wout_ref[...])
        yield
        h1 = _layer_norm(alpha * h_ref[0, rows] + g1_ref[0] * y, l1g_ref[...], l1b_ref[...])
        u2 = (h1 * (1.0 + sc2_ref[0]) + sh2_ref[0]).astype(BF16)
        yield
        gu = _dot(u2, wfi_ref[...])
        yield
        act = (_silu(gu[:, :d_ff]) * gu[:, d_ff:]).astype(BF16)
        yield
        f = _dot(act, wfo_ref[...])
        yield
        o_ref[0, rows] = _layer_norm(alpha * h1 + g2_ref[0] * f, l2g_ref[...], l2b_ref[...])

    gens = [stages(r0) for r0 in range(0, tm, rows_per)]
    live = []
    while gens or live:
        if gens:
            live.append(gens.pop(0))
            for _ in range(2):
                next(live[-1], None)
        for g in list(live):
            if next(g, "done") == "done":
                live.remove(g)


def _post_mla_kernel(a_ref, *rest, alpha, d_ff):
    _post_tail(lambda rows: a_ref[0, rows], *rest, alpha=alpha, d_ff=d_ff)


def _post_gdn_kernel(of_ref, ob_ref, z_ref, nw_ref, *rest, alpha, d_ff, k_heads):
    nw = nw_ref[...]

    def y_in_rows(rows):
        parts = []
        for kh in range(k_heads):
            o2 = of_ref[0, kh, rows] + ob_ref[0, kh, rows]
            for v in range(2):
                o = o2[:, v * GDN_DV:(v + 1) * GDN_DV]
                vh = 2 * kh + v
                z = z_ref[0, rows, vh * GDN_DV:(vh + 1) * GDN_DV]
                parts.append((_rms_norm(o, nw) * _silu(z)).astype(BF16))
        return jnp.concatenate(parts, axis=1)

    _post_tail(y_in_rows, *rest, alpha=alpha, d_ff=d_ff)


def _post_call(mixer_in, h, mod, w_out, l1g, l1b, wfi, wfo, l2g, l2b, *, tm, alpha, gdn_norm=None):
    b, t, d = h.shape
    d_ff = wfo.shape[0]
    assert t % tm == 0
    row = lambda w: pl.BlockSpec((1, tm, w), lambda bb, i: (bb, i, 0))
    if gdn_norm is None:
        (a,) = mixer_in
        kern = functools.partial(_post_mla_kernel, alpha=alpha, d_ff=d_ff)
        head_specs = [row(a.shape[2])]
        head_args = [a]
    else:
        o_f, o_b, z = mixer_in
        kh = o_f.shape[1]
        kern = functools.partial(_post_gdn_kernel, alpha=alpha, d_ff=d_ff, k_heads=kh)
        hm = pl.BlockSpec((1, kh, tm, 2 * GDN_DV), lambda bb, i: (bb, 0, i, 0))
        head_specs = [hm, hm, row(z.shape[2]), _const_spec(gdn_norm.shape)]
        head_args = [o_f, o_b, z, gdn_norm]
    return pl.pallas_call(
        kern,
        grid=(b, t // tm),
        in_specs=head_specs + [
            row(d),
            _mod_spec(d, 2), _mod_spec(d, 3), _mod_spec(d, 4), _mod_spec(d, 5),
            _const_spec(w_out.shape), _const_spec(l1g.shape), _const_spec(l1b.shape),
            _const_spec(wfi.shape), _const_spec(wfo.shape),
            _const_spec(l2g.shape), _const_spec(l2b.shape),
        ],
        out_specs=row(d),
        out_shape=jax.ShapeDtypeStruct((b, t, d), F32),
        compiler_params=_cparams(("parallel", "parallel")),
        name="post_mla" if gdn_norm is None else "post_gdn",
    )(*head_args, h, mod, mod, mod, mod, w_out, l1g, l1b, wfi, wfo, l2g, l2b)


def _chunk_cumsum(g, suffix):
    n = g.shape[0]
    pos = lax.broadcasted_iota(jnp.int32, g.shape, 0) % CHUNK
    s = 1
    while s < CHUNK:
        if suffix:
            shifted = pltpu.roll(g, n - s, axis=0)
            ok = pos < CHUNK - s
        else:
            shifted = pltpu.roll(g, s, axis=0)
            ok = pos >= s
        g = g + jnp.where(ok, shifted, 0.0)
        s *= 2
    return g


CONV_ROWS = 256
CONV_COLS = 256


def _gdn_in_kernel(hp_ref, hc_ref, hn_ref, sh_ref, sc_ref, w_ref, wba_ref, cw_ref, al_ref, dtb_ref,
                   q_ref, k_ref, kt_ref, v_ref, gt_ref, gr_ref, z_ref, xe_ref, y_ref, *, k_heads, tm, z_w):
    i = pl.program_id(1)
    n_i = pl.num_programs(1)
    kw = k_heads * GDN_DK
    c_all = xe_ref.shape[1]
    n_chunks = tm // CHUNK
    half = GDN_CONV // 2
    h_ext = jnp.concatenate([hp_ref[0], hc_ref[0], hn_ref[0]], axis=0)
    u = (h_ext * (1.0 + sc_ref[0]) + sh_ref[0]).astype(BF16)
    keep_prev = jnp.where(i > 0, 1.0, 0.0)
    keep_next = jnp.where(i < n_i - 1, 1.0, 0.0)

    def project(c0):
        return _dot(u, w_ref[:, c0:c0 + CONV_COLS])

    def conv_tile(p, c0):
        cols = slice(c0, c0 + CONV_COLS)
        xe_ref[0:CONV_HALO, cols] = p[0:CONV_HALO] * keep_prev
        xe_ref[CONV_HALO:CONV_HALO + tm, cols] = p[CONV_HALO:CONV_HALO + tm]
        xe_ref[CONV_HALO + tm:, cols] = p[CONV_HALO + tm:] * keep_next
        w = cw_ref[:, cols]
        for r0 in range(0, tm, CONV_ROWS):
            acc = None
            for j in range(GDN_CONV):
                start = CONV_HALO - half + j + r0
                term = xe_ref[start:start + CONV_ROWS, cols] * w[j:j + 1, :]
                acc = term if acc is None else acc + term
            y_ref[r0:r0 + CONV_ROWS, cols] = _silu(acc)

    tiles = list(range(0, c_all, CONV_COLS))
    nxt = project(tiles[0])
    for t_idx, c0 in enumerate(tiles):
        cur = nxt
        if t_idx + 1 < len(tiles):
            nxt = project(tiles[t_idx + 1])
        else:
            u_cur = u[CONV_HALO:CONV_HALO + tm]
            nxt = _dot(u_cur, w_ref[:, c_all:])
            ba = _dot(u_cur, wba_ref[...])
        conv_tile(cur, c0)
    z_ref[0] = nxt
    for kh in range(k_heads):
        qh = y_ref[:, kh * GDN_DK:(kh + 1) * GDN_DK]
        q_ref[0, kh] = (qh * lax.rsqrt(jnp.sum(qh * qh, axis=-1, keepdims=True) + RMS_EPS)
                        * (GDN_DK ** -0.5)).astype(BF16)
        kk = y_ref[:, kw + kh * GDN_DK:kw + (kh + 1) * GDN_DK]
        kn = kk * lax.rsqrt(jnp.sum(kk * kk, axis=-1, keepdims=True) + RMS_EPS)
        k_ref[0, kh] = kn.astype(BF16)
        knt = kn.T.astype(BF16)
        for n in range(n_chunks):
            kt_ref[0, kh, n] = knt[:, n * CHUNK:(n + 1) * CHUNK]
        v_ref[0, kh] = y_ref[:, 2 * kw + kh * 2 * GDN_DV:2 * kw + (kh + 1) * 2 * GDN_DV].astype(BF16)
    lane = lax.broadcasted_iota(jnp.int32, ba.shape, 1) % LANES
    beta = jax.nn.sigmoid(ba)
    g = -jnp.exp(al_ref[...]) * jax.nn.softplus(ba + dtb_ref[...])
    g = jnp.where((lane >= 4) & (lane < 8), g, 0.0)
    gcum = jnp.where(lane < 6, _chunk_cumsum(g, False), _chunk_cumsum(g, True))
    gates = jnp.where(lane < 4, beta, gcum)
    for kh in range(k_heads):
        gk = gates[:, kh * LANES:(kh + 1) * LANES]
        gt_ref[0, kh] = gk
        rows = gk.T[0:8, :]
        for n in range(n_chunks):
            piece = rows[:, n * CHUNK:(n + 1) * CHUNK]
            gr_ref[0, kh, n] = jnp.concatenate([piece, piece], axis=1)


def _gdn_in_call(h, mod, w, w_ba, conv_w, a_log_ext, dtb_ext, *, tm, k_heads, z_w):
    b, t, d = h.shape
    c = conv_w.shape[1]
    assert t % tm == 0 and tm % CHUNK == 0 and tm % CONV_HALO == 0
    assert tm % CONV_ROWS == 0 and c % CONV_COLS == 0
    r = tm // CONV_HALO
    nblk8 = t // CONV_HALO
    ncb = tm // CHUNK
    kern = functools.partial(_gdn_in_kernel, k_heads=k_heads, tm=tm, z_w=z_w)
    hm = lambda wd: pl.BlockSpec((1, k_heads, tm, wd), lambda bb, i: (bb, 0, i, 0))
    per_chunk = lambda rr, wd: pl.BlockSpec((1, k_heads, ncb, rr, wd), lambda bb, i: (bb, 0, i, 0, 0))
    return pl.pallas_call(
        kern,
        grid=(b, t // tm),
        in_specs=[
            pl.BlockSpec((1, CONV_HALO, d), lambda bb, i: (bb, jnp.maximum(i * r - 1, 0), 0)),
            pl.BlockSpec((1, tm, d), lambda bb, i: (bb, i, 0)),
            pl.BlockSpec((1, CONV_HALO, d), lambda bb, i: (bb, jnp.minimum((i + 1) * r, nblk8 - 1), 0)),
            _mod_spec(d, 0), _mod_spec(d, 1),
            _const_spec(w.shape), _const_spec(w_ba.shape), _const_spec(conv_w.shape),
            _const_spec(a_log_ext.shape), _const_spec(dtb_ext.shape),
        ],
        out_specs=[hm(GDN_DK), hm(GDN_DK), per_chunk(GDN_DK, CHUNK), hm(2 * GDN_DV), hm(LANES),
                   per_chunk(8, LANES), pl.BlockSpec((1, tm, z_w), lambda bb, i: (bb, i, 0))],
        out_shape=[
            jax.ShapeDtypeStruct((b, k_heads, t, GDN_DK), BF16),
            jax.ShapeDtypeStruct((b, k_heads, t, GDN_DK), BF16),
            jax.ShapeDtypeStruct((b, k_heads, t // CHUNK, GDN_DK, CHUNK), BF16),
            jax.ShapeDtypeStruct((b, k_heads, t, 2 * GDN_DV), BF16),
            jax.ShapeDtypeStruct((b, k_heads, t, LANES), F32),
            jax.ShapeDtypeStruct((b, k_heads, t // CHUNK, 8, LANES), F32),
            jax.ShapeDtypeStruct((b, t, z_w), F32),
        ],
        scratch_shapes=[pltpu.VMEM((tm + 2 * CONV_HALO, c), F32), pltpu.VMEM((tm, c), F32)],
        compiler_params=_cparams(("parallel", "parallel")),
        name="gdn_in",
    )(h, h, h, mod, mod, w, w_ba, conv_w, a_log_ext, dtb_ext)


def _dot_split(a_hi, a_lo, b_hi, b_lo):
    return _dot(jnp.concatenate([a_hi, a_lo], axis=1), jnp.concatenate([b_hi, b_lo, b_hi, b_lo], axis=0))


def _gdn_scan_kernel(qf_ref, kf_ref, ktf_ref, vf_ref, gf_ref, grf_ref,
                     qb_ref, kb_ref, ktb_ref, vb_ref, gb_ref, grb_ref, s0_ref,
                     of_ref, ob_ref, sfin_ref, s_ref, *, hg):
    n = pl.program_id(2)

    @pl.when(n == 0)
    def _():
        s_ref[...] = s0_ref[0]

    c = CHUNK
    row = lax.broadcasted_iota(jnp.int32, (c, LANES), 0)
    col = lax.broadcasted_iota(jnp.int32, (c, LANES), 1) % c
    dirs = ((qf_ref, kf_ref, ktf_ref, vf_ref, gf_ref, grf_ref), (qb_ref, kb_ref, ktb_ref, vb_ref, gb_ref, grb_ref))

    eye = jnp.where(row == col, 1.0, 0.0)
    zero_rows = jnp.zeros((c, 2 * GDN_DV), BF16)
    tri_masks = ((row >= col, row > col), (row <= col, row < col))
    pairs = []
    blk_r, blk_c = row, col
    b = 1
    while b < c:
        pair = (blk_r // 2 == blk_c // 2) & (blk_r != blk_c)
        pairs.append(pair if b == 1 else jnp.where(pair, 1.0, 0.0).astype(BF16))
        blk_r, blk_c = blk_r // 2, blk_c // 2
        b *= 2

    def stages(heads):
        chains = []
        for j in heads:
            for d in range(2):
                q_ref, k_ref, kt_ref, v_ref, g_ref, gr_ref = dirs[d]
                qc, kc, v2, gt = q_ref[0, j], k_ref[0, j], v_ref[0, j], g_ref[0, j]
                gtt = gr_ref[0, j, 0]
                ktc = kt_ref[0, j, 0]
                kq = _dot_nt(jnp.concatenate([kc, qc], axis=0), jnp.concatenate([kc, kc], axis=0))
                for v in range(2):
                    chains.append(dict(j=j, d=d, v=v, qc=qc, kc=kc, vv=v2[:, v * GDN_DV:(v + 1) * GDN_DV],
                                       gt=gt, gtt=gtt, ktc=ktc, kk=kq[:c], qk=kq[c:]))
            yield
        for i, ch in enumerate(chains):
            slot = 2 * ch["d"] + ch["v"]
            incl, strict = tri_masks[ch["d"]]
            beta_col = jnp.broadcast_to(ch["gt"][:, slot:slot + 1], (c, LANES))
            g_col = jnp.broadcast_to(ch["gt"][:, 4 + slot:5 + slot], (c, LANES))
            g_row = ch["gtt"][4 + slot:5 + slot, :]
            decay = jnp.where(incl, jnp.exp(jnp.where(incl, g_col - g_row, 0.0)), 0.0)
            eg_col = jnp.exp(g_col)
            ch["g_row"] = g_row
            lmat = jnp.where(strict, ch["kk"] * beta_col * decay, 0.0)
            ch.update(
                lm=lmat.astype(BF16), tinv=eye - jnp.where(pairs[0], lmat, 0.0),
                attn=(ch["qk"] * decay)[:, :c].astype(BF16),
                eg_col=eg_col, g_col=g_col,
                rhs=jnp.concatenate([ch["vv"].astype(F32) * beta_col,
                                     ch["kc"].astype(F32) * (beta_col * eg_col)], axis=1).astype(BF16))
            if i % 4 == 3:
                yield
        for pair_bf in pairs[1:]:
            for ch in chains:
                ch["t_hi"], ch["t_lo"] = _split_bf16(ch["tinv"])
                ch["w"] = _dot(ch["lm"] * pair_bf, jnp.concatenate([ch["t_hi"], ch["t_lo"]], axis=0))
            yield
            for ch in chains:
                w_hi, w_lo = _split_bf16(ch["w"])
                ch["tinv"] = ch["tinv"] - _dot_split(ch["t_hi"], ch["t_lo"], w_hi, w_lo)
            yield
        for ch in chains:
            t_hi, t_lo = _split_bf16(ch["tinv"])
            r = ch["rhs"]
            ch["x"] = _dot(jnp.concatenate([t_hi, t_lo], axis=1),
                           jnp.concatenate([r, zero_rows, r, zero_rows], axis=0))
        yield
        for ch in chains:
            ch["s_idx"] = ch["j"] * 4 + 2 * ch["d"] + ch["v"]
            ch["s_old"] = s_ref[ch["s_idx"]]
            ch["wq"] = _dot(jnp.concatenate([ch["x"][:, GDN_DV:].astype(BF16), ch["qc"]], axis=0),
                            ch["s_old"].astype(BF16))
        yield
        for ch in chains:
            g_col = ch["g_col"]
            ch["g_last"] = g_col[0:1, :] if ch["d"] == 1 else g_col[c - 1:c, :]
            v_new = (ch["x"][:, :GDN_DV] - ch["wq"][:c]).astype(BF16)
            dec_row = jnp.exp(ch["g_last"] - ch["g_row"])[:, :c]
            k_dec_t = (ch["ktc"].astype(F32) * dec_row).astype(BF16)
            ch["av"] = _dot(ch["attn"], v_new)
            ch["kv"] = _dot(k_dec_t, v_new)
        yield
        outs = {}
        for ch in chains:
            s_ref[ch["s_idx"]] = ch["s_old"] * jnp.exp(ch["g_last"]) + ch["kv"]
            outs[ch["j"], ch["d"], ch["v"]] = ch["eg_col"] * ch["wq"][c:] + ch["av"]
        for j in heads:
            for d, o_ref in enumerate((of_ref, ob_ref)):
                o_ref[0, j] = jnp.concatenate([outs[j, d, 0], outs[j, d, 1]], axis=1)

    for _ in stages(list(range(hg))):
        pass

    @pl.when(n == pl.num_programs(2) - 1)
    def _():
        sfin_ref[0] = s_ref[...]


def _gdn_scan_call(q, k, kt, v, gates, gate_rows, s0, *, hg):
    b, kh, t, _ = q.shape
    assert kh % hg == 0 and t % CHUNK == 0
    nc = t // CHUNK
    ng = kh // hg
    fwd = lambda w: pl.BlockSpec((1, hg, CHUNK, w), lambda bb, g, n: (bb, g, n, 0))
    bwd = lambda w: pl.BlockSpec((1, hg, CHUNK, w), lambda bb, g, n: (bb, g, nc - 1 - n, 0))
    fwd_c = lambda r, w: pl.BlockSpec((1, hg, 1, r, w), lambda bb, g, n: (bb, g, n, 0, 0))
    bwd_c = lambda r, w: pl.BlockSpec((1, hg, 1, r, w), lambda bb, g, n: (bb, g, nc - 1 - n, 0, 0))
    st = pl.BlockSpec((1, hg * 4, GDN_DK, GDN_DV), lambda bb, g, n: (bb, g, 0, 0))
    kern = functools.partial(_gdn_scan_kernel, hg=hg)
    return pl.pallas_call(
        kern,
        grid=(b, ng, nc),
        in_specs=[fwd(GDN_DK), fwd(GDN_DK), fwd_c(GDN_DK, CHUNK), fwd(2 * GDN_DV), fwd(LANES), fwd_c(8, LANES),
                  bwd(GDN_DK), bwd(GDN_DK), bwd_c(GDN_DK, CHUNK), bwd(2 * GDN_DV), bwd(LANES), bwd_c(8, LANES),
                  st],
        out_specs=[fwd(2 * GDN_DV), bwd(2 * GDN_DV), st],
        out_shape=[
            jax.ShapeDtypeStruct((b, kh, t, 2 * GDN_DV), F32),
            jax.ShapeDtypeStruct((b, kh, t, 2 * GDN_DV), F32),
            jax.ShapeDtypeStruct(s0.shape, F32),
        ],
        scratch_shapes=[pltpu.VMEM((hg * 4, GDN_DK, GDN_DV), F32)],
        compiler_params=_cparams(("parallel", "parallel", "arbitrary")),
        name="gdn_scan",
    )(q, k, kt, v, gates, gate_rows, q, k, kt, v, gates, gate_rows, s0)


def _rope_tables(t, q_scale):
    assert t % GRID_W == 0
    rows = t // GRID_W
    n_freq = MLA_ROPE // 4
    inv_freq = ROPE_THETA ** (-(2.0 * jnp.arange(n_freq, dtype=F32)) / (MLA_ROPE // 2))
    ar = jnp.arange(rows, dtype=F32)[:, None] * inv_freq
    ac = jnp.arange(GRID_W, dtype=F32)[:, None] * inv_freq
    by_row = lambda x: jnp.broadcast_to(x[:, None, :], (rows, GRID_W, n_freq)).reshape(t, n_freq)
    by_col = lambda x: jnp.broadcast_to(x[None, :, :], (rows, GRID_W, n_freq)).reshape(t, n_freq)
    cos_r, sin_r, cos_c, sin_c = jnp.cos(ar), jnp.sin(ar), jnp.cos(ac), jnp.sin(ac)
    by_row_t = lambda x: jnp.broadcast_to(x.T[:, :, None], (n_freq, rows, GRID_W)).reshape(n_freq, t)
    by_col_t = lambda x: jnp.broadcast_to(x.T[:, None, :], (n_freq, rows, GRID_W)).reshape(n_freq, t)
    crt, srt, cct, sct = by_row_t(cos_r), by_row_t(sin_r), by_col_t(cos_c), by_col_t(sin_c)
    base_t = jnp.concatenate([crt, crt, cct, cct, -srt, srt, -sct, sct], axis=0)
    return q_scale * base_t, base_t.T


def _no_rope_tables(t, q_scale):
    km = jnp.concatenate([jnp.ones((t, MLA_ROPE), F32), jnp.zeros((t, MLA_ROPE), F32)], axis=1)
    qrt = q_scale * jnp.concatenate([jnp.ones((MLA_ROPE, t), F32), jnp.zeros((MLA_ROPE, t), F32)], axis=0)
    return qrt, km


def _swap_perm():
    f = MLA_ROPE // 4
    return jnp.array(list(range(f, 2 * f)) + list(range(0, f)) + list(range(3 * f, 4 * f)) + list(range(2 * f, 3 * f)))


def _mla_layer(h_lat, h_ctx, mod_lat, mod_ctx, ln, w_in, q_norm, kv_norm, w_q_up, w_kv_up, w_out, *, alpha, last):
    bsz, seq, d = h_lat.shape
    tctx = h_ctx.shape[1]
    heads = d // 128
    q_rank, kv_rank = q_norm.shape[0], kv_norm.shape[0]
    tm_lat, tm_ctx = min(TOKEN_TILE, seq), min(TOKEN_TILE, tctx)
    q_scale = (MLA_NOPE + MLA_ROPE) ** -0.5 * math.log2(math.e)
    perm = _swap_perm()
    rope_cols = w_in[:, q_rank + kv_rank:]
    w_in_ext = jnp.concatenate([w_in, rope_cols[:, perm]], axis=1).astype(BF16)
    wq = w_q_up.reshape(q_rank, heads, MLA_NOPE + MLA_ROPE)
    wq_ext = jnp.concatenate([wq, wq[:, :, MLA_NOPE:][:, :, perm]], axis=2)
    wqt = wq_ext.reshape(q_rank, heads * MLA_QK).T.astype(BF16)
    wkv = w_kv_up.reshape(kv_rank, heads, MLA_NOPE + MLA_V)
    wk = wkv[:, :, :MLA_NOPE].reshape(kv_rank, -1).astype(BF16)
    wvt = wkv[:, :, MLA_NOPE:].reshape(kv_rank, -1).T.astype(BF16)
    qn, kvn = q_norm.reshape(1, -1), kv_norm.reshape(1, -1)
    qrt_l, km_l = _rope_tables(seq, q_scale)
    qrt_c, km_c = _no_rope_tables(tctx, q_scale)
    q_l, k_l, v_l = _mla_proj_call(h_lat, mod_lat, w_in_ext, qn, kvn, wqt, wk, wvt, qrt_l, km_l,
                                   tm=tm_lat, heads=heads, q_scale=q_scale)
    q_c, k_c, v_c = _mla_proj_call(h_ctx, mod_ctx, w_in_ext, qn, kvn, wqt, wk, wvt, qrt_c, km_c,
                                   tm=tm_ctx, heads=heads, q_scale=q_scale)
    o_l = _attn_call(q_l, k_c, v_c, k_l, v_l, tq=min(ATTN_Q_TILE, seq))
    w_out = w_out.astype(BF16)
    h_lat_new = _post_call((o_l,), h_lat, mod_lat, w_out, *ln, tm=tm_lat, alpha=alpha)
    if not last:
        o_c = _attn_call(q_c, k_c, v_c, tq=tm_ctx)
        h_ctx = _post_call((o_c,), h_ctx, mod_ctx, w_out, *ln, tm=tm_ctx, alpha=alpha)
    return h_lat_new, h_ctx


def _gdn_layer(h_lat, h_ctx, mod_lat, mod_ctx, ln, w_in, conv_w, a_log, dt_bias, norm_w, w_out, *, alpha, last):
    bsz, seq, d = h_lat.shape
    tctx = h_ctx.shape[1]
    k_heads = d // 128
    v_heads = 2 * k_heads
    kw, vw = k_heads * GDN_DK, v_heads * GDN_DV
    qkv_w = 2 * kw + vw
    tm_lat, tm_ctx = min(TOKEN_TILE, seq), min(TOKEN_TILE, tctx)
    w_ba = w_in[:, qkv_w + vw:].reshape(d, 2, 2, k_heads, 2)
    w_ba = jnp.transpose(w_ba, (0, 3, 2, 1, 4)).reshape(d, k_heads, 8)
    w_ba = jnp.pad(w_ba, ((0, 0), (0, 0), (0, LANES - 8))).reshape(d, k_heads * LANES)
    w_main = w_in[:, :qkv_w + vw].astype(BF16)
    w_ba = w_ba.astype(BF16)

    def gate_lanes(p):
        p = jnp.transpose(p.astype(F32).reshape(2, k_heads, 2), (1, 0, 2)).reshape(k_heads, 4)
        p = jnp.pad(p, ((0, 0), (4, LANES - 8)))
        return p.reshape(1, k_heads * LANES)

    al_ext, dtb_ext = gate_lanes(a_log), gate_lanes(dt_bias)
    s = jnp.zeros((bsz, k_heads * 4, GDN_DK, GDN_DV), F32)
    streams = []
    for h_in, m_in, tm in ((h_ctx, mod_ctx, tm_ctx), (h_lat, mod_lat, tm_lat)):
        qh, kh, kt, vh, gates, gate_rows, z = _gdn_in_call(h_in, m_in, w_main, w_ba, conv_w, al_ext, dtb_ext,
                                                           tm=tm, k_heads=k_heads, z_w=vw)
        o_f, o_b, s = _gdn_scan_call(qh, kh, kt, vh, gates, gate_rows, s, hg=min(SCAN_KEY_HEADS, k_heads))
        streams.append((o_f, o_b, z))
    w_out = w_out.astype(BF16)
    nw = norm_w.reshape(1, -1)
    h_lat_new = _post_call(streams[1], h_lat, mod_lat, w_out, *ln, tm=tm_lat, alpha=alpha, gdn_norm=nw)
    if not last:
        h_ctx = _post_call(streams[0], h_ctx, mod_ctx, w_out, *ln, tm=tm_ctx, alpha=alpha, gdn_norm=nw)
    return h_lat_new, h_ctx


def kernel(x, c, ctx, c_ctx, w_mod, b_mod, ln1_g, ln1_b, ln2_g, ln2_b, w_ffn_in, w_ffn_out, mla_w_in, mla_q_norm, mla_kv_norm, mla_w_q_up, mla_w_kv_up, mla_w_out, gdn_w_in, gdn_conv, gdn_a_log, gdn_dt_bias, gdn_norm, gdn_w_out):
    bsz, _, d = x.shape
    depth = w_mod.shape[0]
    alpha = (2.0 * depth) ** 0.25

    cc = jnp.zeros((8, d), F32).at[:bsz].set(c).at[bsz].set(c_ctx)
    mod = _mod_call(cc, w_mod, b_mod)

    row = lambda a, i: a[i].reshape(1, -1)
    h_lat, h_ctx = x, ctx
    for i in range(depth):
        last = i == depth - 1
        mod_lat = mod[i, :bsz].reshape(bsz, 1, N_MOD * d)
        mod_ctx = jnp.broadcast_to(mod[i, bsz].reshape(1, 1, N_MOD * d), (bsz, 1, N_MOD * d))
        ln = (row(ln1_g, i), row(ln1_b, i), w_ffn_in[i].astype(BF16), w_ffn_out[i].astype(BF16),
              row(ln2_g, i), row(ln2_b, i))
        j = i // 2
        if i % 2 == 0:
            h_lat, h_ctx = _mla_layer(h_lat, h_ctx, mod_lat, mod_ctx, ln, mla_w_in[j], mla_q_norm[j],
                                      mla_kv_norm[j], mla_w_q_up[j], mla_w_kv_up[j], mla_w_out[j],
                                      alpha=alpha, last=last)
        else:
            h_lat, h_ctx = _gdn_layer(h_lat, h_ctx, mod_lat, mod_ctx, ln, gdn_w_in[j], gdn_conv[j],
                                      gdn_a_log[j], gdn_dt_bias[j], gdn_norm[j], gdn_w_out[j],
                                      alpha=alpha, last=last)
    return h_lat
```
